```python
import jax, jax.numpy as jnp
from jax import lax
import numpy as np

D_MODEL = 2048
BATCH = 4
SEQ = 8192
DEPTH = 2

GRID_W = 64
CTX_LEN = 256
D_MIX = D_MODEL
A_GROUPS = 4
A_GROUP_DIM = 128
A_WIDTH = A_GROUPS * A_GROUP_DIM
A_CHUNK = 128
B_HEADS = 6
B_HEAD_DIM = 128
B_WIDTH = B_HEADS * B_HEAD_DIM
DN_CHUNK = 64
DN_CONV = 5
C_HEADS = 6
C_NOPE = 128
C_ROPE = 64
C_V = 128
C_WIDTH = C_HEADS * C_V
Q_LORA = 512
KV_LORA = 256
Q_BLOCK = 128
ROPE_BASE = 10000.0
ROPE_AXIS_FREQS = C_ROPE // 4
IN_A = 2 * A_WIDTH
IN_B = 4 * B_WIDTH + 4 * B_HEADS
IN_C = Q_LORA + KV_LORA + C_ROPE
IN_COLS = IN_A + IN_B + IN_C
D_FF = 5632
N_EXPERTS = 8
TOP_K = 2
E_FF = 7168
N_DENSE = (DEPTH + 1) // 2
N_MOE = DEPTH // 2
NORM_EPS = 1e-6

kernel_name = 'hybrid_gmlp_deltanet_mla_moe_dit'


def rmsnorm(x, g):
    xf = x.astype(jnp.float32)
    y = xf * lax.rsqrt(jnp.mean(xf * xf, axis=-1, keepdims=True) + NORM_EPS)
    return (y * g.astype(jnp.float32)).astype(x.dtype)


def layernorm(x, g):
    xf = x.astype(jnp.float32)
    xc = xf - jnp.mean(xf, axis=-1, keepdims=True)
    y = xc * lax.rsqrt(jnp.mean(xc * xc, axis=-1, keepdims=True) + NORM_EPS)
    return (y * g.astype(jnp.float32)).astype(x.dtype)


def l2norm(x):
    xf = x.astype(jnp.float32)
    return xf * lax.rsqrt(jnp.sum(xf * xf, axis=-1, keepdims=True) + NORM_EPS)


def modulate(h, shift, scale):
    return h * (1 + scale) + shift


def axial_rope(n):
    rows = n // GRID_W
    row = jnp.repeat(jnp.arange(rows, dtype=jnp.float32), GRID_W)
    col = jnp.tile(jnp.arange(GRID_W, dtype=jnp.float32), rows)
    inv = jnp.power(ROPE_BASE, -jnp.arange(ROPE_AXIS_FREQS, dtype=jnp.float32) / ROPE_AXIS_FREQS)
    ar = row[:, None] * inv
    ac = col[:, None] * inv
    ang = jnp.concatenate([ar, ar, ac, ac], axis=-1)
    return jnp.cos(ang), jnp.sin(ang)


def apply_axial_rope(x, cos, sin):
    r1, r2, c1, c2 = jnp.split(x, 4, axis=-1)
    rot = jnp.concatenate([-r2, r1, -c2, c1], axis=-1)
    return (x.astype(jnp.float32) * cos + rot.astype(jnp.float32) * sin).astype(x.dtype)


def chunk_sgu(za, norm_g, w_s, b_s):
    b, n, _ = za.shape
    z = jax.nn.gelu(za)
    u = z[..., :A_WIDTH]
    v = layernorm(z[..., A_WIDTH:].reshape(b, n, A_GROUPS, A_GROUP_DIM), norm_g.reshape(A_GROUPS, A_GROUP_DIM))
    v = v.reshape(b, n // A_CHUNK, A_CHUNK, A_GROUPS, A_GROUP_DIM)
    mixed = jnp.einsum('gij,bmjgc->bmigc', w_s, v) + b_s.T[None, None, :, :, None]
    return u * mixed.reshape(b, n, A_WIDTH)


def short_conv_silu(x, w):
    n = x.shape[1]
    half = DN_CONV // 2
    xp = jnp.pad(x, ((0, 0), (half, half), (0, 0)))
    y = xp[:, 0:n] * w[0]
    for i in range(1, DN_CONV):
        y = y + xp[:, i:i + n] * w[i]
    return jax.nn.silu(y)


def deltanet_prep(zb, conv_w, a_log, dt_bias):
    b, n, _ = zb.shape
    qkv = short_conv_silu(zb[..., :3 * B_WIDTH], conv_w).reshape(b, n, 3, B_HEADS, B_HEAD_DIM)
    q = l2norm(qkv[:, :, 0])
    k = l2norm(qkv[:, :, 1])
    v = qkv[:, :, 2]
    gate = zb[..., 3 * B_WIDTH:4 * B_WIDTH]
    a = zb[..., 4 * B_WIDTH:4 * B_WIDTH + 2 * B_HEADS].reshape(b, n, 2, B_HEADS).astype(jnp.float32)
    bt = zb[..., 4 * B_WIDTH + 2 * B_HEADS:].reshape(b, n, 2, B_HEADS).astype(jnp.float32)
    g = -jnp.exp(a_log.astype(jnp.float32)) * jax.nn.softplus(a + dt_bias.astype(jnp.float32))
    beta = jax.nn.sigmoid(bt)
    return q, k, v, gate, g, beta


def gated_delta_chunked(q, k, v, g, beta, s0):
    f32 = jnp.float32
    b, n, h, dk = q.shape
    dv = v.shape[-1]
    nc = n // DN_CHUNK

    def chunks(t):
        t = t.astype(f32).reshape((b, nc, DN_CHUNK, h) + t.shape[3:])
        return jnp.moveaxis(t, (1, 3), (0, 2))

    q = chunks(q) * (dk ** -0.5)
    k = chunks(k)
    v = chunks(v)
    beta = chunks(beta)
    gc = jnp.cumsum(chunks(g), axis=-1)
    idx = jnp.arange(DN_CHUNK)
    incl = idx[:, None] >= idx[None, :]
    strict = idx[:, None] > idx[None, :]
    diff = gc[..., :, None] - gc[..., None, :]
    decay = jnp.where(incl, jnp.exp(jnp.where(incl, diff, 0.0)), 0.0)
    kb = k * beta[..., None]
    a_mat = jnp.where(strict, jnp.einsum('nbhik,nbhjk->nbhij', kb, k) * decay, 0.0) + jnp.eye(DN_CHUNK, dtype=f32)
    rhs = jnp.concatenate([v * beta[..., None], kb * jnp.exp(gc)[..., None]], axis=-1)
    sol = lax.linalg.triangular_solve(a_mat, rhs, left_side=True, lower=True, unit_diagonal=True)
    u, w = sol[..., :dv], sol[..., dv:]
    qk = jnp.einsum('nbhik,nbhjk->nbhij', q, k) * decay
    q_g = q * jnp.exp(gc)[..., None]
    k_g = k * jnp.exp(gc[..., -1:] - gc)[..., None]
    g_last = jnp.exp(gc[..., -1])

    def step(s, xs):
        qk_i, q_i, k_i, u_i, w_i, gl_i = xs
        v_new = u_i - jnp.einsum('bhck,bhkv->bhcv', w_i, s)
        o = jnp.einsum('bhck,bhkv->bhcv', q_i, s) + jnp.einsum('bhij,bhjv->bhiv', qk_i, v_new)
        s = s * gl_i[..., None, None] + jnp.einsum('bhck,bhcv->bhkv', k_i, v_new)
        return s, o

    s_fin, o = lax.scan(step, s0.astype(f32), (qk, q_g, k_g, u, w, g_last))
    o = jnp.moveaxis(o, (0, 2), (1, 3)).reshape(b, n, h, dv)
    return o, s_fin


def deltanet_bidir(q, k, v, g, beta, s0_f, s0_b):
    o_f, s_f = gated_delta_chunked(q, k, v, g[:, :, 0], beta[:, :, 0], s0_f)
    fl = lambda t: jnp.flip(t, axis=1)
    o_b, s_b = gated_delta_chunked(fl(q), fl(k), fl(v), fl(g[:, :, 1]), fl(beta[:, :, 1]), s0_b)
    return o_f + fl(o_b), s_f, s_b


def deltanet_out(o, gate, norm_g):
    b, n = o.shape[:2]
    o = rmsnorm(o.astype(gate.dtype), norm_g) * jax.nn.silu(gate.reshape(b, n, B_HEADS, B_HEAD_DIM))
    return o.reshape(b, n, B_WIDTH)


def mla_q(zc, q_norm_g, w_uq, rope):
    b, n, _ = zc.shape
    cq = rmsnorm(zc[..., :Q_LORA], q_norm_g)
    q = (cq @ w_uq).reshape(b, n, C_HEADS, C_NOPE + C_ROPE)
    if rope is not None:
        q_pe = apply_axial_rope(q[..., C_NOPE:], rope[0][:, None, :], rope[1][:, None, :])
        q = jnp.concatenate([q[..., :C_NOPE], q_pe], axis=-1)
    return q


def mla_kv(zc, kv_norm_g, w_ukv, rope):
    b, n, _ = zc.shape
    ckv = rmsnorm(zc[..., Q_LORA:Q_LORA + KV_LORA], kv_norm_g)
    k_pe = zc[..., Q_LORA + KV_LORA:]
    if rope is not None:
        k_pe = apply_axial_rope(k_pe, rope[0], rope[1])
    kv = (ckv @ w_ukv).reshape(b, n, C_HEADS, C_NOPE + C_V)
    k = jnp.concatenate([kv[..., :C_NOPE], jnp.broadcast_to(k_pe[:, :, None, :], (b, n, C_HEADS, C_ROPE))], axis=-1)
    return k, kv[..., C_NOPE:]


def attend(q, k, v):
    s = jnp.einsum('bqhd,bkhd->bhqk', q, k).astype(jnp.float32) * ((C_NOPE + C_ROPE) ** -0.5)
    p = jax.nn.softmax(s, axis=-1).astype(v.dtype)
    return jnp.einsum('bhqk,bkhd->bqhd', p, v)


def attend_blocked(q, k, v):
    b, n, h, d = q.shape
    qb = jnp.moveaxis(q.reshape(b, n // Q_BLOCK, Q_BLOCK, h, d), 1, 0)
    ob = lax.map(lambda qi: attend(qi, k, v), qb)
    return jnp.moveaxis(ob, 0, 1).reshape(b, n, h, -1)


def mixer_sublayer(h, hc, w_in, sgu_norm_g, sgu_w, sgu_b, conv_w, a_log, dt_bias, dn_norm_g,
                   q_norm_g, w_uq, kv_norm_g, w_ukv, w_out, rope, need_ctx):
    b = h.shape[0]
    z = h @ w_in
    za, zb, zl = z[..., :IN_A], z[..., IN_A:IN_A + IN_B], z[..., IN_A + IN_B:]
    if need_ctx:
        zctx = hc @ w_in
        zca = zctx[..., :IN_A]
        zc_bc = zctx[..., IN_A:]
    else:
        zc_bc = hc @ w_in[:, IN_A:]
    zcb, zcc = zc_bc[..., :IN_B], zc_bc[..., IN_B:]
    out_a = chunk_sgu(za, sgu_norm_g, sgu_w, sgu_b)
    s_zero = jnp.zeros((b, B_HEADS, B_HEAD_DIM, B_HEAD_DIM), jnp.float32)
    qc, kc, vc, gatec, gcx, betac = deltanet_prep(zcb, conv_w, a_log, dt_bias)
    oc_b, s_f, s_b = deltanet_bidir(qc, kc, vc, gcx, betac, s_zero, s_zero)
    q, k, v, gate, g, beta = deltanet_prep(zb, conv_w, a_log, dt_bias)
    o_b, _, _ = deltanet_bidir(q, k, v, g, beta, s_f, s_b)
    out_b = deltanet_out(o_b, gate, dn_norm_g)
    k_ctx, v_ctx = mla_kv(zcc, kv_norm_g, w_ukv, None)
    k_lat, v_lat = mla_kv(zl, kv_norm_g, w_ukv, rope)
    q_lat = mla_q(zl, q_norm_g, w_uq, rope)
    out_c = attend_blocked(q_lat, jnp.concatenate([k_lat, k_ctx], axis=1), jnp.concatenate([v_lat, v_ctx], axis=1))
    out_c = out_c.reshape(b, -1, C_WIDTH)
    y = jnp.concatenate([out_a, out_b, out_c], axis=-1) @ w_out
    if not need_ctx:
        return y, None
    oa_c = chunk_sgu(zca, sgu_norm_g, sgu_w, sgu_b)
    ob_c = deltanet_out(oc_b, gatec, dn_norm_g)
    oc_c = attend(mla_q(zcc, q_norm_g, w_uq, None), k_ctx, v_ctx).reshape(b, -1, C_WIDTH)
    yc = jnp.concatenate([oa_c, ob_c, oc_c], axis=-1) @ w_out
    return y, yc


def swiglu(h, wg, wu, wd):
    return (jax.nn.silu(h @ wg) * (h @ wu)) @ wd


def moe_swiglu(h, router, wg, wu, wd):
    logits = (h @ router).astype(jnp.float32)
    probs = jax.nn.softmax(logits, axis=-1)
    top_p, top_i = lax.top_k(probs, TOP_K)
    top_p = top_p / jnp.sum(top_p, axis=-1, keepdims=True)
    flat_e = top_i.reshape(-1)
    order = jnp.argsort(flat_e)
    tok = order // TOP_K
    xs = h[tok]
    sizes = jnp.bincount(flat_e, length=N_EXPERTS).astype(jnp.int32)
    hid = jax.nn.silu(lax.ragged_dot(xs, wg, sizes)) * lax.ragged_dot(xs, wu, sizes)
    ys = lax.ragged_dot(hid, wd, sizes)
    wts = top_p.reshape(-1)[order].astype(ys.dtype)
    return jnp.zeros_like(h).at[tok].add(ys * wts[:, None])


def setup_inputs(seed: int = 0) -> dict:
    key = jax.random.key(seed)
    ks = list(jax.random.split(key, 40))
    f32 = jnp.float32
    def nrm(i, shape, scale):
        return jax.random.normal(ks[i], shape, f32) * scale
    def gain(i, shape):
        return 1.0 + 0.1 * jax.random.normal(ks[i], shape, f32)
    dt = jnp.exp(jax.random.uniform(ks[14], (DEPTH, 2, B_HEADS), f32, np.log(1e-3), np.log(1e-1)))
    return {
        'x': nrm(0, (BATCH, SEQ, D_MODEL), 1.0),
        'c': nrm(1, (BATCH, D_MODEL), 1.0),
        'ctx': nrm(2, (BATCH, CTX_LEN, D_MODEL), 1.0),
        'c_ctx': nrm(3, (D_MODEL,), 1.0),
        'ada_w': nrm(4, (DEPTH, D_MODEL, 6 * D_MODEL), 0.5 * D_MODEL ** -0.5),
        'ada_b': nrm(5, (DEPTH, 6 * D_MODEL), 0.02),
        'norm1_g': gain(6, (DEPTH, D_MODEL)),
        'norm2_g': gain(7, (DEPTH, D_MODEL)),
        'w_in': nrm(8, (DEPTH, D_MODEL, IN_COLS), D_MODEL ** -0.5),
        'sgu_norm_g': gain(9, (DEPTH, A_WIDTH)),
        'sgu_w': nrm(10, (DEPTH, A_GROUPS, A_CHUNK, A_CHUNK), A_CHUNK ** -0.5),
        'sgu_b': nrm(11, (DEPTH, A_GROUPS, A_CHUNK), 0.02),
        'dn_conv_w': nrm(12, (DEPTH, DN_CONV, 3 * B_WIDTH), DN_CONV ** -0.5),
        'dn_a_log': jnp.log(jax.random.uniform(ks[13], (DEPTH, 2, B_HEADS), f32, 1.0, 16.0)),
        'dn_dt_bias': dt + jnp.log(-jnp.expm1(-dt)),
        'dn_norm_g': gain(15, (DEPTH, B_HEAD_DIM)),
        'mla_q_norm_g': gain(16, (DEPTH, Q_LORA)),
        'mla_w_uq': nrm(17, (DEPTH, Q_LORA, C_HEADS * (C_NOPE + C_ROPE)), Q_LORA ** -0.5),
        'mla_kv_norm_g': gain(18, (DEPTH, KV_LORA)),
        'mla_w_ukv': nrm(19, (DEPTH, KV_LORA, C_HEADS * (C_NOPE + C_V)), KV_LORA ** -0.5),
        'w_out': nrm(20, (DEPTH, D_MIX, D_MODEL), D_MIX ** -0.5),
        'ffn_w_gate': nrm(21, (N_DENSE, D_MODEL, D_FF), D_MODEL ** -0.5),
        'ffn_w_up': nrm(22, (N_DENSE, D_MODEL, D_FF), D_MODEL ** -0.5),
        'ffn_w_down': nrm(23, (N_DENSE, D_FF, D_MODEL), D_FF ** -0.5),
        'moe_router': nrm(24, (N_MOE, D_MODEL, N_EXPERTS), D_MODEL ** -0.5),
        'moe_w_gate': nrm(25, (N_MOE, N_EXPERTS, D_MODEL, E_FF), D_MODEL ** -0.5),
        'moe_w_up': nrm(26, (N_MOE, N_EXPERTS, D_MODEL, E_FF), D_MODEL ** -0.5),
        'moe_w_down': nrm(27, (N_MOE, N_EXPERTS, E_FF, D_MODEL), E_FF ** -0.5),
        'final_norm_g': gain(28, (D_MODEL,)),
    }


def reference(x, c, ctx, c_ctx, ada_w, ada_b, norm1_g, norm2_g, w_in, sgu_norm_g, sgu_w, sgu_b,
              dn_conv_w, dn_a_log, dn_dt_bias, dn_norm_g, mla_q_norm_g, mla_w_uq, mla_kv_norm_g, mla_w_ukv,
              w_out, ffn_w_gate, ffn_w_up, ffn_w_down, moe_router, moe_w_gate, moe_w_up, moe_w_down,
              final_norm_g):
    b, n, d = x.shape
    lc = ctx.shape[1]
    rope = axial_rope(n)
    xc = ctx
    for i in range(DEPTH):
        last = i == DEPTH - 1
        mod = jax.nn.silu(c) @ ada_w[i] + ada_b[i]
        mod_c = jax.nn.silu(c_ctx) @ ada_w[i] + ada_b[i]
        sh1, sc1, g1, sh2, sc2, g2 = jnp.split(mod[:, None, :], 6, axis=-1)
        sh1c, sc1c, g1c, sh2c, sc2c, g2c = jnp.split(mod_c, 6, axis=-1)
        h = modulate(rmsnorm(x, norm1_g[i]), sh1, sc1)
        hc = modulate(rmsnorm(xc, norm1_g[i]), sh1c, sc1c)
        y, yc = mixer_sublayer(h, hc, w_in[i], sgu_norm_g[i], sgu_w[i], sgu_b[i], dn_conv_w[i], dn_a_log[i],
                               dn_dt_bias[i], dn_norm_g[i], mla_q_norm_g[i], mla_w_uq[i], mla_kv_norm_g[i],
                               mla_w_ukv[i], w_out[i], rope, not last)
        x = x + g1 * y
        h = modulate(rmsnorm(x, norm2_g[i]), sh2, sc2).reshape(b * n, d)
        if not last:
            xc = xc + g1c * yc
            hc = modulate(rmsnorm(xc, norm2_g[i]), sh2c, sc2c).reshape(b * lc, d)
            h = jnp.concatenate([h, hc], axis=0)
        if i % 2 == 0:
            f = swiglu(h, ffn_w_gate[i // 2], ffn_w_up[i // 2], ffn_w_down[i // 2])
        else:
            f = moe_swiglu(h, moe_router[i // 2], moe_w_gate[i // 2], moe_w_up[i // 2], moe_w_down[i // 2])
        x = x + g2 * f[:b * n].reshape(b, n, d)
        if not last:
            xc = xc + g2c * f[b * n:].reshape(b, lc, d)
    return rmsnorm(x, final_norm_g)
```

```python
import functools
import math

import jax
import jax.numpy as jnp
from jax import lax
from jax.experimental import pallas as pl
from jax.experimental.pallas import tpu as pltpu

F32 = jnp.float32
BF16 = jnp.bfloat16
HIGHEST = lax.Precision.HIGHEST

NORM_EPS = 1e-6
GRID_W = 64
A_GROUPS = 4
A_GROUP_DIM = 128
A_WIDTH = A_GROUPS * A_GROUP_DIM
A_CHUNK = 128
B_HEADS = 6
B_HEAD_DIM = 128
B_WIDTH = B_HEADS * B_HEAD_DIM
DN_CHUNK = 64
DN_CONV = 5
C_HEADS = 6
C_NOPE = 128
C_ROPE = 64
C_V = 128
C_WIDTH = C_HEADS * C_V
Q_LORA = 512
KV_LORA = 256
ROPE_BASE = 10000.0
ROPE_AXIS_FREQS = C_ROPE // 4
N_EXPERTS = 8
TOP_K = 2

LANE = 128
ROW_TILE = 16
VMEM_LIMIT = 56 * 1024 * 1024

ZC_QKV = 0
ZC_GATE = 3 * B_WIDTH
ZC_A = 4 * B_WIDTH
ZC_C = 4 * B_WIDTH + 2 * A_WIDTH
Z_COLS = ZC_C + 1024
C_HEAD_COLS = 3 * LANE


def _cparams(*sem):
    return pltpu.CompilerParams(dimension_semantics=sem, vmem_limit_bytes=VMEM_LIMIT)


def _tile(n, pref):
    if n <= pref:
        return n
    t = (pref // LANE) * LANE
    while n % t:
        t -= LANE
    return t


def _sigmoid(x):
    return 1.0 / (1.0 + jnp.exp(-x))


def _silu(x):
    return x * _sigmoid(x)


def _dot(a, b):
    return jnp.dot(a, b, preferred_element_type=F32)


def _dot_nt(a, b):
    return lax.dot_general(a, b, (((1,), (1,)), ((), ())), preferred_element_type=F32)


def _dot_tn(a, b):
    return lax.dot_general(a, b, (((0,), (0,)), ((), ())), preferred_element_type=F32)


def _split_bf16(a):
    hi = a.astype(BF16)
    lo = (a - hi.astype(F32)).astype(BF16)
    return hi, lo


def _dot_x3(a, b):
    ah, al = _split_bf16(a)
    bh, bl = _split_bf16(b)
    return _dot(ah, bh) + (_dot(ah, bl) + _dot(al, bh))


def _ada_kernel(c_ref, w_ref, b_ref, o_ref):
    c = c_ref[...]
    o_ref[0] = jnp.dot(_silu(c), w_ref[0], preferred_element_type=F32, precision=HIGHEST) + b_ref[0]


def ada_modulation(cc, ada_w, ada_b):
    depth, d, n = ada_w.shape
    tn = 1024
    return pl.pallas_call(
        _ada_kernel,
        grid=(depth, n // tn),
        in_specs=[
            pl.BlockSpec((8, d), lambda i, j: (0, 0)),
            pl.BlockSpec((1, d, tn), lambda i, j: (i, 0, j)),
            pl.BlockSpec((1, 1, tn), lambda i, j: (i, 0, j)),
        ],
        out_specs=pl.BlockSpec((1, 8, tn), lambda i, j: (i, 0, j)),
        out_shape=jax.ShapeDtypeStruct((depth, 8, n), F32),
        compiler_params=_cparams("parallel", "parallel"),
        name="ada_modulation",
    )(cc, ada_w, ada_b.reshape(depth, 1, n))


def _norm_mod_kernel(x_ref, g_ref, sh_ref, sc_ref, o_ref):
    x = x_ref[0]
    y = x * lax.rsqrt(jnp.mean(x * x, axis=-1, keepdims=True) + NORM_EPS) * g_ref[...]
    o_ref[0] = (y * (1.0 + sc_ref[0]) + sh_ref[0]).astype(o_ref.dtype)


def _norm_mod_router_kernel(x_ref, g_ref, sh_ref, sc_ref, r_ref, o_ref, lg_ref):
    x = x_ref[0]
    y = x * lax.rsqrt(jnp.mean(x * x, axis=-1, keepdims=True) + NORM_EPS) * g_ref[...]
    h = y * (1.0 + sc_ref[0]) + sh_ref[0]
    o_ref[0] = h.astype(o_ref.dtype)
    lg_ref[0] = jnp.dot(h, r_ref[...], preferred_element_type=F32, precision=HIGHEST)


def norm_mod(x, g, shift, scale, out_dtype=BF16, router=None):
    b, l, d = x.shape
    tl = min(512, l)
    in_specs = [
        pl.BlockSpec((1, tl, d), lambda i, j: (i, j, 0)),
        pl.BlockSpec((1, d), lambda i, j: (0, 0)),
        pl.BlockSpec((1, 1, d), lambda i, j: (i, 0, 0)),
        pl.BlockSpec((1, 1, d), lambda i, j: (i, 0, 0)),
    ]
    args = [x, g.reshape(1, d), shift.reshape(b, 1, d), scale.reshape(b, 1, d)]
    h_spec = pl.BlockSpec((1, tl, d), lambda i, j: (i, j, 0))
    h_shape = jax.ShapeDtypeStruct((b, l, d), out_dtype)
    if router is None:
        return pl.pallas_call(
            _norm_mod_kernel, grid=(b, l // tl), in_specs=in_specs, out_specs=h_spec, out_shape=h_shape,
            compiler_params=_cparams("parallel", "parallel"), name="norm_mod",
        )(*args)
    in_specs.append(pl.BlockSpec((d, LANE), lambda i, j: (0, 0)))
    return pl.pallas_call(
        _norm_mod_router_kernel, grid=(b, l // tl), in_specs=in_specs,
        out_specs=[h_spec, pl.BlockSpec((1, tl, LANE), lambda i, j: (i, j, 0))],
        out_shape=[h_shape, jax.ShapeDtypeStruct((b, l, LANE), F32)],
        compiler_params=_cparams("parallel", "parallel"), name="norm_mod_router",
    )(*args, router)


def _rmsnorm_kernel(x_ref, g_ref, o_ref):
    x = x_ref[...]
    o_ref[...] = x * lax.rsqrt(jnp.mean(x * x, axis=-1, keepdims=True) + NORM_EPS) * g_ref[...]


def rmsnorm_rows(x, g):
    m, d = x.shape
    tm = min(512, m)
    return pl.pallas_call(
        _rmsnorm_kernel, grid=(m // tm,),
        in_specs=[pl.BlockSpec((tm, d), lambda i: (i, 0)), pl.BlockSpec((1, d), lambda i: (0, 0))],
        out_specs=pl.BlockSpec((tm, d), lambda i: (i, 0)),
        out_shape=jax.ShapeDtypeStruct((m, d), F32),
        compiler_params=_cparams("parallel"), name="final_rmsnorm",
    )(x, g.reshape(1, d))


def _mm_kernel(a_ref, w_ref, o_ref):
    o_ref[...] = _dot(a_ref[...], w_ref[...]).astype(o_ref.dtype)


def matmul(a, w, out_dtype, tm, tn):
    m, k = a.shape
    n = w.shape[1]
    tm, tn = _tile(m, tm), _tile(n, tn)
    return pl.pallas_call(
        _mm_kernel, grid=(m // tm, n // tn),
        in_specs=[pl.BlockSpec((tm, k), lambda i, j: (i, 0)), pl.BlockSpec((k, tn), lambda i, j: (0, j))],
        out_specs=pl.BlockSpec((tm, tn), lambda i, j: (i, j)),
        out_shape=jax.ShapeDtypeStruct((m, n), out_dtype),
        compiler_params=_cparams("parallel", "parallel"), name="matmul",
    )(a, w)


def _swiglu_up_kernel(a_ref, wg_ref, wu_ref, o_ref):
    a = a_ref[...]
    g = _dot(a, wg_ref[...])
    u = _dot(a, wu_ref[...])
    o_ref[...] = (_silu(g) * u).astype(o_ref.dtype)


def swiglu_up(a, wg, wu, tm, tn):
    m, k = a.shape
    n = wg.shape[1]
    tm, tn = _tile(m, tm), _tile(n, tn)
    return pl.pallas_call(
        _swiglu_up_kernel, grid=(m // tm, n // tn),
        in_specs=[pl.BlockSpec((tm, k), lambda i, j: (i, 0)),
                  pl.BlockSpec((k, tn), lambda i, j: (0, j)),
                  pl.BlockSpec((k, tn), lambda i, j: (0, j))],
        out_specs=pl.BlockSpec((tm, tn), lambda i, j: (i, j)),
        out_shape=jax.ShapeDtypeStruct((m, n), BF16),
        compiler_params=_cparams("parallel", "parallel"), name="swiglu_up",
    )(a, wg, wu)


def _down_res_kernel(a_ref, w_ref, x_ref, gate_ref, o_ref, acc_ref):
    k = pl.program_id(2)

    @pl.when(k == 0)
    def _():
        acc_ref[...] = jnp.zeros_like(acc_ref)

    acc_ref[...] += _dot(a_ref[...], w_ref[...])

    @pl.when(k == pl.num_programs(2) - 1)
    def _():
        o_ref[...] = x_ref[...] + gate_ref[0] * acc_ref[...]


def down_residual(a, w, x, gate, rows_per_batch, tm, tn, tk):
    m, kdim = a.shape
    n = w.shape[1]
    tm, tn, tk = _tile(rows_per_batch, tm), _tile(n, tn), _tile(kdim, tk)
    bsz = gate.shape[0]
    return pl.pallas_call(
        _down_res_kernel, grid=(m // tm, n // tn, kdim // tk),
        in_specs=[pl.BlockSpec((tm, tk), lambda i, j, k: (i, k)),
                  pl.BlockSpec((tk, tn), lambda i, j, k: (k, j)),
                  pl.BlockSpec((tm, tn), lambda i, j, k: (i, j)),
                  pl.BlockSpec((1, 1, tn), lambda i, j, k: ((i * tm) // rows_per_batch, 0, j))],
        out_specs=pl.BlockSpec((tm, tn), lambda i, j, k: (i, j)),
        out_shape=jax.ShapeDtypeStruct((m, n), F32),
        scratch_shapes=[pltpu.VMEM((tm, tn), F32)],
        compiler_params=_cparams("parallel", "parallel", "arbitrary"), name="down_residual",
    )(a, w, x, gate.reshape(bsz, 1, n))


def _mix_out_kernel(a_ref, b_ref, c_ref, w_ref, x_ref, gate_ref, o_ref):
    ka, kb = a_ref.shape[1], b_ref.shape[1]
    acc = _dot(a_ref[...], w_ref[0:ka, :])
    acc += _dot(b_ref[...], w_ref[ka:ka + kb, :])
    acc += _dot(c_ref[...], w_ref[ka + kb:, :])
    o_ref[...] = x_ref[...] + gate_ref[0] * acc


def mix_out_residual(oa, ob, oc, w, x, gate, rows_per_batch, tm, tn):
    m = oa.shape[0]
    kdim, n = w.shape
    tm, tn = _tile(rows_per_batch, tm), _tile(n, tn)
    bsz = gate.shape[0]
    return pl.pallas_call(
        _mix_out_kernel, grid=(m // tm, n // tn),
        in_specs=[pl.BlockSpec((tm, oa.shape[1]), lambda i, j: (i, 0)),
                  pl.BlockSpec((tm, ob.shape[1]), lambda i, j: (i, 0)),
                  pl.BlockSpec((tm, oc.shape[1]), lambda i, j: (i, 0)),
                  pl.BlockSpec((kdim, tn), lambda i, j: (0, j)),
                  pl.BlockSpec((tm, tn), lambda i, j: (i, j)),
                  pl.BlockSpec((1, 1, tn), lambda i, j: ((i * tm) // rows_per_batch, 0, j))],
        out_specs=pl.BlockSpec((tm, tn), lambda i, j: (i, j)),
        out_shape=jax.ShapeDtypeStruct((m, n), F32),
        compiler_params=_cparams("parallel", "parallel"), name="mix_out_residual",
    )(oa, ob, oc, w, x, gate.reshape(bsz, 1, n))


def _gelu_tanh(x):
    return 0.5 * x * (1.0 + jnp.tanh(math.sqrt(2.0 / math.pi) * (x + 0.044715 * (x * x * x))))


def _sgu_kernel(z_ref, g_ref, w_ref, b_ref, o_ref):
    tl = z_ref.shape[1]
    for c in range(tl // A_CHUNK):
        rows = slice(c * A_CHUNK, (c + 1) * A_CHUNK)
        for g in range(A_GROUPS):
            cols = slice(g * A_GROUP_DIM, (g + 1) * A_GROUP_DIM)
            u = _gelu_tanh(z_ref[0, rows, cols].astype(F32))
            v = _gelu_tanh(z_ref[0, rows, A_WIDTH + g * A_GROUP_DIM:A_WIDTH + (g + 1) * A_GROUP_DIM].astype(F32))
            vc = v - jnp.mean(v, axis=-1, keepdims=True)
            vn = vc * lax.rsqrt(jnp.mean(vc * vc, axis=-1, keepdims=True) + NORM_EPS) * g_ref[:, cols]
            mixed = _dot(w_ref[g], vn.astype(BF16)) + b_ref[g]
            o_ref[0, rows, cols] = (u * mixed).astype(o_ref.dtype)


def chunk_sgu(z, norm_g, w_s, b_s):
    b, l, _ = z.shape
    tl = min(512, l)
    blk = ZC_A // (2 * A_WIDTH)
    return pl.pallas_call(
        _sgu_kernel, grid=(b, l // tl),
        in_specs=[pl.BlockSpec((1, tl, 2 * A_WIDTH), lambda i, j: (i, j, blk)),
                  pl.BlockSpec((1, A_WIDTH), lambda i, j: (0, 0)),
                  pl.BlockSpec((A_GROUPS, A_CHUNK, A_CHUNK), lambda i, j: (0, 0, 0)),
                  pl.BlockSpec((A_GROUPS, A_CHUNK, 1), lambda i, j: (0, 0, 0))],
        out_specs=pl.BlockSpec((1, tl, A_WIDTH), lambda i, j: (i, j, 0)),
        out_shape=jax.ShapeDtypeStruct((b, l, A_WIDTH), BF16),
        compiler_params=_cparams("parallel", "parallel"), name="chunk_sgu",
    )(z, norm_g.reshape(1, A_WIDTH), w_s.astype(BF16), b_s.reshape(A_GROUPS, A_CHUNK, 1))


def _mla_prep_kernel(z_ref, qg_ref, kvg_ref, wq_ref, wkv_ref, cos_ref, sin_ref, q_ref, k_ref, v_ref, *, q_scale):
    lat = z_ref[0, :, 0:Q_LORA].astype(F32)
    cq = (lat * lax.rsqrt(jnp.mean(lat * lat, axis=-1, keepdims=True) + NORM_EPS) * qg_ref[...]).astype(BF16)
    kvl = z_ref[0, :, Q_LORA:Q_LORA + KV_LORA].astype(F32)
    ckv = (kvl * lax.rsqrt(jnp.mean(kvl * kvl, axis=-1, keepdims=True) + NORM_EPS) * kvg_ref[...]).astype(BF16)
    cos = cos_ref[...]
    sin = sin_ref[...]
    base = Q_LORA + KV_LORA
    k_pe = z_ref[0, :, base:base + LANE].astype(F32) * cos + z_ref[0, :, base + LANE:base + 2 * LANE].astype(F32) * sin
    k_pe = k_pe.astype(k_ref.dtype)
    for h in range(C_HEADS):
        qh = _dot(cq, wq_ref[:, h * C_HEAD_COLS:(h + 1) * C_HEAD_COLS])
        q_pe = qh[:, LANE:2 * LANE] * cos + qh[:, 2 * LANE:3 * LANE] * sin
        q_ref[0, h, :, 0:LANE] = (qh[:, 0:LANE] * q_scale).astype(q_ref.dtype)
        q_ref[0, h, :, LANE:2 * LANE] = (q_pe * q_scale).astype(q_ref.dtype)
        kv = _dot(ckv, wkv_ref[:, h * 2 * LANE:(h + 1) * 2 * LANE])
        k_ref[0, h, :, 0:LANE] = kv[:, 0:LANE].astype(k_ref.dtype)
        k_ref[0, h, :, LANE:2 * LANE] = k_pe
        v_ref[0, h] = kv[:, LANE:2 * LANE].astype(v_ref.dtype)


def mla_prep(z, q_norm_g, kv_norm_g, wq_arr, wkv_arr, cos2, sin2):
    b, l, _ = z.shape
    tl = min(256, l)
    blk = ZC_C // 1024
    q_scale = (C_NOPE + C_ROPE) ** -0.5 * math.log2(math.e)
    return pl.pallas_call(
        functools.partial(_mla_prep_kernel, q_scale=q_scale), grid=(b, l // tl),
        in_specs=[pl.BlockSpec((1, tl, 1024), lambda i, j: (i, j, blk)),
                  pl.BlockSpec((1, Q_LORA), lambda i, j: (0, 0)),
                  pl.BlockSpec((1, KV_LORA), lambda i, j: (0, 0)),
                  pl.BlockSpec((Q_LORA, C_HEADS * C_HEAD_COLS), lambda i, j: (0, 0)),
                  pl.BlockSpec((KV_LORA, C_HEADS * 2 * LANE), lambda i, j: (0, 0)),
                  pl.BlockSpec((tl, LANE), lambda i, j: (j, 0)),
                  pl.BlockSpec((tl, LANE), lambda i, j: (j, 0))],
        out_specs=[pl.BlockSpec((1, C_HEADS, tl, 2 * LANE), lambda i, j: (i, 0, j, 0)),
                   pl.BlockSpec((1, C_HEADS, tl, 2 * LANE), lambda i, j: (i, 0, j, 0)),
                   pl.BlockSpec((1, C_HEADS, tl, LANE), lambda i, j: (i, 0, j, 0))],
        out_shape=[jax.ShapeDtypeStruct((b, C_HEADS, l, 2 * LANE), BF16),
                   jax.ShapeDtypeStruct((b, C_HEADS, l, 2 * LANE), BF16),
                   jax.ShapeDtypeStruct((b, C_HEADS, l, LANE), BF16)],
        compiler_params=_cparams("parallel", "parallel"), name="mla_prep",
    )(z, q_norm_g.reshape(1, Q_LORA), kv_norm_g.reshape(1, KV_LORA), wq_arr, wkv_arr, cos2, sin2)


def _flash_kernel(q_ref, k_ref, v_ref, o_ref, *, tk, nk):
    q = q_ref[0, 0]
    tq = q.shape[0]

    def body(j, carry):
        m, l, acc = carry
        start = pl.multiple_of(j * tk, tk)
        k = k_ref[0, 0, pl.ds(start, tk), :]
        v = v_ref[0, 0, pl.ds(start, tk), :]
        s = _dot_nt(q, k)
        m_new = jnp.maximum(m, jnp.max(s, axis=-1, keepdims=True))
        alpha = jnp.exp2(m - m_new)
        p = jnp.exp2(s - m_new)
        l = alpha * l + jnp.sum(p, axis=-1, keepdims=True)
        acc = alpha * acc + _dot(p.astype(BF16), v)
        return m_new, l, acc

    init = (jnp.full((tq, 1), -jnp.inf, F32), jnp.zeros((tq, 1), F32), jnp.zeros((tq, C_V), F32))
    m, l, acc = lax.fori_loop(0, nk, body, init)
    o_ref[0] = (acc / l).astype(o_ref.dtype)


def flash_attention(q, k, v):
    b, h, lq, dq = q.shape
    lk = k.shape[2]
    tq = min(256, lq)
    tk = _tile(lk, 768)
    return pl.pallas_call(
        functools.partial(_flash_kernel, tk=tk, nk=lk // tk), grid=(b, h, lq // tq),
        in_specs=[pl.BlockSpec((1, 1, tq, dq), lambda i, j, t: (i, j, t, 0)),
                  pl.BlockSpec((1, 1, lk, dq), lambda i, j, t: (i, j, 0, 0)),
                  pl.BlockSpec((1, 1, lk, C_V), lambda i, j, t: (i, j, 0, 0))],
        out_specs=pl.BlockSpec((1, tq, C_V), lambda i, j, t: (i, t, j)),
        out_shape=jax.ShapeDtypeStruct((b, lq, h * C_V), BF16),
        compiler_params=_cparams("parallel", "parallel", "parallel"), name="flash_attention",
    )(q, k, v)


def _softplus(x):
    return jnp.maximum(x, 0.0) + jnp.log1p(jnp.exp(-jnp.abs(x)))


def _dn_prep_kernel(zm_ref, zp_ref, zn_ref, zg_ref, w_ref, alog_ref, dtb_ref, qkv_ref, gb_ref):
    j = pl.program_id(1)
    tl = zm_ref.shape[1]
    half = DN_CONV // 2
    keep_prev = (j > 0).astype(F32)
    keep_next = (j < pl.num_programs(1) - 1).astype(F32)
    for c in range(3 * B_HEADS):
        cols = slice(c * LANE, (c + 1) * LANE)
        prev = zp_ref[0, :, cols].astype(F32)[8:16] * keep_prev
        nxt = zn_ref[0, :, cols].astype(F32)[0:8] * keep_next
        ext = jnp.concatenate([prev, zm_ref[0, :, cols].astype(F32), nxt], axis=0)
        y = ext[8 - half:8 - half + tl] * w_ref[0:1, cols]
        for i in range(1, DN_CONV):
            y = y + ext[8 - half + i:8 - half + i + tl] * w_ref[i:i + 1, cols]
        y = _silu(y)
        if c < 2 * B_HEADS:
            y = y * lax.rsqrt(jnp.sum(y * y, axis=-1, keepdims=True) + NORM_EPS)
        qkv_ref[0, :, cols] = y.astype(qkv_ref.dtype)
    zg = zg_ref[0]
    lane = lax.broadcasted_iota(jnp.int32, zg.shape, 1)
    g = -jnp.exp(alog_ref[...]) * _softplus(zg + dtb_ref[...])
    gb_ref[0] = jnp.where(lane < 2 * B_HEADS, g, _sigmoid(zg))


def deltanet_prep(z, zg, conv_w, a_log, dt_bias):
    b, l, _ = z.shape
    tl = min(256, l)
    wq = 3 * B_WIDTH
    nb16 = l // 16
    pad = LANE - 2 * B_HEADS
    alog = jnp.pad(a_log.reshape(1, -1), ((0, 0), (0, pad)))
    dtb = jnp.pad(dt_bias.reshape(1, -1), ((0, 0), (0, pad)))
    return pl.pallas_call(
        _dn_prep_kernel, grid=(b, l // tl),
        in_specs=[pl.BlockSpec((1, tl, wq), lambda i, j: (i, j, 0)),
                  pl.BlockSpec((1, 16, wq), lambda i, j: (i, jnp.maximum(j * (tl // 16) - 1, 0), 0)),
                  pl.BlockSpec((1, 16, wq), lambda i, j: (i, jnp.minimum((j + 1) * (tl // 16), nb16 - 1), 0)),
                  pl.BlockSpec((1, tl, LANE), lambda i, j: (i, j, 0)),
                  pl.BlockSpec((DN_CONV, wq), lambda i, j: (0, 0)),
                  pl.BlockSpec((1, LANE), lambda i, j: (0, 0)),
                  pl.BlockSpec((1, LANE), lambda i, j: (0, 0))],
        out_specs=[pl.BlockSpec((1, tl, wq), lambda i, j: (i, j, 0)),
                   pl.BlockSpec((1, tl, LANE), lambda i, j: (i, j, 0))],
        out_shape=[jax.ShapeDtypeStruct((b, l, wq), BF16), jax.ShapeDtypeStruct((b, l, LANE), F32)],
        compiler_params=_cparams("parallel", "parallel"), name="deltanet_prep",
    )(z, z, z, zg, conv_w, alog, dtb)


def _dn_intra_kernel(qkv_ref, gb_ref, u_ref, w_ref, qg_ref, kg_ref, qk_ref, gl_ref):
    c = DN_CHUNK
    gb = gb_ref[0]
    row = lax.broadcasted_iota(jnp.int32, (c, LANE), 0)
    lane = lax.broadcasted_iota(jnp.int32, (c, LANE), 1)
    col = jnp.where(lane < c, lane, lane - c)
    fwd = lane < c
    bwd = jnp.logical_not(fwd)
    incl = (fwd & (row >= col)) | (bwd & (row <= col))
    strict = (fwd & (row > col)) | (bwd & (row < col))
    same16 = lax.shift_right_logical(row, 4) == lax.shift_right_logical(col, 4)
    same32 = lax.shift_right_logical(row, 5) == lax.shift_right_logical(col, 5)
    eye2 = (row == col).astype(F32)
    r64 = lax.broadcasted_iota(jnp.int32, (c, c), 0)
    c64 = lax.broadcasted_iota(jnp.int32, (c, c), 1)
    tri_lo = (r64 >= c64).astype(F32)
    tri_up = (r64 <= c64).astype(F32)
    cum_f = jnp.dot(tri_lo, gb, preferred_element_type=F32, precision=HIGHEST)
    cum_b = jnp.dot(tri_up, gb, preferred_element_type=F32, precision=HIGHEST)
    scale = B_HEAD_DIM ** -0.5

    def pick(x, idx):
        return jnp.sum(jnp.where(lane == idx, x, 0.0), axis=-1, keepdims=True)

    def blockdiag(y2):
        return jnp.concatenate([jnp.where(fwd, y2, 0.0), jnp.where(fwd, 0.0, y2)], axis=0)

    def mm(x2, y2):
        return _dot_x3(x2, blockdiag(y2))

    gl_rows = []
    for d in range(2):
        cum = cum_f if d == 0 else cum_b
        last = cum[c - 1:c, :] if d == 0 else cum[0:1, :]
        gl_rows.append(last)
    for h in range(B_HEADS):
        hc = slice(h * LANE, (h + 1) * LANE)
        q = qkv_ref[0, :, h * LANE:(h + 1) * LANE]
        k = qkv_ref[0, :, B_WIDTH + h * LANE:B_WIDTH + (h + 1) * LANE]
        v = qkv_ref[0, :, 2 * B_WIDTH + h * LANE:2 * B_WIDTH + (h + 1) * LANE].astype(F32)
        k2 = jnp.concatenate([k, k], axis=0)
        kk2 = _dot_nt(k, k2)
        qk2 = _dot_nt(q, k2)
        cf = pick(cum_f, h)
        cb = pick(cum_b, B_HEADS + h)
        bf = pick(gb, 2 * B_HEADS + h)
        bb = pick(gb, 3 * B_HEADS + h)
        c2 = jnp.where(fwd, cf, cb)
        r2 = jnp.sum(jnp.where(row == col, c2, 0.0), axis=0, keepdims=True)
        beta2 = jnp.where(fwd, bf, bb)
        decay2 = jnp.where(incl, jnp.exp(jnp.where(incl, c2 - r2, 0.0)), 0.0)
        l2 = jnp.where(strict, beta2 * kk2 * decay2, 0.0)
        md = jnp.where(same16, -l2, 0.0)
        p = eye2 + md
        mp = md
        for _ in range(3):
            mp = mm(mp, mp)
            p = p + mm(p, mp)
        off32 = jnp.where(same32 & jnp.logical_not(same16), l2, 0.0)
        p = p - mm(mm(p, off32), p)
        off64 = jnp.where(same32, 0.0, l2)
        p = p - mm(mm(p, off64), p)
        kf = k.astype(F32)
        qf = q.astype(F32)
        ef, eb = jnp.exp(cf), jnp.exp(cb)
        zero = jnp.zeros((c, LANE), F32)
        rhs = jnp.concatenate([
            jnp.concatenate([v * bf, kf * (bf * ef), zero, zero], axis=1),
            jnp.concatenate([zero, zero, v * bb, kf * (bb * eb)], axis=1)], axis=0)
        sol = _dot_x3(p, rhs)
        lf = pick(jnp.broadcast_to(gl_rows[0], (c, LANE)), h)
        lb = pick(jnp.broadcast_to(gl_rows[1], (c, LANE)), B_HEADS + h)
        u_ref[0, 0, :, hc] = sol[:, 0:LANE]
        w_ref[0, 0, :, hc] = sol[:, LANE:2 * LANE].astype(w_ref.dtype)
        u_ref[1, 0, :, hc] = sol[:, 2 * LANE:3 * LANE]
        w_ref[1, 0, :, hc] = sol[:, 3 * LANE:4 * LANE].astype(w_ref.dtype)
        qg_ref[0, 0, :, hc] = (qf * (ef * scale)).astype(qg_ref.dtype)
        qg_ref[1, 0, :, hc] = (qf * (eb * scale)).astype(qg_ref.dtype)
        kg_ref[0, 0, :, hc] = (kf * jnp.exp(lf - cf)).astype(kg_ref.dtype)
        kg_ref[1, 0, :, hc] = (kf * jnp.exp(lb - cb)).astype(kg_ref.dtype)
        qk_ref[0, :, hc] = (qk2 * decay2 * scale).astype(qk_ref.dtype)
    r8 = lax.broadcasted_iota(jnp.int32, (2 * 8, LANE), 0)
    l8 = lax.broadcasted_iota(jnp.int32, (2 * 8, LANE), 1)
    src = jnp.concatenate([jnp.broadcast_to(gl_rows[0], (8, LANE)), jnp.broadcast_to(gl_rows[1], (8, LANE))], axis=0)
    want = jnp.where(r8 < 8, r8, r8 - 8 + B_HEADS)
    tot = jnp.sum(jnp.where(l8 == want, src, 0.0), axis=-1, keepdims=True)
    gl_ref[0, 0] = jnp.broadcast_to(jnp.exp(tot), (2 * 8, LANE))


def deltanet_intra(qkv, gb):
    b, l, _ = qkv.shape
    nc = l // DN_CHUNK
    dir_spec = pl.BlockSpec((2, 1, DN_CHUNK, B_WIDTH), lambda i, j: (0, i, j, 0))
    return pl.pallas_call(
        _dn_intra_kernel, grid=(b, nc),
        in_specs=[pl.BlockSpec((1, DN_CHUNK, 3 * B_WIDTH), lambda i, j: (i, j, 0)),
                  pl.BlockSpec((1, DN_CHUNK, LANE), lambda i, j: (i, j, 0))],
        out_specs=[dir_spec, dir_spec, dir_spec, dir_spec,
                   pl.BlockSpec((1, DN_CHUNK, B_WIDTH), lambda i, j: (i, j, 0)),
                   pl.BlockSpec((1, 1, 16, LANE), lambda i, j: (i, j, 0, 0))],
        out_shape=[jax.ShapeDtypeStruct((2, b, l, B_WIDTH), F32),
                   jax.ShapeDtypeStruct((2, b, l, B_WIDTH), BF16),
                   jax.ShapeDtypeStruct((2, b, l, B_WIDTH), BF16),
                   jax.ShapeDtypeStruct((2, b, l, B_WIDTH), BF16),
                   jax.ShapeDtypeStruct((b, l, B_WIDTH), BF16),
                   jax.ShapeDtypeStruct((b, nc, 16, LANE), F32)],
        compiler_params=_cparams("parallel", "parallel"), name="deltanet_intra",
    )(qkv, gb)


def _dn_scan_kernel(uf_ref, wf_ref, qgf_ref, kgf_ref, qkf_ref, glf_ref,
                    ub_ref, wb_ref, qgb_ref, kgb_ref, qkb_ref, glb_ref, s0_ref,
                    of_ref, ob_ref, sfin_ref, s_ref):
    j = pl.program_id(1)
    c = DN_CHUNK

    @pl.when(j == 0)
    def _():
        s_ref[...] = s0_ref[0]

    lane = lax.broadcasted_iota(jnp.int32, (c, LANE), 1)
    zeros_b = jnp.zeros((c, LANE), BF16)
    for d, (u_ref, w_ref, qg_ref, kg_ref, qk_ref, gl_ref, o_ref) in enumerate((
            (uf_ref, wf_ref, qgf_ref, kgf_ref, qkf_ref, glf_ref, of_ref),
            (ub_ref, wb_ref, qgb_ref, kgb_ref, qkb_ref, glb_ref, ob_ref))):
        for h in range(B_HEADS):
            hc = slice(h * LANE, (h + 1) * LANE)
            s = s_ref[d * B_HEADS + h]
            wq = jnp.concatenate([w_ref[0, 0, :, hc], qg_ref[0, 0, :, hc]], axis=0)
            r = _dot(wq, s.astype(BF16))
            v_new = u_ref[0, 0, :, hc] - r[0:c]
            vb = v_new.astype(BF16)
            qk2 = qk_ref[0, :, hc]
            if d == 0:
                intra = _dot(jnp.where(lane < c, qk2, jnp.zeros_like(qk2)), jnp.concatenate([vb, zeros_b], axis=0))
            else:
                intra = _dot(jnp.where(lane < c, jnp.zeros_like(qk2), qk2), jnp.concatenate([zeros_b, vb], axis=0))
            o_ref[0, :, hc] = r[c:2 * c] + intra
            gl = gl_ref[0, 0, d * 8 + h:d * 8 + h + 1, :]
            s_ref[d * B_HEADS + h] = s * gl + _dot_tn(kg_ref[0, 0, :, hc], vb)

    @pl.when(j == pl.num_programs(1) - 1)
    def _():
        sfin_ref[0] = s_ref[...]


def deltanet_scan(u, w, qg, kg, qk, gl, s0):
    _, b, l, _ = u.shape
    nc = l // DN_CHUNK
    fdir = pl.BlockSpec((1, 1, DN_CHUNK, B_WIDTH), lambda i, j: (0, i, j, 0))
    bdir = pl.BlockSpec((1, 1, DN_CHUNK, B_WIDTH), lambda i, j: (1, i, nc - 1 - j, 0))
    fqk = pl.BlockSpec((1, DN_CHUNK, B_WIDTH), lambda i, j: (i, j, 0))
    bqk = pl.BlockSpec((1, DN_CHUNK, B_WIDTH), lambda i, j: (i, nc - 1 - j, 0))
    fgl = pl.BlockSpec((1, 1, 16, LANE), lambda i, j: (i, j, 0, 0))
    bgl = pl.BlockSpec((1, 1, 16, LANE), lambda i, j: (i, nc - 1 - j, 0, 0))
    st = pl.BlockSpec((1, 2 * B_HEADS, B_HEAD_DIM, B_HEAD_DIM), lambda i, j: (i, 0, 0, 0))
    return pl.pallas_call(
        _dn_scan_kernel, grid=(b, nc),
        in_specs=[fdir, fdir, fdir, fdir, fqk, fgl, bdir, bdir, bdir, bdir, bqk, bgl, st],
        out_specs=[fqk, bqk, st],
        out_shape=[jax.ShapeDtypeStruct((b, l, B_WIDTH), F32),
                   jax.ShapeDtypeStruct((b, l, B_WIDTH), F32),
                   jax.ShapeDtypeStruct((b, 2 * B_HEADS, B_HEAD_DIM, B_HEAD_DIM), F32)],
        scratch_shapes=[pltpu.VMEM((2 * B_HEADS, B_HEAD_DIM, B_HEAD_DIM), F32)],
        compiler_params=_cparams("parallel", "arbitrary"), name="deltanet_scan",
    )(u, w, qg, kg, qk, gl, u, w, qg, kg, qk, gl, s0)


def _dn_out_kernel(of_ref, ob_ref, gate_ref, g_ref, o_ref):
    for h in range(B_HEADS):
        hc = slice(h * LANE, (h + 1) * LANE)
        o = of_ref[0, :, hc] + ob_ref[0, :, hc]
        y = o * lax.rsqrt(jnp.mean(o * o, axis=-1, keepdims=True) + NORM_EPS) * g_ref[...]
        o_ref[0, :, hc] = (y * _silu(gate_ref[0, :, hc].astype(F32))).astype(o_ref.dtype)


def deltanet_out(o_f, o_b, z, norm_g):
    b, l, _ = o_f.shape
    tl = min(512, l)
    blk = ZC_GATE // B_WIDTH
    return pl.pallas_call(
        _dn_out_kernel, grid=(b, l // tl),
        in_specs=[pl.BlockSpec((1, tl, B_WIDTH), lambda i, j: (i, j, 0)),
                  pl.BlockSpec((1, tl, B_WIDTH), lambda i, j: (i, j, 0)),
                  pl.BlockSpec((1, tl, B_WIDTH), lambda i, j: (i, j, blk)),
                  pl.BlockSpec((1, B_HEAD_DIM), lambda i, j: (0, 0))],
        out_specs=pl.BlockSpec((1, tl, B_WIDTH), lambda i, j: (i, j, 0)),
        out_shape=jax.ShapeDtypeStruct((b, l, B_WIDTH), BF16),
        compiler_params=_cparams("parallel", "parallel"), name="deltanet_out",
    )(o_f, o_b, z, norm_g.reshape(1, B_HEAD_DIM))


def _route_kernel(lg_ref, info_ref, cnt_ref, carry_ref):
    i = pl.program_id(0)

    @pl.when(i == 0)
    def _():
        carry_ref[...] = jnp.zeros_like(carry_ref)

    lg = lg_ref[...]
    tl = lg.shape[0]
    lane = lax.broadcasted_iota(jnp.int32, lg.shape, 1)
    valid = lane < N_EXPERTS
    lg = jnp.where(valid, lg, -jnp.inf)
    e = jnp.exp(lg - jnp.max(lg, axis=-1, keepdims=True))
    p = e / jnp.sum(e, axis=-1, keepdims=True)
    p = jnp.where(valid, p, -1.0)
    p1 = jnp.max(p, axis=-1, keepdims=True)
    i1 = jnp.min(jnp.where(p == p1, lane, LANE), axis=-1, keepdims=True)
    pm = jnp.where(lane == i1, -1.0, p)
    p2 = jnp.max(pm, axis=-1, keepdims=True)
    i2 = jnp.min(jnp.where(pm == p2, lane, LANE), axis=-1, keepdims=True)
    tot = p1 + p2
    w1, w2 = p1 / tot, p2 / tot
    hit1, hit2 = lane == i1, lane == i2
    onehot = (hit1 | hit2).astype(F32)
    r = lax.broadcasted_iota(jnp.int32, (tl, tl), 0)
    c = lax.broadcasted_iota(jnp.int32, (tl, tl), 1)
    before = _dot((r > c).astype(BF16), onehot.astype(BF16)) + carry_ref[...]
    r1 = jnp.sum(jnp.where(hit1, before, 0.0), axis=-1, keepdims=True)
    r2 = jnp.sum(jnp.where(hit2, before, 0.0), axis=-1, keepdims=True)
    carry_ref[...] += jnp.sum(onehot, axis=0, keepdims=True)
    cnt_ref[...] = jnp.broadcast_to(carry_ref[...], cnt_ref.shape)
    info = jnp.where(lane == 0, i1.astype(F32), 0.0)
    info = jnp.where(lane == 1, i2.astype(F32), info)
    info = jnp.where(lane == 2, r1, info)
    info = jnp.where(lane == 3, r2, info)
    info = jnp.where(lane == 4, w1, info)
    info = jnp.where(lane == 5, w2, info)
    info_ref[...] = info


def moe_route(logits):
    t = logits.shape[0]
    tl = min(512, t)
    return pl.pallas_call(
        _route_kernel, grid=(t // tl,),
        in_specs=[pl.BlockSpec((tl, LANE), lambda i: (i, 0))],
        out_specs=[pl.BlockSpec((tl, LANE), lambda i: (i, 0)), pl.BlockSpec((8, LANE), lambda i: (0, 0))],
        out_shape=[jax.ShapeDtypeStruct((t, LANE), F32), jax.ShapeDtypeStruct((8, LANE), F32)],
        scratch_shapes=[pltpu.VMEM((1, LANE), F32)],
        compiler_params=_cparams("arbitrary"), name="moe_route",
    )(logits)


def _dispatch_kernel(pos_ref, h_ref, xs_in_ref, xs_ref, sem):
    del xs_in_ref
    i = pl.program_id(0)
    tb = pos_ref.shape[2] // TOP_K

    def copy(src_row, dst_row):
        return pltpu.make_async_copy(h_ref.at[pl.ds(pl.multiple_of(src_row * ROW_TILE, ROW_TILE), ROW_TILE)],
                                     xs_ref.at[pl.ds(pl.multiple_of(dst_row * ROW_TILE, ROW_TILE), ROW_TILE)], sem)

    def issue(j, carry):
        for k in range(TOP_K):
            copy(i * tb + j, pos_ref[0, 0, TOP_K * j + k]).start()
        return carry

    lax.fori_loop(0, tb, issue, 0)

    def drain(j, carry):
        copy(0, 0).wait()
        return carry

    lax.fori_loop(0, TOP_K * tb, drain, 0)


def moe_dispatch(h_rows, pos, n_rows):
    t = pos.shape[0]
    tb = min(256, t)
    d_rows = h_rows.shape[0] // t
    assert d_rows == ROW_TILE
    zeros = jnp.zeros((n_rows * ROW_TILE, LANE), h_rows.dtype)
    return pl.pallas_call(
        _dispatch_kernel, grid=(t // tb,),
        in_specs=[pl.BlockSpec((1, 1, TOP_K * tb), lambda i: (i, 0, 0), memory_space=pltpu.SMEM),
                  pl.BlockSpec(memory_space=pl.ANY),
                  pl.BlockSpec(memory_space=pl.ANY)],
        out_specs=pl.BlockSpec(memory_space=pl.ANY),
        out_shape=jax.ShapeDtypeStruct(zeros.shape, zeros.dtype),
        scratch_shapes=[pltpu.SemaphoreType.DMA],
        input_output_aliases={2: 0},
        compiler_params=_cparams("arbitrary"), name="moe_dispatch",
    )(pos.reshape(t // tb, 1, TOP_K * tb), h_rows, zeros)


def _moe_up_kernel(te_ref, tv_ref, xs_ref, wg_ref, wu_ref, o_ref):
    del te_ref
    i = pl.program_id(0)

    @pl.when(tv_ref[i] != 0)
    def _():
        a = xs_ref[...]
        g = _dot(a, wg_ref[0])
        u = _dot(a, wu_ref[0])
        o_ref[...] = (_silu(g) * u).astype(o_ref.dtype)

    @pl.when(tv_ref[i] == 0)
    def _():
        o_ref[...] = jnp.zeros_like(o_ref)


def moe_up(xs, wg, wu, tile_expert, tile_valid, tm, tn):
    r, d = xs.shape
    n = wg.shape[2]
    tn = _tile(n, tn)
    grid_spec = pltpu.PrefetchScalarGridSpec(
        num_scalar_prefetch=2, grid=(r // tm, n // tn),
        in_specs=[pl.BlockSpec((tm, d), lambda i, j, te, tv: (i, 0)),
                  pl.BlockSpec((1, d, tn), lambda i, j, te, tv: (te[i], 0, j)),
                  pl.BlockSpec((1, d, tn), lambda i, j, te, tv: (te[i], 0, j))],
        out_specs=pl.BlockSpec((tm, tn), lambda i, j, te, tv: (i, j)))
    return pl.pallas_call(
        _moe_up_kernel, grid_spec=grid_spec,
        out_shape=jax.ShapeDtypeStruct((r, n), BF16),
        compiler_params=_cparams("parallel", "parallel"), name="moe_up",
    )(tile_expert, tile_valid, xs, wg, wu)


def _moe_down_kernel(te_ref, tv_ref, a_ref, w_ref, o_ref, acc_ref):
    del te_ref
    i = pl.program_id(0)
    k = pl.program_id(2)

    @pl.when(k == 0)
    def _():
        acc_ref[...] = jnp.zeros_like(acc_ref)

    @pl.when(tv_ref[i] != 0)
    def _():
        acc_ref[...] += _dot(a_ref[...], w_ref[0])

    @pl.when(k == pl.num_programs(2) - 1)
    def _():
        o_ref[...] = acc_ref[...]


def moe_down(hid, wd, tile_expert, tile_valid, tm, tn, tk):
    r, kdim = hid.shape
    n = wd.shape[2]
    tn, tk = _tile(n, tn), _tile(kdim, tk)
    grid_spec = pltpu.PrefetchScalarGridSpec(
        num_scalar_prefetch=2, grid=(r // tm, n // tn, kdim // tk),
        in_specs=[pl.BlockSpec((tm, tk), lambda i, j, k, te, tv: (i, k)),
                  pl.BlockSpec((1, tk, tn), lambda i, j, k, te, tv: (te[i], k, j))],
        out_specs=pl.BlockSpec((tm, tn), lambda i, j, k, te, tv: (i, j)),
        scratch_shapes=[pltpu.VMEM((tm, tn), F32)])
    return pl.pallas_call(
        _moe_down_kernel, grid_spec=grid_spec,
        out_shape=jax.ShapeDtypeStruct((r, n), F32),
        compiler_params=_cparams("parallel", "parallel", "arbitrary"), name="moe_down",
    )(tile_expert, tile_valid, hid, wd)


def _combine_kernel(pos_ref, ys_ref, x_ref, gate_ref, wt_ref, o_ref, buf_ref, sem):
    tb = x_ref.shape[0]

    def copy(k, j, src_row):
        return pltpu.make_async_copy(ys_ref.at[pl.ds(pl.multiple_of(src_row * ROW_TILE, ROW_TILE), ROW_TILE)],
                                     buf_ref.at[k, j], sem)

    def issue(j, carry):
        for k in range(TOP_K):
            copy(k, j, pos_ref[0, 0, TOP_K * j + k]).start()
        return carry

    lax.fori_loop(0, tb, issue, 0)

    def drain(j, carry):
        copy(0, 0, 0).wait()
        return carry

    lax.fori_loop(0, TOP_K * tb, drain, 0)
    w = wt_ref[...]
    f = buf_ref[0] * w[:, 0] + buf_ref[1] * w[:, 1]
    o_ref[...] = x_ref[...] + gate_ref[0] * f


def moe_combine(ys_rows, pos, wts, x_rows, gate_rows, rows_per_batch):
    t = pos.shape[0]
    tb = min(128, t)
    return pl.pallas_call(
        _combine_kernel, grid=(t // tb,),
        in_specs=[pl.BlockSpec((1, 1, TOP_K * tb), lambda i: (i, 0, 0), memory_space=pltpu.SMEM),
                  pl.BlockSpec(memory_space=pl.ANY),
                  pl.BlockSpec((tb, ROW_TILE, LANE), lambda i: (i, 0, 0)),
                  pl.BlockSpec((1, ROW_TILE, LANE), lambda i: ((i * tb) // rows_per_batch, 0, 0)),
                  pl.BlockSpec((tb, TOP_K, 1, 1), lambda i: (i, 0, 0, 0))],
        out_specs=pl.BlockSpec((tb, ROW_TILE, LANE), lambda i: (i, 0, 0)),
        out_shape=jax.ShapeDtypeStruct(x_rows.shape, F32),
        scratch_shapes=[pltpu.VMEM((TOP_K, tb, ROW_TILE, LANE), F32), pltpu.SemaphoreType.DMA],
        compiler_params=_cparams("arbitrary"), name="moe_combine",
    )(pos.reshape(t // tb, 1, TOP_K * tb), ys_rows, x_rows, gate_rows, wts.reshape(t, TOP_K, 1, 1))


MOE_TM = 1024


def moe_ffn(h_f32, logits, wg, wu, wd, x_rows, gate, rows_per_batch):
    t, d = h_f32.shape
    tm = min(MOE_TM, t)
    info, counts = moe_route(logits)
    sizes = counts[0, :N_EXPERTS].astype(jnp.int32)
    padded = ((sizes + tm - 1) // tm) * tm
    ends = jnp.cumsum(padded)
    starts = ends - padded
    n_tiles = (t * TOP_K) // tm + N_EXPERTS
    n_rows = n_tiles * tm
    experts = info[:, 0:TOP_K].astype(jnp.int32)
    pos = starts[experts] + info[:, 2:2 + TOP_K].astype(jnp.int32)
    wts = info[:, 4:4 + TOP_K]
    tile_start = jnp.arange(n_tiles, dtype=jnp.int32) * tm
    tile_expert = jnp.minimum(jnp.sum((tile_start[:, None] >= ends[None, :]).astype(jnp.int32), axis=1), N_EXPERTS - 1)
    tile_valid = (tile_start < ends[-1]).astype(jnp.int32)
    xs_rows = moe_dispatch(h_f32.reshape(t * ROW_TILE, LANE), pos, n_rows)
    xs = xs_rows.reshape(n_rows, d).astype(BF16)
    hid = moe_up(xs, wg, wu, tile_expert, tile_valid, tm, 512)
    ys = moe_down(hid, wd, tile_expert, tile_valid, tm, 1024, 1792)
    gate_rows = gate.reshape(gate.shape[0], ROW_TILE, LANE)
    return moe_combine(ys.reshape(n_rows * ROW_TILE, LANE), pos, wts, x_rows, gate_rows, rows_per_batch)


def _rot_cols(w):
    f = ROPE_AXIS_FREQS
    return jnp.concatenate([-w[:, f:2 * f], w[:, 0:f], -w[:, 3 * f:4 * f], w[:, 2 * f:3 * f]], axis=1)


def _arrange_w_in(w):
    d = w.shape[0]
    a = w[:, 0:2 * A_WIDTH]
    off = 2 * A_WIDTH
    qkv_gate = w[:, off:off + 4 * B_WIDTH]
    logit = w[:, off + 4 * B_WIDTH:off + 4 * B_WIDTH + 4 * B_HEADS]
    off = off + 4 * B_WIDTH + 4 * B_HEADS
    lat = w[:, off:off + Q_LORA + KV_LORA]
    k_pe = w[:, off + Q_LORA + KV_LORA:off + Q_LORA + KV_LORA + C_ROPE]
    z64 = jnp.zeros((d, LANE - C_ROPE), w.dtype)
    main = jnp.concatenate([qkv_gate, a, lat, k_pe, z64, _rot_cols(k_pe), z64], axis=1).astype(BF16)
    logit = jnp.pad(logit, ((0, 0), (0, LANE - 4 * B_HEADS))).astype(BF16)
    return main, logit


def _arrange_w_uq(w):
    k = w.shape[0]
    w = w.reshape(k, C_HEADS, C_NOPE + C_ROPE)
    z64 = jnp.zeros((k, C_HEADS, LANE - C_ROPE), w.dtype)
    pe = w[:, :, C_NOPE:]
    pe_rot = jnp.stack([_rot_cols(pe[:, h]) for h in range(C_HEADS)], axis=1)
    return jnp.concatenate([w[:, :, :C_NOPE], pe, z64, pe_rot, z64], axis=2).reshape(k, C_HEADS * C_HEAD_COLS).astype(BF16)


def _rope_tables(n):
    rows = n // GRID_W
    row = jnp.repeat(jnp.arange(rows, dtype=F32), GRID_W)
    col = jnp.tile(jnp.arange(GRID_W, dtype=F32), rows)
    inv = jnp.power(ROPE_BASE, -jnp.arange(ROPE_AXIS_FREQS, dtype=F32) / ROPE_AXIS_FREQS)
    ar = row[:, None] * inv
    ac = col[:, None] * inv
    ang = jnp.concatenate([ar, ar, ac, ac], axis=-1)
    pad = ((0, 0), (0, LANE - C_ROPE))
    return jnp.pad(jnp.cos(ang), pad), jnp.pad(jnp.sin(ang), pad)


def _mixer_branches(z, zg, p, cos2, sin2):
    out_a = chunk_sgu(z, p["sgu_norm_g"], p["sgu_w"], p["sgu_b"])
    qkv, gb = deltanet_prep(z, zg, p["dn_conv_w"], p["dn_a_log"], p["dn_dt_bias"])
    intra = deltanet_intra(qkv, gb)
    q, k, v = mla_prep(z, p["mla_q_norm_g"], p["mla_kv_norm_g"], p["wq_arr"], p["wkv_arr"], cos2, sin2)
    return out_a, intra, (q, k, v)


def kernel(x, c, ctx, c_ctx, ada_w, ada_b, norm1_g, norm2_g, w_in, sgu_norm_g, sgu_w, sgu_b, dn_conv_w, dn_a_log, dn_dt_bias, dn_norm_g, mla_q_norm_g, mla_w_uq, mla_kv_norm_g, mla_w_ukv, w_out, ffn_w_gate, ffn_w_up, ffn_w_down, moe_router, moe_w_gate, moe_w_up, moe_w_down, final_norm_g):
    b, n, d = x.shape
    lc = ctx.shape[1]
    depth = ada_w.shape[0]
    assert d == ROW_TILE * LANE
    cos_lat, sin_lat = _rope_tables(n)
    cos_ctx = jnp.pad(jnp.ones((lc, C_ROPE), F32), ((0, 0), (0, LANE - C_ROPE)))
    sin_ctx = jnp.zeros((lc, LANE), F32)

    cc = jnp.zeros((8, d), F32).at[0:b].set(c).at[b].set(c_ctx)
    mod_all = ada_modulation(cc, ada_w, ada_b)

    xc = ctx
    x_rows = None
    for i in range(depth):
        last = i == depth - 1
        mod = mod_all[i, 0:b].reshape(b, 6, d)
        mod_c = jnp.broadcast_to(mod_all[i, b].reshape(1, 6, d), (b, 6, d))
        w_main, w_logit = _arrange_w_in(w_in[i])
        p = dict(sgu_norm_g=sgu_norm_g[i], sgu_w=sgu_w[i], sgu_b=sgu_b[i], dn_conv_w=dn_conv_w[i],
                 dn_a_log=dn_a_log[i], dn_dt_bias=dn_dt_bias[i], mla_q_norm_g=mla_q_norm_g[i],
                 mla_kv_norm_g=mla_kv_norm_g[i], wq_arr=_arrange_w_uq(mla_w_uq[i]),
                 wkv_arr=mla_w_ukv[i].astype(BF16))
        w_out_b = w_out[i].astype(BF16)

        h = norm_mod(x, norm1_g[i], mod[:, 0], mod[:, 1]).reshape(b * n, d)
        hc = norm_mod(xc, norm1_g[i], mod_c[:, 0], mod_c[:, 1]).reshape(b * lc, d)
        z = matmul(h, w_main, BF16, 1024, 1024).reshape(b, n, Z_COLS)
        zg = matmul(h, w_logit, F32, 1024, LANE).reshape(b, n, LANE)
        zc = matmul(hc, w_main, BF16, 1024, 1024).reshape(b, lc, Z_COLS)
        zgc = matmul(hc, w_logit, F32, 1024, LANE).reshape(b, lc, LANE)

        oa_c, intra_c, (q_c, k_c, v_c) = _mixer_branches(zc, zgc, p, cos_ctx, sin_ctx)
        out_a, intra, (q_l, k_l, v_l) = _mixer_branches(z, zg, p, cos_lat, sin_lat)
        s_zero = jnp.zeros((b, 2 * B_HEADS, B_HEAD_DIM, B_HEAD_DIM), F32)
        ocf, ocb, s_ctx = deltanet_scan(*intra_c, s_zero)
        o_f, o_b, _ = deltanet_scan(*intra, s_ctx)
        out_b = deltanet_out(o_f, o_b, z, dn_norm_g[i])
        out_c = flash_attention(q_l, jnp.concatenate([k_l, k_c], axis=2), jnp.concatenate([v_l, v_c], axis=2))
        x2 = mix_out_residual(out_a.reshape(b * n, -1), out_b.reshape(b * n, -1), out_c.reshape(b * n, -1),
                              w_out_b, x.reshape(b * n, d), mod[:, 2], n, 1024, 1024)
        if not last:
            ob_c = deltanet_out(ocf, ocb, zc, dn_norm_g[i])
            oc_c = flash_attention(q_c, k_c, v_c)
            xc2 = mix_out_residual(oa_c.reshape(b * lc, -1), ob_c.reshape(b * lc, -1), oc_c.reshape(b * lc, -1),
                                   w_out_b, xc.reshape(b * lc, d), mod_c[:, 2], lc, 1024, 1024)

        if i % 2 == 0:
            wg, wu, wd = (ffn_w_gate[i // 2].astype(BF16), ffn_w_up[i // 2].astype(BF16), ffn_w_down[i // 2].astype(BF16))
            h2 = norm_mod(x2.reshape(b, n, d), norm2_g[i], mod[:, 3], mod[:, 4]).reshape(b * n, d)
            hid = swiglu_up(h2, wg, wu, 1024, 512)
            x = down_residual(hid, wd, x2, mod[:, 5], n, 1024, 1024, 1408).reshape(b, n, d)
            if not last:
                hc2 = norm_mod(xc2.reshape(b, lc, d), norm2_g[i], mod_c[:, 3], mod_c[:, 4]).reshape(b * lc, d)
                hid_c = swiglu_up(hc2, wg, wu, 1024, 512)
                xc = down_residual(hid_c, wd, xc2, mod_c[:, 5], lc, 1024, 1024, 1408).reshape(b, lc, d)
        else:
            e = i // 2
            router = jnp.pad(moe_router[e], ((0, 0), (0, LANE - N_EXPERTS)))
            wg, wu, wd = moe_w_gate[e].astype(BF16), moe_w_up[e].astype(BF16), moe_w_down[e].astype(BF16)
            if last:
                h2, logits = norm_mod(x2.reshape(b, n, d), norm2_g[i], mod[:, 3], mod[:, 4], out_dtype=F32, router=router)
                xr = moe_ffn(h2.reshape(b * n, d), logits.reshape(b * n, LANE), wg, wu, wd,
                             x2.reshape(b * n, ROW_TILE, LANE), mod[:, 5], n)
                x = xr.reshape(b, n, d)
            else:
                raise NotImplementedError("an expert layer followed by another layer is not part of this model")
    return rmsnorm_rows(x.reshape(b * n, d), final_norm_g).reshape(b, n, d)
```

```python
import functools
import math

import jax
import jax.numpy as jnp
from jax import lax
from jax.experimental import pallas as pl
from jax.experimental.pallas import tpu as pltpu

F32 = jnp.float32
BF16 = jnp.bfloat16
HIGHEST = lax.Precision.HIGHEST

NORM_EPS = 1e-6
GRID_W = 64
A_GROUPS = 4
A_GROUP_DIM = 128
A_WIDTH = A_GROUPS * A_GROUP_DIM
A_CHUNK = 128
B_HEADS = 6
B_HEAD_DIM = 128
B_WIDTH = B_HEADS * B_HEAD_DIM
DN_CHUNK = 64
DN_CONV = 5
C_HEADS = 6
C_NOPE = 128
C_ROPE = 64
C_V = 128
C_WIDTH = C_HEADS * C_V
Q_LORA = 512
KV_LORA = 256
ROPE_BASE = 10000.0
ROPE_AXIS_FREQS = C_ROPE // 4
N_EXPERTS = 8
TOP_K = 2

LANE = 128
ROW_TILE = 16
VMEM_LIMIT = 56 * 1024 * 1024

ZC_QKV = 0
ZC_GATE = 3 * B_WIDTH
ZC_A = 4 * B_WIDTH
ZC_C = 4 * B_WIDTH + 2 * A_WIDTH
Z_COLS = ZC_C + 1024
C_HEAD_COLS = 3 * LANE


def _cparams(*sem):
    return pltpu.CompilerParams(dimension_semantics=sem, vmem_limit_bytes=VMEM_LIMIT)


def _tile(n, pref):
    if n <= pref:
        return n
    t = (pref // LANE) * LANE
    while n % t:
        t -= LANE
    return t


def _sigmoid(x):
    return 1.0 / (1.0 + jnp.exp(-x))


def _silu(x):
    return x * _sigmoid(x)


def _dot(a, b):
    return jnp.dot(a, b, preferred_element_type=F32)


def _dot_nt(a, b):
    return lax.dot_general(a, b, (((1,), (1,)), ((), ())), preferred_element_type=F32)


def _dot_tn(a, b):
    return lax.dot_general(a, b, (((0,), (0,)), ((), ())), preferred_element_type=F32)


def _split_bf16(a):
    hi = a.astype(BF16)
    lo = (a - hi.astype(F32)).astype(BF16)
    return hi, lo


def _dot_x3(a, b):
    ah, al = _split_bf16(a)
    bh, bl = _split_bf16(b)
    return _dot(ah, bh) + (_dot(ah, bl) + _dot(al, bh))


def _ada_kernel(c_ref, w_ref, b_ref, o_ref):
    c = c_ref[...]
    o_ref[0] = jnp.dot(_silu(c), w_ref[0], preferred_element_type=F32, precision=HIGHEST) + b_ref[0]


def ada_modulation(cc, ada_w, ada_b):
    depth, d, n = ada_w.shape
    tn = 1024
    return pl.pallas_call(
        _ada_kernel,
        grid=(depth, n // tn),
        in_specs=[
            pl.BlockSpec((8, d), lambda i, j: (0, 0)),
            pl.BlockSpec((1, d, tn), lambda i, j: (i, 0, j)),
            pl.BlockSpec((1, 1, tn), lambda i, j: (i, 0, j)),
        ],
        out_specs=pl.BlockSpec((1, 8, tn), lambda i, j: (i, 0, j)),
        out_shape=jax.ShapeDtypeStruct((depth, 8, n), F32),
        compiler_params=_cparams("parallel", "parallel"),
        name="ada_modulation",
    )(cc, ada_w, ada_b.reshape(depth, 1, n))


def _norm_mod_kernel(x_ref, g_ref, sh_ref, sc_ref, o_ref):
    x = x_ref[0]
    y = x * lax.rsqrt(jnp.mean(x * x, axis=-1, keepdims=True) + NORM_EPS) * g_ref[...]
    o_ref[0] = (y * (1.0 + sc_ref[0]) + sh_ref[0]).astype(o_ref.dtype)


def _norm_mod_router_kernel(x_ref, g_ref, sh_ref, sc_ref, r_ref, o_ref, lg_ref):
    x = x_ref[0]
    y = x * lax.rsqrt(jnp.mean(x * x, axis=-1, keepdims=True) + NORM_EPS) * g_ref[...]
    h = y * (1.0 + sc_ref[0]) + sh_ref[0]
    o_ref[0] = h.astype(o_ref.dtype)
    lg_ref[0] = jnp.dot(h, r_ref[...], preferred_element_type=F32, precision=HIGHEST)


def norm_mod(x, g, shift, scale, out_dtype=BF16, router=None):
    b, l, d = x.shape
    tl = min(512, l)
    in_specs = [
        pl.BlockSpec((1, tl, d), lambda i, j: (i, j, 0)),
        pl.BlockSpec((1, d), lambda i, j: (0, 0)),
        pl.BlockSpec((1, 1, d), lambda i, j: (i, 0, 0)),
        pl.BlockSpec((1, 1, d), lambda i, j: (i, 0, 0)),
    ]
    args = [x, g.reshape(1, d), shift.reshape(b, 1, d), scale.reshape(b, 1, d)]
    h_spec = pl.BlockSpec((1, tl, d), lambda i, j: (i, j, 0))
    h_shape = jax.ShapeDtypeStruct((b, l, d), out_dtype)
    if router is None:
        return pl.pallas_call(
            _norm_mod_kernel, grid=(b, l // tl), in_specs=in_specs, out_specs=h_spec, out_shape=h_shape,
            compiler_params=_cparams("parallel", "parallel"), name="norm_mod",
        )(*args)
    in_specs.append(pl.BlockSpec((d, LANE), lambda i, j: (0, 0)))
    return pl.pallas_call(
        _norm_mod_router_kernel, grid=(b, l // tl), in_specs=in_specs,
        out_specs=[h_spec, pl.BlockSpec((1, tl, LANE), lambda i, j: (i, j, 0))],
        out_shape=[h_shape, jax.ShapeDtypeStruct((b, l, LANE), F32)],
        compiler_params=_cparams("parallel", "parallel"), name="norm_mod_router",
    )(*args, router)


def _rmsnorm_kernel(x_ref, g_ref, o_ref):
    x = x_ref[...]
    o_ref[...] = x * lax.rsqrt(jnp.mean(x * x, axis=-1, keepdims=True) + NORM_EPS) * g_ref[...]


def rmsnorm_rows(x, g):
    m, d = x.shape
    tm = min(512, m)
    return pl.pallas_call(
        _rmsnorm_kernel, grid=(m // tm,),
        in_specs=[pl.BlockSpec((tm, d), lambda i: (i, 0)), pl.BlockSpec((1, d), lambda i: (0, 0))],
        out_specs=pl.BlockSpec((tm, d), lambda i: (i, 0)),
        out_shape=jax.ShapeDtypeStruct((m, d), F32),
        compiler_params=_cparams("parallel"), name="final_rmsnorm",
    )(x, g.reshape(1, d))


def _mm_kernel(a_ref, w_ref, o_ref):
    o_ref[...] = _dot(a_ref[...], w_ref[...]).astype(o_ref.dtype)


def matmul(a, w, out_dtype, tm, tn):
    m, k = a.shape
    n = w.shape[1]
    tm, tn = _tile(m, tm), _tile(n, tn)
    return pl.pallas_call(
        _mm_kernel, grid=(m // tm, n // tn),
        in_specs=[pl.BlockSpec((tm, k), lambda i, j: (i, 0)), pl.BlockSpec((k, tn), lambda i, j: (0, j))],
        out_specs=pl.BlockSpec((tm, tn), lambda i, j: (i, j)),
        out_shape=jax.ShapeDtypeStruct((m, n), out_dtype),
        compiler_params=_cparams("parallel", "parallel"), name="matmul",
    )(a, w)


def _swiglu_up_kernel(a_ref, wg_ref, wu_ref, o_ref):
    a = a_ref[...]
    g = _dot(a, wg_ref[...])
    u = _dot(a, wu_ref[...])
    o_ref[...] = (_silu(g) * u).astype(o_ref.dtype)


def swiglu_up(a, wg, wu, tm, tn):
    m, k = a.shape
    n = wg.shape[1]
    tm, tn = _tile(m, tm), _tile(n, tn)
    return pl.pallas_call(
        _swiglu_up_kernel, grid=(m // tm, n // tn),
        in_specs=[pl.BlockSpec((tm, k), lambda i, j: (i, 0)),
                  pl.BlockSpec((k, tn), lambda i, j: (0, j)),
                  pl.BlockSpec((k, tn), lambda i, j: (0, j))],
        out_specs=pl.BlockSpec((tm, tn), lambda i, j: (i, j)),
        out_shape=jax.ShapeDtypeStruct((m, n), BF16),
        compiler_params=_cparams("parallel", "parallel"), name="swiglu_up",
    )(a, wg, wu)


def _down_res_kernel(a_ref, w_ref, x_ref, gate_ref, o_ref, acc_ref):
    k = pl.program_id(2)

    @pl.when(k == 0)
    def _():
        acc_ref[...] = jnp.zeros_like(acc_ref)

    acc_ref[...] += _dot(a_ref[...], w_ref[...])

    @pl.when(k == pl.num_programs(2) - 1)
    def _():
        o_ref[...] = x_ref[...] + gate_ref[0] * acc_ref[...]


def down_residual(a, w, x, gate, rows_per_batch, tm, tn, tk):
    m, kdim = a.shape
    n = w.shape[1]
    tm, tn, tk = _tile(rows_per_batch, tm), _tile(n, tn), _tile(kdim, tk)
    bsz = gate.shape[0]
    return pl.pallas_call(
        _down_res_kernel, grid=(m // tm, n // tn, kdim // tk),
        in_specs=[pl.BlockSpec((tm, tk), lambda i, j, k: (i, k)),
                  pl.BlockSpec((tk, tn), lambda i, j, k: (k, j)),
                  pl.BlockSpec((tm, tn), lambda i, j, k: (i, j)),
                  pl.BlockSpec((1, 1, tn), lambda i, j, k: ((i * tm) // rows_per_batch, 0, j))],
        out_specs=pl.BlockSpec((tm, tn), lambda i, j, k: (i, j)),
        out_shape=jax.ShapeDtypeStruct((m, n), F32),
        scratch_shapes=[pltpu.VMEM((tm, tn), F32)],
        compiler_params=_cparams("parallel", "parallel", "arbitrary"), name="down_residual",
    )(a, w, x, gate.reshape(bsz, 1, n))


def _mix_out_kernel(a_ref, b_ref, c_ref, w_ref, x_ref, gate_ref, o_ref):
    ka, kb = a_ref.shape[1], b_ref.shape[1]
    acc = _dot(a_ref[...], w_ref[0:ka, :])
    acc += _dot(b_ref[...], w_ref[ka:ka + kb, :])
    acc += _dot(c_ref[...], w_ref[ka + kb:, :])
    o_ref[...] = x_ref[...] + gate_ref[0] * acc


def mix_out_residual(oa, ob, oc, w, x, gate, rows_per_batch, tm, tn):
    m = oa.shape[0]
    kdim, n = w.shape
    tm, tn = _tile(rows_per_batch, tm), _tile(n, tn)
    bsz = gate.shape[0]
    return pl.pallas_call(
        _mix_out_kernel, grid=(m // tm, n // tn),
        in_specs=[pl.BlockSpec((tm, oa.shape[1]), lambda i, j: (i, 0)),
                  pl.BlockSpec((tm, ob.shape[1]), lambda i, j: (i, 0)),
                  pl.BlockSpec((tm, oc.shape[1]), lambda i, j: (i, 0)),
                  pl.BlockSpec((kdim, tn), lambda i, j: (0, j)),
                  pl.BlockSpec((tm, tn), lambda i, j: (i, j)),
                  pl.BlockSpec((1, 1, tn), lambda i, j: ((i * tm) // rows_per_batch, 0, j))],
        out_specs=pl.BlockSpec((tm, tn), lambda i, j: (i, j)),
        out_shape=jax.ShapeDtypeStruct((m, n), F32),
        compiler_params=_cparams("parallel", "parallel"), name="mix_out_residual",
    )(oa, ob, oc, w, x, gate.reshape(bsz, 1, n))


def _gelu_tanh(x):
    return 0.5 * x * (1.0 + jnp.tanh(math.sqrt(2.0 / math.pi) * (x + 0.044715 * (x * x * x))))


def _sgu_kernel(z_ref, g_ref, w_ref, b_ref, o_ref):
    tl = z_ref.shape[1]
    for c in range(tl // A_CHUNK):
        rows = slice(c * A_CHUNK, (c + 1) * A_CHUNK)
        for g in range(A_GROUPS):
            cols = slice(g * A_GROUP_DIM, (g + 1) * A_GROUP_DIM)
            u = _gelu_tanh(z_ref[0, rows, cols].astype(F32))
            v = _gelu_tanh(z_ref[0, rows, A_WIDTH + g * A_GROUP_DIM:A_WIDTH + (g + 1) * A_GROUP_DIM].astype(F32))
            vc = v - jnp.mean(v, axis=-1, keepdims=True)
            vn = vc * lax.rsqrt(jnp.mean(vc * vc, axis=-1, keepdims=True) + NORM_EPS) * g_ref[:, cols]
            mixed = _dot(w_ref[g], vn.astype(BF16)) + b_ref[g]
            o_ref[0, rows, cols] = (u * mixed).astype(o_ref.dtype)


def chunk_sgu(z, norm_g, w_s, b_s):
    b, l, _ = z.shape
    tl = min(512, l)
    blk = ZC_A // (2 * A_WIDTH)
    return pl.pallas_call(
        _sgu_kernel, grid=(b, l // tl),
        in_specs=[pl.BlockSpec((1, tl, 2 * A_WIDTH), lambda i, j: (i, j, blk)),
                  pl.BlockSpec((1, A_WIDTH), lambda i, j: (0, 0)),
                  pl.BlockSpec((A_GROUPS, A_CHUNK, A_CHUNK), lambda i, j: (0, 0, 0)),
                  pl.BlockSpec((A_GROUPS, A_CHUNK, 1), lambda i, j: (0, 0, 0))],
        out_specs=pl.BlockSpec((1, tl, A_WIDTH), lambda i, j: (i, j, 0)),
        out_shape=jax.ShapeDtypeStruct((b, l, A_WIDTH), BF16),
        compiler_params=_cparams("parallel", "parallel"), name="chunk_sgu",
    )(z, norm_g.reshape(1, A_WIDTH), w_s.astype(BF16), b_s.reshape(A_GROUPS, A_CHUNK, 1))


def _mla_prep_kernel(z_ref, qg_ref, kvg_ref, wq_ref, wkv_ref, cos_ref, sin_ref, q_ref, k_ref, v_ref, *, q_scale):
    lat = z_ref[0, :, 0:Q_LORA].astype(F32)
    cq = (lat * lax.rsqrt(jnp.mean(lat * lat, axis=-1, keepdims=True) + NORM_EPS) * qg_ref[...]).astype(BF16)
    kvl = z_ref[0, :, Q_LORA:Q_LORA + KV_LORA].astype(F32)
    ckv = (kvl * lax.rsqrt(jnp.mean(kvl * kvl, axis=-1, keepdims=True) + NORM_EPS) * kvg_ref[...]).astype(BF16)
    cos = cos_ref[...]
    sin = sin_ref[...]
    base = Q_LORA + KV_LORA
    k_pe = z_ref[0, :, base:base + LANE].astype(F32) * cos + z_ref[0, :, base + LANE:base + 2 * LANE].astype(F32) * sin
    k_pe = k_pe.astype(k_ref.dtype)
    for h in range(C_HEADS):
        qh = _dot(cq, wq_ref[:, h * C_HEAD_COLS:(h + 1) * C_HEAD_COLS])
        q_pe = qh[:, LANE:2 * LANE] * cos + qh[:, 2 * LANE:3 * LANE] * sin
        q_ref[0, h, :, 0:LANE] = (qh[:, 0:LANE] * q_scale).astype(q_ref.dtype)
        q_ref[0, h, :, LANE:2 * LANE] = (q_pe * q_scale).astype(q_ref.dtype)
        kv = _dot(ckv, wkv_ref[:, h * 2 * LANE:(h + 1) * 2 * LANE])
        k_ref[0, h, :, 0:LANE] = kv[:, 0:LANE].astype(k_ref.dtype)
        k_ref[0, h, :, LANE:2 * LANE] = k_pe
        v_ref[0, h] = kv[:, LANE:2 * LANE].astype(v_ref.dtype)


def mla_prep(z, q_norm_g, kv_norm_g, wq_arr, wkv_arr, cos2, sin2):
    b, l, _ = z.shape
    tl = min(256, l)
    blk = ZC_C // 1024
    q_scale = (C_NOPE + C_ROPE) ** -0.5 * math.log2(math.e)
    return pl.pallas_call(
        functools.partial(_mla_prep_kernel, q_scale=q_scale), grid=(b, l // tl),
        in_specs=[pl.BlockSpec((1, tl, 1024), lambda i, j: (i, j, blk)),
                  pl.BlockSpec((1, Q_LORA), lambda i, j: (0, 0)),
                  pl.BlockSpec((1, KV_LORA), lambda i, j: (0, 0)),
                  pl.BlockSpec((Q_LORA, C_HEADS * C_HEAD_COLS), lambda i, j: (0, 0)),
                  pl.BlockSpec((KV_LORA, C_HEADS * 2 * LANE), lambda i, j: (0, 0)),
                  pl.BlockSpec((tl, LANE), lambda i, j: (j, 0)),
                  pl.BlockSpec((tl, LANE), lambda i, j: (j, 0))],
        out_specs=[pl.BlockSpec((1, C_HEADS, tl, 2 * LANE), lambda i, j: (i, 0, j, 0)),
                   pl.BlockSpec((1, C_HEADS, tl, 2 * LANE), lambda i, j: (i, 0, j, 0)),
                   pl.BlockSpec((1, C_HEADS, tl, LANE), lambda i, j: (i, 0, j, 0))],
        out_shape=[jax.ShapeDtypeStruct((b, C_HEADS, l, 2 * LANE), BF16),
                   jax.ShapeDtypeStruct((b, C_HEADS, l, 2 * LANE), BF16),
                   jax.ShapeDtypeStruct((b, C_HEADS, l, LANE), BF16)],
        compiler_params=_cparams("parallel", "parallel"), name="mla_prep",
    )(z, q_norm_g.reshape(1, Q_LORA), kv_norm_g.reshape(1, KV_LORA), wq_arr, wkv_arr, cos2, sin2)


FLASH_ROWS = 32


def _flash_kernel(q_ref, k_ref, v_ref, o_ref, s_ref, p_ref, m_ref, l_ref, acc_ref, *, tk, nk):
    tq = q_ref.shape[2]
    rb = min(FLASH_ROWS, tq)
    m_ref[...] = jnp.full(m_ref.shape, -jnp.inf, F32)
    l_ref[...] = jnp.zeros(l_ref.shape, F32)
    acc_ref[...] = jnp.zeros(acc_ref.shape, F32)

    for j in range(nk):
        s_buf, p_buf = s_ref.at[j % 2], p_ref.at[j % 2]
        keys = slice(j * tk, (j + 1) * tk)
        s_buf[...] = _dot_nt(q_ref[0, 0], k_ref[0, 0, keys, :])
        for r in range(tq // rb):
            rows = slice(r * rb, (r + 1) * rb)
            blocks = [s_buf[rows, t * LANE:(t + 1) * LANE] for t in range(tk // LANE)]
            mx = blocks[0]
            for blk in blocks[1:]:
                mx = jnp.maximum(mx, blk)
            m_old = m_ref[rows, :]
            m_new = jnp.maximum(m_old, jnp.broadcast_to(jnp.max(mx, axis=-1, keepdims=True), (rb, LANE)))
            alpha = jnp.exp2(m_old - m_new)
            lane_sum = None
            for t, blk in enumerate(blocks):
                p = jnp.exp2(blk - m_new)
                lane_sum = p if lane_sum is None else lane_sum + p
                p_buf[rows, t * LANE:(t + 1) * LANE] = p.astype(BF16)
            l_ref[rows, :] = alpha * l_ref[rows, :] + lane_sum
            m_ref[rows, :] = m_new
            acc_ref[rows, :] = alpha * acc_ref[rows, :]
        acc_ref[...] += _dot(p_buf[...], v_ref[0, 0, keys, :])
    o_ref[0] = (acc_ref[...] / jnp.sum(l_ref[...], axis=-1, keepdims=True)).astype(o_ref.dtype)


def flash_attention(q, k, v):
    b, h, lq, dq = q.shape
    lk = k.shape[2]
    tq = min(512, lq)
    tk = _tile(lk, 768)
    return pl.pallas_call(
        functools.partial(_flash_kernel, tk=tk, nk=lk // tk), grid=(b, h, lq // tq),
        in_specs=[pl.BlockSpec((1, 1, tq, dq), lambda i, j, t: (i, j, t, 0)),
                  pl.BlockSpec((1, 1, lk, dq), lambda i, j, t: (i, j, 0, 0)),
                  pl.BlockSpec((1, 1, lk, C_V), lambda i, j, t: (i, j, 0, 0))],
        out_specs=pl.BlockSpec((1, tq, C_V), lambda i, j, t: (i, t, j)),
        out_shape=jax.ShapeDtypeStruct((b, lq, h * C_V), BF16),
        scratch_shapes=[pltpu.VMEM((2, tq, tk), F32), pltpu.VMEM((2, tq, tk), BF16), pltpu.VMEM((tq, LANE), F32),
                        pltpu.VMEM((tq, LANE), F32), pltpu.VMEM((tq, C_V), F32)],
        compiler_params=_cparams("parallel", "parallel", "parallel"), name="flash_attention",
    )(q, k, v)


def _softplus(x):
    return jnp.maximum(x, 0.0) + jnp.log1p(jnp.exp(-jnp.abs(x)))


def _dn_prep_kernel(zm_ref, zp_ref, zn_ref, zg_ref, w_ref, alog_ref, dtb_ref, qkv_ref, gb_ref):
    j = pl.program_id(1)
    tl = zm_ref.shape[1]
    half = DN_CONV // 2
    keep_prev = (j > 0).astype(F32)
    keep_next = (j < pl.num_programs(1) - 1).astype(F32)
    for c in range(3 * B_HEADS):
        cols = slice(c * LANE, (c + 1) * LANE)
        prev = zp_ref[0, :, cols].astype(F32)[8:16] * keep_prev
        nxt = zn_ref[0, :, cols].astype(F32)[0:8] * keep_next
        ext = jnp.concatenate([prev, zm_ref[0, :, cols].astype(F32), nxt], axis=0)
        y = ext[8 - half:8 - half + tl] * w_ref[0:1, cols]
        for i in range(1, DN_CONV):
            y = y + ext[8 - half + i:8 - half + i + tl] * w_ref[i:i + 1, cols]
        y = _silu(y)
        if c < 2 * B_HEADS:
            y = y * lax.rsqrt(jnp.sum(y * y, axis=-1, keepdims=True) + NORM_EPS)
        qkv_ref[0, :, cols] = y.astype(qkv_ref.dtype)
    zg = zg_ref[0]
    lane = lax.broadcasted_iota(jnp.int32, zg.shape, 1)
    g = -jnp.exp(alog_ref[...]) * _softplus(zg + dtb_ref[...])
    gb_ref[0] = jnp.where(lane < 2 * B_HEADS, g, _sigmoid(zg))


def deltanet_prep(z, zg, conv_w, a_log, dt_bias):
    b, l, _ = z.shape
    tl = min(256, l)
    wq = 3 * B_WIDTH
    nb16 = l // 16
    pad = LANE - 2 * B_HEADS
    alog = jnp.pad(a_log.reshape(1, -1), ((0, 0), (0, pad)))
    dtb = jnp.pad(dt_bias.reshape(1, -1), ((0, 0), (0, pad)))
    return pl.pallas_call(
        _dn_prep_kernel, grid=(b, l // tl),
        in_specs=[pl.BlockSpec((1, tl, wq), lambda i, j: (i, j, 0)),
                  pl.BlockSpec((1, 16, wq), lambda i, j: (i, jnp.maximum(j * (tl // 16) - 1, 0), 0)),
                  pl.BlockSpec((1, 16, wq), lambda i, j: (i, jnp.minimum((j + 1) * (tl // 16), nb16 - 1), 0)),
                  pl.BlockSpec((1, tl, LANE), lambda i, j: (i, j, 0)),
                  pl.BlockSpec((DN_CONV, wq), lambda i, j: (0, 0)),
                  pl.BlockSpec((1, LANE), lambda i, j: (0, 0)),
                  pl.BlockSpec((1, LANE), lambda i, j: (0, 0))],
        out_specs=[pl.BlockSpec((1, tl, wq), lambda i, j: (i, j, 0)),
                   pl.BlockSpec((1, tl, LANE), lambda i, j: (i, j, 0))],
        out_shape=[jax.ShapeDtypeStruct((b, l, wq), BF16), jax.ShapeDtypeStruct((b, l, LANE), F32)],
        compiler_params=_cparams("parallel", "parallel"), name="deltanet_prep",
    )(z, z, z, zg, conv_w, alog, dtb)


def _dot_bf16(a, b):
    return _dot(a.astype(BF16), b.astype(BF16))


def _dn_intra_kernel(qkv_ref, gb_ref, u_ref, w_ref, qg_ref, kg_ref, qk_ref, gl_ref, *, chunks, passes):
    c = DN_CHUNK
    row = lax.broadcasted_iota(jnp.int32, (c, LANE), 0)
    lane = lax.broadcasted_iota(jnp.int32, (c, LANE), 1)
    col = jnp.where(lane < c, lane, lane - c)
    fwd = lane < c
    bwd = jnp.logical_not(fwd)
    incl = (fwd & (row >= col)) | (bwd & (row <= col))
    strict = (fwd & (row > col)) | (bwd & (row < col))
    same16 = lax.shift_right_logical(row, 4) == lax.shift_right_logical(col, 4)
    same32 = lax.shift_right_logical(row, 5) == lax.shift_right_logical(col, 5)
    diag = row == col
    eye2 = diag.astype(F32)
    r64 = lax.broadcasted_iota(jnp.int32, (c, c), 0)
    c64 = lax.broadcasted_iota(jnp.int32, (c, c), 1)
    tri_lo = (r64 >= c64).astype(F32)
    tri_up = (r64 <= c64).astype(F32)
    scale = B_HEAD_DIM ** -0.5
    dotp = _dot_x3 if passes == 3 else _dot_bf16

    def pick(x, idx):
        return jnp.sum(jnp.where(lane == idx, x, 0.0), axis=-1, keepdims=True)

    def blockdiag(y2):
        return jnp.concatenate([jnp.where(fwd, y2, 0.0), jnp.where(fwd, 0.0, y2)], axis=0)

    def mm(xs, ys):
        return [dotp(x2, blockdiag(y2)) for x2, y2 in zip(xs, ys)]

    rows = [slice(ci * c, (ci + 1) * c) for ci in range(chunks)]
    gb = [gb_ref[0, rs, :] for rs in rows]
    cum_f = [jnp.dot(tri_lo, g, preferred_element_type=F32, precision=HIGHEST) for g in gb]
    cum_b = [jnp.dot(tri_up, g, preferred_element_type=F32, precision=HIGHEST) for g in gb]
    tot_f = [x[c - 1:c, :] for x in cum_f]
    tot_b = [x[0:1, :] for x in cum_b]

    units = [(ci, h) for ci in range(chunks) for h in range(B_HEADS)]
    q = [qkv_ref[0, rows[ci], h * LANE:(h + 1) * LANE] for ci, h in units]
    k = [qkv_ref[0, rows[ci], B_WIDTH + h * LANE:B_WIDTH + (h + 1) * LANE] for ci, h in units]
    v = [qkv_ref[0, rows[ci], 2 * B_WIDTH + h * LANE:2 * B_WIDTH + (h + 1) * LANE].astype(F32) for ci, h in units]
    k2 = [jnp.concatenate([x, x], axis=0) for x in k]
    kk2 = [_dot_nt(x, y) for x, y in zip(k, k2)]
    qk2 = [_dot_nt(x, y) for x, y in zip(q, k2)]
    cf = [pick(cum_f[ci], h) for ci, h in units]
    cb = [pick(cum_b[ci], B_HEADS + h) for ci, h in units]
    bf = [pick(gb[ci], 2 * B_HEADS + h) for ci, h in units]
    bb = [pick(gb[ci], 3 * B_HEADS + h) for ci, h in units]
    lf = [pick(jnp.broadcast_to(tot_f[ci], (c, LANE)), h) for ci, h in units]
    lb = [pick(jnp.broadcast_to(tot_b[ci], (c, LANE)), B_HEADS + h) for ci, h in units]
    c2 = [jnp.where(fwd, x, y) for x, y in zip(cf, cb)]
    r2 = [jnp.sum(jnp.where(diag, x, 0.0), axis=0, keepdims=True) for x in c2]
    decay2 = [jnp.where(incl, jnp.exp(jnp.where(incl, x - y, 0.0)), 0.0) for x, y in zip(c2, r2)]
    l2 = [jnp.where(strict, jnp.where(fwd, x, y) * kk * dc, 0.0) for x, y, kk, dc in zip(bf, bb, kk2, decay2)]
    mp = [jnp.where(same16, -x, 0.0) for x in l2]
    p = [eye2 + x for x in mp]
    for _ in range(3):
        mp = mm(mp, mp)
        p = [x + y for x, y in zip(p, mm(p, mp))]
    off = [jnp.where(same32 & jnp.logical_not(same16), x, 0.0) for x in l2]
    p = [x - y for x, y in zip(p, mm(mm(p, off), p))]
    off = [jnp.where(same32, 0.0, x) for x in l2]
    p = [x - y for x, y in zip(p, mm(mm(p, off), p))]
    ef = [jnp.exp(x) for x in cf]
    eb = [jnp.exp(x) for x in cb]
    zero = jnp.zeros((c, LANE), F32)
    sol = []
    for i in range(len(units)):
        kf = k[i].astype(F32)
        rhs = jnp.concatenate([
            jnp.concatenate([v[i] * bf[i], kf * (bf[i] * ef[i]), zero, zero], axis=1),
            jnp.concatenate([zero, zero, v[i] * bb[i], kf * (bb[i] * eb[i])], axis=1)], axis=0)
        sol.append(dotp(p[i], rhs))
    for i, (ci, h) in enumerate(units):
        hc = slice(h * LANE, (h + 1) * LANE)
        rs = rows[ci]
        kf = k[i].astype(F32)
        qf = q[i].astype(F32)
        u_ref[0, 0, rs, hc] = sol[i][:, 0:LANE]
        w_ref[0, 0, rs, hc] = sol[i][:, LANE:2 * LANE].astype(w_ref.dtype)
        u_ref[1, 0, rs, hc] = sol[i][:, 2 * LANE:3 * LANE]
        w_ref[1, 0, rs, hc] = sol[i][:, 3 * LANE:4 * LANE].astype(w_ref.dtype)
        qg_ref[0, 0, rs, hc] = (qf * (ef[i] * scale)).astype(qg_ref.dtype)
        qg_ref[1, 0, rs, hc] = (qf * (eb[i] * scale)).astype(qg_ref.dtype)
        kg_ref[0, 0, rs, hc] = (kf * jnp.exp(lf[i] - cf[i])).astype(kg_ref.dtype)
        kg_ref[1, 0, rs, hc] = (kf * jnp.exp(lb[i] - cb[i])).astype(kg_ref.dtype)
        qk_ref[0, rs, hc] = (qk2[i] * decay2[i] * scale).astype(qk_ref.dtype)
    r8 = lax.broadcasted_iota(jnp.int32, (2 * 8, LANE), 0)
    l8 = lax.broadcasted_iota(jnp.int32, (2 * 8, LANE), 1)
    want = jnp.where(r8 < 8, r8, r8 - 8 + B_HEADS)
    for ci in range(chunks):
        src = jnp.concatenate([jnp.broadcast_to(tot_f[ci], (8, LANE)), jnp.broadcast_to(tot_b[ci], (8, LANE))], axis=0)
        tot = jnp.sum(jnp.where(l8 == want, src, 0.0), axis=-1, keepdims=True)
        gl_ref[0, ci] = jnp.broadcast_to(jnp.exp(tot), (2 * 8, LANE))


DN_INTRA_CHUNKS = 2
DN_INTRA_PASSES = 1


def deltanet_intra(qkv, gb):
    b, l, _ = qkv.shape
    nc = l // DN_CHUNK
    chunks = DN_INTRA_CHUNKS
    assert nc % chunks == 0
    tl = chunks * DN_CHUNK
    dir_spec = pl.BlockSpec((2, 1, tl, B_WIDTH), lambda i, j: (0, i, j, 0))
    return pl.pallas_call(
        functools.partial(_dn_intra_kernel, chunks=chunks, passes=DN_INTRA_PASSES), grid=(b, nc // chunks),
        in_specs=[pl.BlockSpec((1, tl, 3 * B_WIDTH), lambda i, j: (i, j, 0)),
                  pl.BlockSpec((1, tl, LANE), lambda i, j: (i, j, 0))],
        out_specs=[dir_spec, dir_spec, dir_spec, dir_spec,
                   pl.BlockSpec((1, tl, B_WIDTH), lambda i, j: (i, j, 0)),
                   pl.BlockSpec((1, chunks, 16, LANE), lambda i, j: (i, j, 0, 0))],
        out_shape=[jax.ShapeDtypeStruct((2, b, l, B_WIDTH), F32),
                   jax.ShapeDtypeStruct((2, b, l, B_WIDTH), BF16),
                   jax.ShapeDtypeStruct((2, b, l, B_WIDTH), BF16),
                   jax.ShapeDtypeStruct((2, b, l, B_WIDTH), BF16),
                   jax.ShapeDtypeStruct((b, l, B_WIDTH), BF16),
                   jax.ShapeDtypeStruct((b, nc, 16, LANE), F32)],
        compiler_params=_cparams("parallel", "parallel"), name="deltanet_intra",
    )(qkv, gb)


def _dn_scan_kernel(uf_ref, wf_ref, qgf_ref, kgf_ref, qkf_ref, glf_ref,
                    ub_ref, wb_ref, qgb_ref, kgb_ref, qkb_ref, glb_ref, s0_ref,
                    of_ref, ob_ref, sfin_ref, s_ref):
    j = pl.program_id(1)
    c = DN_CHUNK

    @pl.when(j == 0)
    def _():
        s_ref[...] = s0_ref[0]

    lane = lax.broadcasted_iota(jnp.int32, (c, LANE), 1)
    zeros_b = jnp.zeros((c, LANE), BF16)
    dirs = ((uf_ref, wf_ref, qgf_ref, kgf_ref, qkf_ref, glf_ref, of_ref),
            (ub_ref, wb_ref, qgb_ref, kgb_ref, qkb_ref, glb_ref, ob_ref))
    units = [(d, h) for d in range(2) for h in range(B_HEADS)]
    cols = [slice(h * LANE, (h + 1) * LANE) for _, h in units]
    state = [s_ref[d * B_HEADS + h] for d, h in units]
    r = [_dot(jnp.concatenate([dirs[d][1][0, 0, :, hc], dirs[d][2][0, 0, :, hc]], axis=0), s.astype(BF16))
         for (d, _), hc, s in zip(units, cols, state)]
    vb = [(dirs[d][0][0, 0, :, hc] - ri[0:c]).astype(BF16) for (d, _), hc, ri in zip(units, cols, r)]
    for (d, h), hc, s, ri, vi in zip(units, cols, state, r, vb):
        qk2 = dirs[d][4][0, :, hc]
        if d == 0:
            intra = _dot(jnp.where(lane < c, qk2, jnp.zeros_like(qk2)), jnp.concatenate([vi, zeros_b], axis=0))
        else:
            intra = _dot(jnp.where(lane < c, jnp.zeros_like(qk2), qk2), jnp.concatenate([zeros_b, vi], axis=0))
        dirs[d][6][0, :, hc] = ri[c:2 * c] + intra
        gl = dirs[d][5][0, 0, d * 8 + h:d * 8 + h + 1, :]
        s_ref[d * B_HEADS + h] = s * gl + _dot_tn(dirs[d][3][0, 0, :, hc], vi)

    @pl.when(j == pl.num_programs(1) - 1)
    def _():
        sfin_ref[0] = s_ref[...]


def deltanet_scan(u, w, qg, kg, qk, gl, s0):
    _, b, l, _ = u.shape
    nc = l // DN_CHUNK
    fdir = pl.BlockSpec((1, 1, DN_CHUNK, B_WIDTH), lambda i, j: (0, i, j, 0))
    bdir = pl.BlockSpec((1, 1, DN_CHUNK, B_WIDTH), lambda i, j: (1, i, nc - 1 - j, 0))
    fqk = pl.BlockSpec((1, DN_CHUNK, B_WIDTH), lambda i, j: (i, j, 0))
    bqk = pl.BlockSpec((1, DN_CHUNK, B_WIDTH), lambda i, j: (i, nc - 1 - j, 0))
    fgl = pl.BlockSpec((1, 1, 16, LANE), lambda i, j: (i, j, 0, 0))
    bgl = pl.BlockSpec((1, 1, 16, LANE), lambda i, j: (i, nc - 1 - j, 0, 0))
    st = pl.BlockSpec((1, 2 * B_HEADS, B_HEAD_DIM, B_HEAD_DIM), lambda i, j: (i, 0, 0, 0))
    return pl.pallas_call(
        _dn_scan_kernel, grid=(b, nc),
        in_specs=[fdir, fdir, fdir, fdir, fqk, fgl, bdir, bdir, bdir, bdir, bqk, bgl, st],
        out_specs=[fqk, bqk, st],
        out_shape=[jax.ShapeDtypeStruct((b, l, B_WIDTH), F32),
                   jax.ShapeDtypeStruct((b, l, B_WIDTH), F32),
                   jax.ShapeDtypeStruct((b, 2 * B_HEADS, B_HEAD_DIM, B_HEAD_DIM), F32)],
        scratch_shapes=[pltpu.VMEM((2 * B_HEADS, B_HEAD_DIM, B_HEAD_DIM), F32)],
        compiler_params=_cparams("parallel", "arbitrary"), name="deltanet_scan",
    )(u, w, qg, kg, qk, gl, u, w, qg, kg, qk, gl, s0)


def _dn_out_kernel(of_ref, ob_ref, gate_ref, g_ref, o_ref):
    for h in range(B_HEADS):
        hc = slice(h * LANE, (h + 1) * LANE)
        o = of_ref[0, :, hc] + ob_ref[0, :, hc]
        y = o * lax.rsqrt(jnp.mean(o * o, axis=-1, keepdims=True) + NORM_EPS) * g_ref[...]
        o_ref[0, :, hc] = (y * _silu(gate_ref[0, :, hc].astype(F32))).astype(o_ref.dtype)


def deltanet_out(o_f, o_b, z, norm_g):
    b, l, _ = o_f.shape
    tl = min(512, l)
    blk = ZC_GATE // B_WIDTH
    return pl.pallas_call(
        _dn_out_kernel, grid=(b, l // tl),
        in_specs=[pl.BlockSpec((1, tl, B_WIDTH), lambda i, j: (i, j, 0)),
                  pl.BlockSpec((1, tl, B_WIDTH), lambda i, j: (i, j, 0)),
                  pl.BlockSpec((1, tl, B_WIDTH), lambda i, j: (i, j, blk)),
                  pl.BlockSpec((1, B_HEAD_DIM), lambda i, j: (0, 0))],
        out_specs=pl.BlockSpec((1, tl, B_WIDTH), lambda i, j: (i, j, 0)),
        out_shape=jax.ShapeDtypeStruct((b, l, B_WIDTH), BF16),
        compiler_params=_cparams("parallel", "parallel"), name="deltanet_out",
    )(o_f, o_b, z, norm_g.reshape(1, B_HEAD_DIM))


def _route_kernel(lg_ref, info_ref, cnt_ref, carry_ref):
    i = pl.program_id(0)

    @pl.when(i == 0)
    def _():
        carry_ref[...] = jnp.zeros_like(carry_ref)

    lg = lg_ref[...]
    tl = lg.shape[0]
    lane = lax.broadcasted_iota(jnp.int32, lg.shape, 1)
    valid = lane < N_EXPERTS
    lg = jnp.where(valid, lg, -jnp.inf)
    e = jnp.exp(lg - jnp.max(lg, axis=-1, keepdims=True))
    p = e / jnp.sum(e, axis=-1, keepdims=True)
    p = jnp.where(valid, p, -1.0)
    p1 = jnp.max(p, axis=-1, keepdims=True)
    i1 = jnp.min(jnp.where(p == p1, lane, LANE), axis=-1, keepdims=True)
    pm = jnp.where(lane == i1, -1.0, p)
    p2 = jnp.max(pm, axis=-1, keepdims=True)
    i2 = jnp.min(jnp.where(pm == p2, lane, LANE), axis=-1, keepdims=True)
    tot = p1 + p2
    w1, w2 = p1 / tot, p2 / tot
    hit1, hit2 = lane == i1, lane == i2
    onehot = (hit1 | hit2).astype(F32)
    r = lax.broadcasted_iota(jnp.int32, (tl, tl), 0)
    c = lax.broadcasted_iota(jnp.int32, (tl, tl), 1)
    before = _dot((r > c).astype(BF16), onehot.astype(BF16)) + carry_ref[...]
    r1 = jnp.sum(jnp.where(hit1, before, 0.0), axis=-1, keepdims=True)
    r2 = jnp.sum(jnp.where(hit2, before, 0.0), axis=-1, keepdims=True)
    carry_ref[...] += jnp.sum(onehot, axis=0, keepdims=True)
    cnt_ref[...] = jnp.broadcast_to(carry_ref[...], cnt_ref.shape)
    info = jnp.where(lane == 0, i1.astype(F32), 0.0)
    info = jnp.where(lane == 1, i2.astype(F32), info)
    info = jnp.where(lane == 2, r1, info)
    info = jnp.where(lane == 3, r2, info)
    info = jnp.where(lane == 4, w1, info)
    info = jnp.where(lane == 5, w2, info)
    info_ref[...] = info


def moe_route(logits):
    t = logits.shape[0]
    tl = min(512, t)
    return pl.pallas_call(
        _route_kernel, grid=(t // tl,),
        in_specs=[pl.BlockSpec((tl, LANE), lambda i: (i, 0))],
        out_specs=[pl.BlockSpec((tl, LANE), lambda i: (i, 0)), pl.BlockSpec((8, LANE), lambda i: (0, 0))],
        out_shape=[jax.ShapeDtypeStruct((t, LANE), F32), jax.ShapeDtypeStruct((8, LANE), F32)],
        scratch_shapes=[pltpu.VMEM((1, LANE), F32)],
        compiler_params=_cparams("arbitrary"), name="moe_route",
    )(logits)


def _dispatch_kernel(pos_ref, h_ref, xs_in_ref, xs_ref, sem):
    del xs_in_ref
    tb = pos_ref.shape[2] // TOP_K

    def copy(src_row, dst_row):
        return pltpu.make_async_copy(h_ref.at[pl.ds(pl.multiple_of(src_row * ROW_TILE, ROW_TILE), ROW_TILE)],
                                     xs_ref.at[pl.ds(pl.multiple_of(dst_row * ROW_TILE, ROW_TILE), ROW_TILE)], sem)

    def issue(j, carry):
        for k in range(TOP_K):
            copy(j, pos_ref[0, 0, TOP_K * j + k]).start()
        return carry

    lax.fori_loop(0, tb, issue, 0)

    def drain(j, carry):
        copy(0, 0).wait()
        return carry

    lax.fori_loop(0, TOP_K * tb, drain, 0)


def moe_dispatch(h_rows, pos, n_rows):
    t = pos.shape[0]
    tb = min(256, t)
    d_rows = h_rows.shape[0] // t
    assert d_rows == ROW_TILE
    zeros = jnp.zeros((n_rows * ROW_TILE, LANE), h_rows.dtype)
    return pl.pallas_call(
        _dispatch_kernel, grid=(t // tb,),
        in_specs=[pl.BlockSpec((1, 1, TOP_K * tb), lambda i: (i, 0, 0), memory_space=pltpu.SMEM),
                  pl.BlockSpec((tb * ROW_TILE, LANE), lambda i: (i, 0)),
                  pl.BlockSpec(memory_space=pl.ANY)],
        out_specs=pl.BlockSpec(memory_space=pl.ANY),
        out_shape=jax.ShapeDtypeStruct(zeros.shape, zeros.dtype),
        scratch_shapes=[pltpu.SemaphoreType.DMA],
        input_output_aliases={2: 0},
        compiler_params=_cparams("arbitrary"), name="moe_dispatch",
    )(pos.reshape(t // tb, 1, TOP_K * tb), h_rows, zeros)


def _moe_up_kernel(te_ref, tv_ref, xs_ref, wg_ref, wu_ref, o_ref):
    del te_ref
    i = pl.program_id(0)

    @pl.when(tv_ref[i] != 0)
    def _():
        a = xs_ref[...]
        g = _dot(a, wg_ref[0])
        u = _dot(a, wu_ref[0])
        o_ref[...] = (_silu(g) * u).astype(o_ref.dtype)

    @pl.when(tv_ref[i] == 0)
    def _():
        o_ref[...] = jnp.zeros_like(o_ref)


def moe_up(xs, wg, wu, tile_expert, tile_valid, tm, tn):
    r, d = xs.shape
    n = wg.shape[2]
    tn = _tile(n, tn)
    grid_spec = pltpu.PrefetchScalarGridSpec(
        num_scalar_prefetch=2, grid=(r // tm, n // tn),
        in_specs=[pl.BlockSpec((tm, d), lambda i, j, te, tv: (i, 0)),
                  pl.BlockSpec((1, d, tn), lambda i, j, te, tv: (te[i], 0, j)),
                  pl.BlockSpec((1, d, tn), lambda i, j, te, tv: (te[i], 0, j))],
        out_specs=pl.BlockSpec((tm, tn), lambda i, j, te, tv: (i, j)))
    return pl.pallas_call(
        _moe_up_kernel, grid_spec=grid_spec,
        out_shape=jax.ShapeDtypeStruct((r, n), BF16),
        compiler_params=_cparams("parallel", "parallel"), name="moe_up",
    )(tile_expert, tile_valid, xs, wg, wu)


def _moe_down_kernel(te_ref, tv_ref, a_ref, w_ref, o_ref, acc_ref):
    del te_ref
    i = pl.program_id(0)
    k = pl.program_id(2)

    @pl.when(k == 0)
    def _():
        acc_ref[...] = jnp.zeros_like(acc_ref)

    @pl.when(tv_ref[i] != 0)
    def _():
        acc_ref[...] += _dot(a_ref[...], w_ref[0])

    @pl.when(k == pl.num_programs(2) - 1)
    def _():
        o_ref[...] = acc_ref[...]


def moe_down(hid, wd, tile_expert, tile_valid, tm, tn, tk):
    r, kdim = hid.shape
    n = wd.shape[2]
    tn, tk = _tile(n, tn), _tile(kdim, tk)
    grid_spec = pltpu.PrefetchScalarGridSpec(
        num_scalar_prefetch=2, grid=(r // tm, n // tn, kdim // tk),
        in_specs=[pl.BlockSpec((tm, tk), lambda i, j, k, te, tv: (i, k)),
                  pl.BlockSpec((1, tk, tn), lambda i, j, k, te, tv: (te[i], k, j))],
        out_specs=pl.BlockSpec((tm, tn), lambda i, j, k, te, tv: (i, j)),
        scratch_shapes=[pltpu.VMEM((tm, tn), F32)])
    return pl.pallas_call(
        _moe_down_kernel, grid_spec=grid_spec,
        out_shape=jax.ShapeDtypeStruct((r, n), F32),
        compiler_params=_cparams("parallel", "parallel", "arbitrary"), name="moe_down",
    )(tile_expert, tile_valid, hid, wd)


def _combine_kernel(pos_ref, ys_ref, x_ref, gate_ref, wt_ref, o_ref, buf_ref, sem):
    tb = x_ref.shape[0]

    def copy(k, j, src_row):
        return pltpu.make_async_copy(ys_ref.at[pl.ds(pl.multiple_of(src_row * ROW_TILE, ROW_TILE), ROW_TILE)],
                                     buf_ref.at[k, j], sem)

    def issue(j, carry):
        for k in range(TOP_K):
            copy(k, j, pos_ref[0, 0, TOP_K * j + k]).start()
        return carry

    lax.fori_loop(0, tb, issue, 0)

    def drain(j, carry):
        copy(0, 0, 0).wait()
        return carry

    lax.fori_loop(0, TOP_K * tb, drain, 0)
    w = wt_ref[...]
    f = buf_ref[0] * w[:, 0] + buf_ref[1] * w[:, 1]
    o_ref[...] = x_ref[...] + gate_ref[0] * f


def moe_combine(ys_rows, pos, wts, x_rows, gate_rows, rows_per_batch):
    t = pos.shape[0]
    tb = min(128, t)
    return pl.pallas_call(
        _combine_kernel, grid=(t // tb,),
        in_specs=[pl.BlockSpec((1, 1, TOP_K * tb), lambda i: (i, 0, 0), memory_space=pltpu.SMEM),
                  pl.BlockSpec(memory_space=pl.ANY),
                  pl.BlockSpec((tb, ROW_TILE, LANE), lambda i: (i, 0, 0)),
                  pl.BlockSpec((1, ROW_TILE, LANE), lambda i: ((i * tb) // rows_per_batch, 0, 0)),
                  pl.BlockSpec((tb, TOP_K, 1, 1), lambda i: (i, 0, 0, 0))],
        out_specs=pl.BlockSpec((tb, ROW_TILE, LANE), lambda i: (i, 0, 0)),
        out_shape=jax.ShapeDtypeStruct(x_rows.shape, F32),
        scratch_shapes=[pltpu.VMEM((TOP_K, tb, ROW_TILE, LANE), F32), pltpu.SemaphoreType.DMA],
        compiler_params=_cparams("arbitrary"), name="moe_combine",
    )(pos.reshape(t // tb, 1, TOP_K * tb), ys_rows, x_rows, gate_rows, wts.reshape(t, TOP_K, 1, 1))


MOE_TM = 1024


def moe_ffn(h_f32, logits, wg, wu, wd, x_rows, gate, rows_per_batch):
    t, d = h_f32.shape
    tm = min(MOE_TM, t)
    info, counts = moe_route(logits)
    sizes = counts[0, :N_EXPERTS].astype(jnp.int32)
    padded = ((sizes + tm - 1) // tm) * tm
    ends = jnp.cumsum(padded)
    starts = ends - padded
    n_tiles = (t * TOP_K) // tm + N_EXPERTS
    n_rows = n_tiles * tm
    experts = info[:, 0:TOP_K].astype(jnp.int32)
    pos = starts[experts] + info[:, 2:2 + TOP_K].astype(jnp.int32)
    wts = info[:, 4:4 + TOP_K]
    tile_start = jnp.arange(n_tiles, dtype=jnp.int32) * tm
    tile_expert = jnp.minimum(jnp.sum((tile_start[:, None] >= ends[None, :]).astype(jnp.int32), axis=1), N_EXPERTS - 1)
    tile_valid = (tile_start < ends[-1]).astype(jnp.int32)
    xs_rows = moe_dispatch(h_f32.reshape(t * ROW_TILE, LANE), pos, n_rows)
    xs = xs_rows.reshape(n_rows, d).astype(BF16)
    hid = moe_up(xs, wg, wu, tile_expert, tile_valid, tm, 512)
    ys = moe_down(hid, wd, tile_expert, tile_valid, tm, 1024, 1792)
    gate_rows = gate.reshape(gate.shape[0], ROW_TILE, LANE)
    return moe_combine(ys.reshape(n_rows * ROW_TILE, LANE), pos, wts, x_rows, gate_rows, rows_per_batch)


def _rot_cols(w):
    f = ROPE_AXIS_FREQS
    return jnp.concatenate([-w[:, f:2 * f], w[:, 0:f], -w[:, 3 * f:4 * f], w[:, 2 * f:3 * f]], axis=1)


def _arrange_w_in(w):
    d = w.shape[0]
    a = w[:, 0:2 * A_WIDTH]
    off = 2 * A_WIDTH
    qkv_gate = w[:, off:off + 4 * B_WIDTH]
    logit = w[:, off + 4 * B_WIDTH:off + 4 * B_WIDTH + 4 * B_HEADS]
    off = off + 4 * B_WIDTH + 4 * B_HEADS
    lat = w[:, off:off + Q_LORA + KV_LORA]
    k_pe = w[:, off + Q_LORA + KV_LORA:off + Q_LORA + KV_LORA + C_ROPE]
    z64 = jnp.zeros((d, LANE - C_ROPE), w.dtype)
    main = jnp.concatenate([qkv_gate, a, lat, k_pe, z64, _rot_cols(k_pe), z64], axis=1).astype(BF16)
    logit = jnp.pad(logit, ((0, 0), (0, LANE - 4 * B_HEADS))).astype(BF16)
    return main, logit


def _arrange_w_uq(w):
    k = w.shape[0]
    w = w.reshape(k, C_HEADS, C_NOPE + C_ROPE)
    z64 = jnp.zeros((k, C_HEADS, LANE - C_ROPE), w.dtype)
    pe = w[:, :, C_NOPE:]
    pe_rot = jnp.stack([_rot_cols(pe[:, h]) for h in range(C_HEADS)], axis=1)
    return jnp.concatenate([w[:, :, :C_NOPE], pe, z64, pe_rot, z64], axis=2).reshape(k, C_HEADS * C_HEAD_COLS).astype(BF16)


def _rope_tables(n):
    rows = n // GRID_W
    row = jnp.repeat(jnp.arange(rows, dtype=F32), GRID_W)
    col = jnp.tile(jnp.arange(GRID_W, dtype=F32), rows)
    inv = jnp.power(ROPE_BASE, -jnp.arange(ROPE_AXIS_FREQS, dtype=F32) / ROPE_AXIS_FREQS)
    ar = row[:, None] * inv
    ac = col[:, None] * inv
    ang = jnp.concatenate([ar, ar, ac, ac], axis=-1)
    pad = ((0, 0), (0, LANE - C_ROPE))
    return jnp.pad(jnp.cos(ang), pad), jnp.pad(jnp.sin(ang), pad)


def _mixer_branches(z, zg, p, cos2, sin2):
    out_a = chunk_sgu(z, p["sgu_norm_g"], p["sgu_w"], p["sgu_b"])
    qkv, gb = deltanet_prep(z, zg, p["dn_conv_w"], p["dn_a_log"], p["dn_dt_bias"])
    intra = deltanet_intra(qkv, gb)
    q, k, v = mla_prep(z, p["mla_q_norm_g"], p["mla_kv_norm_g"], p["wq_arr"], p["wkv_arr"], cos2, sin2)
    return out_a, intra, (q, k, v)


def kernel(x, c, ctx, c_ctx, ada_w, ada_b, norm1_g, norm2_g, w_in, sgu_norm_g, sgu_w, sgu_b, dn_conv_w, dn_a_log, dn_dt_bias, dn_norm_g, mla_q_norm_g, mla_w_uq, mla_kv_norm_g, mla_w_ukv, w_out, ffn_w_gate, ffn_w_up, ffn_w_down, moe_router, moe_w_gate, moe_w_up, moe_w_down, final_norm_g):
    b, n, d = x.shape
    lc = ctx.shape[1]
    depth = ada_w.shape[0]
    assert d == ROW_TILE * LANE
    cos_lat, sin_lat = _rope_tables(n)
    cos_ctx = jnp.pad(jnp.ones((lc, C_ROPE), F32), ((0, 0), (0, LANE - C_ROPE)))
    sin_ctx = jnp.zeros((lc, LANE), F32)

    cc = jnp.zeros((8, d), F32).at[0:b].set(c).at[b].set(c_ctx)
    mod_all = ada_modulation(cc, ada_w, ada_b)

    xc = ctx
    x_rows = None
    for i in range(depth):
        last = i == depth - 1
        mod = mod_all[i, 0:b].reshape(b, 6, d)
        mod_c = jnp.broadcast_to(mod_all[i, b].reshape(1, 6, d), (b, 6, d))
        w_main, w_logit = _arrange_w_in(w_in[i])
        p = dict(sgu_norm_g=sgu_norm_g[i], sgu_w=sgu_w[i], sgu_b=sgu_b[i], dn_conv_w=dn_conv_w[i],
                 dn_a_log=dn_a_log[i], dn_dt_bias=dn_dt_bias[i], mla_q_norm_g=mla_q_norm_g[i],
                 mla_kv_norm_g=mla_kv_norm_g[i], wq_arr=_arrange_w_uq(mla_w_uq[i]),
                 wkv_arr=mla_w_ukv[i].astype(BF16))
        w_out_b = w_out[i].astype(BF16)

        h = norm_mod(x, norm1_g[i], mod[:, 0], mod[:, 1]).reshape(b * n, d)
        hc = norm_mod(xc, norm1_g[i], mod_c[:, 0], mod_c[:, 1]).reshape(b * lc, d)
        z = matmul(h, w_main, BF16, 1024, 1024).reshape(b, n, Z_COLS)
        zg = matmul(h, w_logit, F32, 1024, LANE).reshape(b, n, LANE)
        zc = matmul(hc, w_main, BF16, 1024, 1024).reshape(b, lc, Z_COLS)
        zgc = matmul(hc, w_logit, F32, 1024, LANE).reshape(b, lc, LANE)

        oa_c, intra_c, (q_c, k_c, v_c) = _mixer_branches(zc, zgc, p, cos_ctx, sin_ctx)
        out_a, intra, (q_l, k_l, v_l) = _mixer_branches(z, zg, p, cos_lat, sin_lat)
        s_zero = jnp.zeros((b, 2 * B_HEADS, B_HEAD_DIM, B_HEAD_DIM), F32)
        ocf, ocb, s_ctx = deltanet_scan(*intra_c, s_zero)
        o_f, o_b, _ = deltanet_scan(*intra, s_ctx)
        out_b = deltanet_out(o_f, o_b, z, dn_norm_g[i])
        out_c = flash_attention(q_l, jnp.concatenate([k_l, k_c], axis=2), jnp.concatenate([v_l, v_c], axis=2))
        x2 = mix_out_residual(out_a.reshape(b * n, -1), out_b.reshape(b * n, -1), out_c.reshape(b * n, -1),
                              w_out_b, x.reshape(b * n, d), mod[:, 2], n, 1024, 1024)
        if not last:
            ob_c = deltanet_out(ocf, ocb, zc, dn_norm_g[i])
            oc_c = flash_attention(q_c, k_c, v_c)
            xc2 = mix_out_residual(oa_c.reshape(b * lc, -1), ob_c.reshape(b * lc, -1), oc_c.reshape(b * lc, -1),
                                   w_out_b, xc.reshape(b * lc, d), mod_c[:, 2], lc, 1024, 1024)

        if i % 2 == 0:
            wg, wu, wd = (ffn_w_gate[i // 2].astype(BF16), ffn_w_up[i // 2].astype(BF16), ffn_w_down[i // 2].astype(BF16))
            h2 = norm_mod(x2.reshape(b, n, d), norm2_g[i], mod[:, 3], mod[:, 4]).reshape(b * n, d)
            hid = swiglu_up(h2, wg, wu, 1024, 512)
            x = down_residual(hid, wd, x2, mod[:, 5], n, 1024, 1024, 1408).reshape(b, n, d)
            if not last:
                hc2 = norm_mod(xc2.reshape(b, lc, d), norm2_g[i], mod_c[:, 3], mod_c[:, 4]).reshape(b * lc, d)
                hid_c = swiglu_up(hc2, wg, wu, 1024, 512)
                xc = down_residual(hid_c, wd, xc2, mod_c[:, 5], lc, 1024, 1024, 1408).reshape(b, lc, d)
        else:
            e = i // 2
            router = jnp.pad(moe_router[e], ((0, 0), (0, LANE - N_EXPERTS)))
            wg, wu, wd = moe_w_gate[e].astype(BF16), moe_w_up[e].astype(BF16), moe_w_down[e].astype(BF16)
            if last:
                h2, logits = norm_mod(x2.reshape(b, n, d), norm2_g[i], mod[:, 3], mod[:, 4], out_dtype=F32, router=router)
                xr = moe_ffn(h2.reshape(b * n, d), logits.reshape(b * n, LANE), wg, wu, wd,
                             x2.reshape(b * n, ROW_TILE, LANE), mod[:, 5], n)
                x = xr.reshape(b, n, d)
            else:
                raise NotImplementedError("an expert layer followed by another layer is not part of this model")
    return rmsnorm_rows(x.reshape(b * n, d), final_norm_g).reshape(b, n, d)
```

```python
import functools
import math

import jax
import jax.numpy as jnp
from jax import lax
from jax.experimental import pallas as pl
from jax.experimental.pallas import tpu as pltpu

F32 = jnp.float32
BF16 = jnp.bfloat16
HIGHEST = lax.Precision.HIGHEST

NORM_EPS = 1e-6
GRID_W = 64
A_GROUPS = 4
A_GROUP_DIM = 128
A_WIDTH = A_GROUPS * A_GROUP_DIM
A_CHUNK = 128
B_HEADS = 6
B_HEAD_DIM = 128
B_WIDTH = B_HEADS * B_HEAD_DIM
DN_CHUNK = 64
DN_CONV = 5
C_HEADS = 6
C_NOPE = 128
C_ROPE = 64
C_V = 128
C_WIDTH = C_HEADS * C_V
Q_LORA = 512
KV_LORA = 256
ROPE_BASE = 10000.0
ROPE_AXIS_FREQS = C_ROPE // 4
N_EXPERTS = 8
TOP_K = 2

LANE = 128
ROW_TILE = 16
VMEM_LIMIT = 56 * 1024 * 1024

ZC_QKV = 0
ZC_GATE = 3 * B_WIDTH
ZC_A = 4 * B_WIDTH
ZC_C = 4 * B_WIDTH + 2 * A_WIDTH
Z_COLS = ZC_C + 1024
C_HEAD_COLS = 3 * LANE


def _cparams(*sem):
    return pltpu.CompilerParams(dimension_semantics=sem, vmem_limit_bytes=VMEM_LIMIT)


def _tile(n, pref):
    if n <= pref:
        return n
    t = (pref // LANE) * LANE
    while n % t:
        t -= LANE
    return t


def _sigmoid(x):
    return 1.0 / (1.0 + jnp.exp(-x))


def _silu(x):
    return x * _sigmoid(x)


def _dot(a, b):
    return jnp.dot(a, b, preferred_element_type=F32)


def _dot_nt(a, b):
    return lax.dot_general(a, b, (((1,), (1,)), ((), ())), preferred_element_type=F32)


def _dot_tn(a, b):
    return lax.dot_general(a, b, (((0,), (0,)), ((), ())), preferred_element_type=F32)


def _split_bf16(a):
    hi = a.astype(BF16)
    lo = (a - hi.astype(F32)).astype(BF16)
    return hi, lo


def _dot_x3(a, b):
    ah, al = _split_bf16(a)
    bh, bl = _split_bf16(b)
    return _dot(ah, bh) + (_dot(ah, bl) + _dot(al, bh))


def _ada_kernel(c_ref, w_ref, b_ref, o_ref):
    c = c_ref[...]
    o_ref[0] = jnp.dot(_silu(c), w_ref[0], preferred_element_type=F32, precision=HIGHEST) + b_ref[0]


def ada_modulation(cc, ada_w, ada_b):
    depth, d, n = ada_w.shape
    tn = 1024
    return pl.pallas_call(
        _ada_kernel,
        grid=(depth, n // tn),
        in_specs=[
            pl.BlockSpec((8, d), lambda i, j: (0, 0)),
            pl.BlockSpec((1, d, tn), lambda i, j: (i, 0, j)),
            pl.BlockSpec((1, 1, tn), lambda i, j: (i, 0, j)),
        ],
        out_specs=pl.BlockSpec((1, 8, tn), lambda i, j: (i, 0, j)),
        out_shape=jax.ShapeDtypeStruct((depth, 8, n), F32),
        compiler_params=_cparams("parallel", "parallel"),
        name="ada_modulation",
    )(cc, ada_w, ada_b.reshape(depth, 1, n))


def _norm_mod_kernel(x_ref, g_ref, sh_ref, sc_ref, o_ref):
    x = x_ref[0]
    y = x * lax.rsqrt(jnp.mean(x * x, axis=-1, keepdims=True) + NORM_EPS) * g_ref[...]
    o_ref[0] = (y * (1.0 + sc_ref[0]) + sh_ref[0]).astype(o_ref.dtype)


def _norm_mod_router_kernel(x_ref, g_ref, sh_ref, sc_ref, r_ref, o_ref, lg_ref):
    x = x_ref[0]
    y = x * lax.rsqrt(jnp.mean(x * x, axis=-1, keepdims=True) + NORM_EPS) * g_ref[...]
    h = y * (1.0 + sc_ref[0]) + sh_ref[0]
    _store_token_rows(o_ref, h)
    lg_ref[0] = jnp.dot(h, r_ref[...], preferred_element_type=F32, precision=HIGHEST)


def _store_token_rows(rows_ref, x):
    n = x.shape[0]
    for c in range(ROW_TILE):
        rows_ref[pl.ds(c, n, stride=ROW_TILE), :] = x[:, c * LANE:(c + 1) * LANE]


def _load_token_rows(rows_ref, n, c):
    return rows_ref[pl.ds(c, n, stride=ROW_TILE), :]


def norm_mod(x, g, shift, scale, out_dtype=BF16, router=None):
    b, l, d = x.shape
    tl = min(512, l)
    in_specs = [
        pl.BlockSpec((1, tl, d), lambda i, j: (i, j, 0)),
        pl.BlockSpec((1, d), lambda i, j: (0, 0)),
        pl.BlockSpec((1, 1, d), lambda i, j: (i, 0, 0)),
        pl.BlockSpec((1, 1, d), lambda i, j: (i, 0, 0)),
    ]
    args = [x, g.reshape(1, d), shift.reshape(b, 1, d), scale.reshape(b, 1, d)]
    h_spec = pl.BlockSpec((1, tl, d), lambda i, j: (i, j, 0))
    h_shape = jax.ShapeDtypeStruct((b, l, d), out_dtype)
    if router is None:
        return pl.pallas_call(
            _norm_mod_kernel, grid=(b, l // tl), in_specs=in_specs, out_specs=h_spec, out_shape=h_shape,
            compiler_params=_cparams("parallel", "parallel"), name="norm_mod",
        )(*args)
    assert d == ROW_TILE * LANE
    in_specs.append(pl.BlockSpec((d, LANE), lambda i, j: (0, 0)))
    nl = l // tl
    return pl.pallas_call(
        _norm_mod_router_kernel, grid=(b, nl), in_specs=in_specs,
        out_specs=[pl.BlockSpec((tl * ROW_TILE, LANE), lambda i, j: (i * nl + j, 0)),
                   pl.BlockSpec((1, tl, LANE), lambda i, j: (i, j, 0))],
        out_shape=[jax.ShapeDtypeStruct((b * l * ROW_TILE, LANE), F32), jax.ShapeDtypeStruct((b, l, LANE), F32)],
        compiler_params=_cparams("parallel", "parallel"), name="norm_mod_router",
    )(*args, router)


def _rmsnorm_kernel(x_ref, g_ref, o_ref):
    x = x_ref[...]
    o_ref[...] = x * lax.rsqrt(jnp.mean(x * x, axis=-1, keepdims=True) + NORM_EPS) * g_ref[...]


def rmsnorm_rows(x, g):
    m, d = x.shape
    tm = min(512, m)
    return pl.pallas_call(
        _rmsnorm_kernel, grid=(m // tm,),
        in_specs=[pl.BlockSpec((tm, d), lambda i: (i, 0)), pl.BlockSpec((1, d), lambda i: (0, 0))],
        out_specs=pl.BlockSpec((tm, d), lambda i: (i, 0)),
        out_shape=jax.ShapeDtypeStruct((m, d), F32),
        compiler_params=_cparams("parallel"), name="final_rmsnorm",
    )(x, g.reshape(1, d))


def _mm_kernel(a_ref, w_ref, o_ref):
    o_ref[...] = _dot(a_ref[...], w_ref[...]).astype(o_ref.dtype)


def matmul(a, w, out_dtype, tm, tn):
    m, k = a.shape
    n = w.shape[1]
    tm, tn = _tile(m, tm), _tile(n, tn)
    return pl.pallas_call(
        _mm_kernel, grid=(m // tm, n // tn),
        in_specs=[pl.BlockSpec((tm, k), lambda i, j: (i, 0)), pl.BlockSpec((k, tn), lambda i, j: (0, j))],
        out_specs=pl.BlockSpec((tm, tn), lambda i, j: (i, j)),
        out_shape=jax.ShapeDtypeStruct((m, n), out_dtype),
        compiler_params=_cparams("parallel", "parallel"), name="matmul",
    )(a, w)


def _swiglu_up_kernel(a_ref, wg_ref, wu_ref, o_ref):
    a = a_ref[...]
    g = _dot(a, wg_ref[...])
    u = _dot(a, wu_ref[...])
    o_ref[...] = (_silu(g) * u).astype(o_ref.dtype)


def swiglu_up(a, wg, wu, tm, tn):
    m, k = a.shape
    n = wg.shape[1]
    tm, tn = _tile(m, tm), _tile(n, tn)
    return pl.pallas_call(
        _swiglu_up_kernel, grid=(m // tm, n // tn),
        in_specs=[pl.BlockSpec((tm, k), lambda i, j: (i, 0)),
                  pl.BlockSpec((k, tn), lambda i, j: (0, j)),
                  pl.BlockSpec((k, tn), lambda i, j: (0, j))],
        out_specs=pl.BlockSpec((tm, tn), lambda i, j: (i, j)),
        out_shape=jax.ShapeDtypeStruct((m, n), BF16),
        compiler_params=_cparams("parallel", "parallel"), name="swiglu_up",
    )(a, wg, wu)


def _down_res_kernel(a_ref, w_ref, x_ref, gate_ref, o_ref, acc_ref):
    k = pl.program_id(2)

    @pl.when(k == 0)
    def _():
        acc_ref[...] = _dot(a_ref[...], w_ref[...])

    @pl.when(k > 0)
    def _():
        acc_ref[...] += _dot(a_ref[...], w_ref[...])

    @pl.when(k == pl.num_programs(2) - 1)
    def _():
        o_ref[...] = x_ref[...] + gate_ref[0] * acc_ref[...]


def down_residual(a, w, x, gate, rows_per_batch, tm, tn, tk):
    m, kdim = a.shape
    n = w.shape[1]
    tm, tn, tk = _tile(rows_per_batch, tm), _tile(n, tn), _tile(kdim, tk)
    bsz = gate.shape[0]
    return pl.pallas_call(
        _down_res_kernel, grid=(m // tm, n // tn, kdim // tk),
        in_specs=[pl.BlockSpec((tm, tk), lambda i, j, k: (i, k)),
                  pl.BlockSpec((tk, tn), lambda i, j, k: (k, j)),
                  pl.BlockSpec((tm, tn), lambda i, j, k: (i, j)),
                  pl.BlockSpec((1, 1, tn), lambda i, j, k: ((i * tm) // rows_per_batch, 0, j))],
        out_specs=pl.BlockSpec((tm, tn), lambda i, j, k: (i, j)),
        out_shape=jax.ShapeDtypeStruct((m, n), F32),
        scratch_shapes=[pltpu.VMEM((tm, tn), F32)],
        compiler_params=_cparams("parallel", "parallel", "arbitrary"), name="down_residual",
    )(a, w, x, gate.reshape(bsz, 1, n))


def _mix_out_kernel(a_ref, b_ref, c_ref, w_ref, x_ref, gate_ref, o_ref):
    ka, kb = a_ref.shape[1], b_ref.shape[1]
    acc = _dot(a_ref[...], w_ref[0:ka, :])
    acc += _dot(b_ref[...], w_ref[ka:ka + kb, :])
    acc += _dot(c_ref[...], w_ref[ka + kb:, :])
    o_ref[...] = x_ref[...] + gate_ref[0] * acc


def mix_out_residual(oa, ob, oc, w, x, gate, rows_per_batch, tm, tn):
    m = oa.shape[0]
    kdim, n = w.shape
    tm, tn = _tile(rows_per_batch, tm), _tile(n, tn)
    bsz = gate.shape[0]
    return pl.pallas_call(
        _mix_out_kernel, grid=(m // tm, n // tn),
        in_specs=[pl.BlockSpec((tm, oa.shape[1]), lambda i, j: (i, 0)),
                  pl.BlockSpec((tm, ob.shape[1]), lambda i, j: (i, 0)),
                  pl.BlockSpec((tm, oc.shape[1]), lambda i, j: (i, 0)),
                  pl.BlockSpec((kdim, tn), lambda i, j: (0, j)),
                  pl.BlockSpec((tm, tn), lambda i, j: (i, j)),
                  pl.BlockSpec((1, 1, tn), lambda i, j: ((i * tm) // rows_per_batch, 0, j))],
        out_specs=pl.BlockSpec((tm, tn), lambda i, j: (i, j)),
        out_shape=jax.ShapeDtypeStruct((m, n), F32),
        compiler_params=_cparams("parallel", "parallel"), name="mix_out_residual",
    )(oa, ob, oc, w, x, gate.reshape(bsz, 1, n))


def _gelu_tanh(x):
    return 0.5 * x * (1.0 + jnp.tanh(math.sqrt(2.0 / math.pi) * (x + 0.044715 * (x * x * x))))


def _sgu_kernel(z_ref, g_ref, w_ref, b_ref, o_ref):
    tl = z_ref.shape[1]
    for c in range(tl // A_CHUNK):
        rows = slice(c * A_CHUNK, (c + 1) * A_CHUNK)
        for g in range(A_GROUPS):
            cols = slice(g * A_GROUP_DIM, (g + 1) * A_GROUP_DIM)
            u = _gelu_tanh(z_ref[0, rows, cols].astype(F32))
            v = _gelu_tanh(z_ref[0, rows, A_WIDTH + g * A_GROUP_DIM:A_WIDTH + (g + 1) * A_GROUP_DIM].astype(F32))
            vc = v - jnp.mean(v, axis=-1, keepdims=True)
            vn = vc * lax.rsqrt(jnp.mean(vc * vc, axis=-1, keepdims=True) + NORM_EPS) * g_ref[:, cols]
            mixed = _dot(w_ref[g], vn.astype(BF16)) + b_ref[g]
            o_ref[0, rows, cols] = (u * mixed).astype(o_ref.dtype)


def chunk_sgu(z, norm_g, w_s, b_s):
    b, l, _ = z.shape
    tl = min(512, l)
    blk = ZC_A // (2 * A_WIDTH)
    return pl.pallas_call(
        _sgu_kernel, grid=(b, l // tl),
        in_specs=[pl.BlockSpec((1, tl, 2 * A_WIDTH), lambda i, j: (i, j, blk)),
                  pl.BlockSpec((1, A_WIDTH), lambda i, j: (0, 0)),
                  pl.BlockSpec((A_GROUPS, A_CHUNK, A_CHUNK), lambda i, j: (0, 0, 0)),
                  pl.BlockSpec((A_GROUPS, A_CHUNK, 1), lambda i, j: (0, 0, 0))],
        out_specs=pl.BlockSpec((1, tl, A_WIDTH), lambda i, j: (i, j, 0)),
        out_shape=jax.ShapeDtypeStruct((b, l, A_WIDTH), BF16),
        compiler_params=_cparams("parallel", "parallel"), name="chunk_sgu",
    )(z, norm_g.reshape(1, A_WIDTH), w_s.astype(BF16), b_s.reshape(A_GROUPS, A_CHUNK, 1))


def _mla_prep_kernel(z_ref, qg_ref, kvg_ref, wq_ref, wkv_ref, cos_ref, sin_ref, q_ref, k_ref, v_ref, *, q_scale):
    lat = z_ref[0, :, 0:Q_LORA].astype(F32)
    cq = (lat * lax.rsqrt(jnp.mean(lat * lat, axis=-1, keepdims=True) + NORM_EPS) * qg_ref[...]).astype(BF16)
    kvl = z_ref[0, :, Q_LORA:Q_LORA + KV_LORA].astype(F32)
    ckv = (kvl * lax.rsqrt(jnp.mean(kvl * kvl, axis=-1, keepdims=True) + NORM_EPS) * kvg_ref[...]).astype(BF16)
    cos = cos_ref[...]
    sin = sin_ref[...]
    base = Q_LORA + KV_LORA
    k_pe = z_ref[0, :, base:base + LANE].astype(F32) * cos + z_ref[0, :, base + LANE:base + 2 * LANE].astype(F32) * sin
    k_pe = k_pe.astype(k_ref.dtype)
    for h in range(C_HEADS):
        qh = _dot(cq, wq_ref[:, h * C_HEAD_COLS:(h + 1) * C_HEAD_COLS])
        q_pe = qh[:, LANE:2 * LANE] * cos + qh[:, 2 * LANE:3 * LANE] * sin
        q_ref[0, h, :, 0:LANE] = (qh[:, 0:LANE] * q_scale).astype(q_ref.dtype)
        q_ref[0, h, :, LANE:2 * LANE] = (q_pe * q_scale).astype(q_ref.dtype)
        kv = _dot(ckv, wkv_ref[:, h * 2 * LANE:(h + 1) * 2 * LANE])
        k_ref[0, h, :, 0:LANE] = kv[:, 0:LANE].astype(k_ref.dtype)
        k_ref[0, h, :, LANE:2 * LANE] = k_pe
        v_ref[0, h] = kv[:, LANE:2 * LANE].astype(v_ref.dtype)


def mla_prep(z, q_norm_g, kv_norm_g, wq_arr, wkv_arr, cos2, sin2):
    b, l, _ = z.shape
    tl = min(256, l)
    blk = ZC_C // 1024
    q_scale = (C_NOPE + C_ROPE) ** -0.5 * math.log2(math.e)
    return pl.pallas_call(
        functools.partial(_mla_prep_kernel, q_scale=q_scale), grid=(b, l // tl),
        in_specs=[pl.BlockSpec((1, tl, 1024), lambda i, j: (i, j, blk)),
                  pl.BlockSpec((1, Q_LORA), lambda i, j: (0, 0)),
                  pl.BlockSpec((1, KV_LORA), lambda i, j: (0, 0)),
                  pl.BlockSpec((Q_LORA, C_HEADS * C_HEAD_COLS), lambda i, j: (0, 0)),
                  pl.BlockSpec((KV_LORA, C_HEADS * 2 * LANE), lambda i, j: (0, 0)),
                  pl.BlockSpec((tl, LANE), lambda i, j: (j, 0)),
                  pl.BlockSpec((tl, LANE), lambda i, j: (j, 0))],
        out_specs=[pl.BlockSpec((1, C_HEADS, tl, 2 * LANE), lambda i, j: (i, 0, j, 0)),
                   pl.BlockSpec((1, C_HEADS, tl, 2 * LANE), lambda i, j: (i, 0, j, 0)),
                   pl.BlockSpec((1, C_HEADS, tl, LANE), lambda i, j: (i, 0, j, 0))],
        out_shape=[jax.ShapeDtypeStruct((b, C_HEADS, l, 2 * LANE), BF16),
                   jax.ShapeDtypeStruct((b, C_HEADS, l, 2 * LANE), BF16),
                   jax.ShapeDtypeStruct((b, C_HEADS, l, LANE), BF16)],
        compiler_params=_cparams("parallel", "parallel"), name="mla_prep",
    )(z, q_norm_g.reshape(1, Q_LORA), kv_norm_g.reshape(1, KV_LORA), wq_arr, wkv_arr, cos2, sin2)


FLASH_ROWS = 32


def _flash_kernel(q_ref, k_ref, v_ref, o_ref, s_ref, p_ref, m_ref, l_ref, acc_ref, *, tk, nk):
    tq = q_ref.shape[2]
    rb = min(FLASH_ROWS, tq)
    m_ref[...] = jnp.full(m_ref.shape, -jnp.inf, F32)
    l_ref[...] = jnp.zeros(l_ref.shape, F32)
    acc_ref[...] = jnp.zeros(acc_ref.shape, F32)

    for j in range(nk):
        s_buf, p_buf = s_ref.at[j % 2], p_ref.at[j % 2]
        keys = slice(j * tk, (j + 1) * tk)
        s_buf[...] = _dot_nt(q_ref[0, 0], k_ref[0, 0, keys, :])
        for r in range(tq // rb):
            rows = slice(r * rb, (r + 1) * rb)
            blocks = [s_buf[rows, t * LANE:(t + 1) * LANE] for t in range(tk // LANE)]
            mx = blocks[0]
            for blk in blocks[1:]:
                mx = jnp.maximum(mx, blk)
            m_old = m_ref[rows, :]
            m_new = jnp.maximum(m_old, jnp.broadcast_to(jnp.max(mx, axis=-1, keepdims=True), (rb, LANE)))
            alpha = jnp.exp2(m_old - m_new)
            lane_sum = None
            for t, blk in enumerate(blocks):
                p = jnp.exp2(blk - m_new)
                lane_sum = p if lane_sum is None else lane_sum + p
                p_buf[rows, t * LANE:(t + 1) * LANE] = p.astype(BF16)
            l_ref[rows, :] = alpha * l_ref[rows, :] + lane_sum
            m_ref[rows, :] = m_new
            acc_ref[rows, :] = alpha * acc_ref[rows, :]
        acc_ref[...] += _dot(p_buf[...], v_ref[0, 0, keys, :])
    o_ref[0] = (acc_ref[...] / jnp.sum(l_ref[...], axis=-1, keepdims=True)).astype(o_ref.dtype)


def flash_attention(q, k, v):
    b, h, lq, dq = q.shape
    lk = k.shape[2]
    tq = min(512, lq)
    tk = _tile(lk, 768)
    return pl.pallas_call(
        functools.partial(_flash_kernel, tk=tk, nk=lk // tk), grid=(b, h, lq // tq),
        in_specs=[pl.BlockSpec((1, 1, tq, dq), lambda i, j, t: (i, j, t, 0)),
                  pl.BlockSpec((1, 1, lk, dq), lambda i, j, t: (i, j, 0, 0)),
                  pl.BlockSpec((1, 1, lk, C_V), lambda i, j, t: (i, j, 0, 0))],
        out_specs=pl.BlockSpec((1, tq, C_V), lambda i, j, t: (i, t, j)),
        out_shape=jax.ShapeDtypeStruct((b, lq, h * C_V), BF16),
        scratch_shapes=[pltpu.VMEM((2, tq, tk), F32), pltpu.VMEM((2, tq, tk), BF16), pltpu.VMEM((tq, LANE), F32),
                        pltpu.VMEM((tq, LANE), F32), pltpu.VMEM((tq, C_V), F32)],
        compiler_params=_cparams("parallel", "parallel", "parallel"), name="flash_attention",
    )(q, k, v)


def _softplus(x):
    return jnp.maximum(x, 0.0) + jnp.log1p(jnp.exp(-jnp.abs(x)))


def _dn_prep_kernel(zm_ref, zp_ref, zn_ref, zg_ref, w_ref, alog_ref, dtb_ref, qkv_ref, gb_ref):
    j = pl.program_id(1)
    tl = zm_ref.shape[1]
    half = DN_CONV // 2
    keep_prev = (j > 0).astype(F32)
    keep_next = (j < pl.num_programs(1) - 1).astype(F32)
    for c in range(3 * B_HEADS):
        cols = slice(c * LANE, (c + 1) * LANE)
        prev = zp_ref[0, :, cols].astype(F32)[8:16] * keep_prev
        nxt = zn_ref[0, :, cols].astype(F32)[0:8] * keep_next
        ext = jnp.concatenate([prev, zm_ref[0, :, cols].astype(F32), nxt], axis=0)
        y = ext[8 - half:8 - half + tl] * w_ref[0:1, cols]
        for i in range(1, DN_CONV):
            y = y + ext[8 - half + i:8 - half + i + tl] * w_ref[i:i + 1, cols]
        y = _silu(y)
        if c < 2 * B_HEADS:
            y = y * lax.rsqrt(jnp.sum(y * y, axis=-1, keepdims=True) + NORM_EPS)
        qkv_ref[0, :, cols] = y.astype(qkv_ref.dtype)
    zg = zg_ref[0]
    lane = lax.broadcasted_iota(jnp.int32, zg.shape, 1)
    g = -jnp.exp(alog_ref[...]) * _softplus(zg + dtb_ref[...])
    gb_ref[0] = jnp.where(lane < 2 * B_HEADS, g, _sigmoid(zg))


def deltanet_prep(z, zg, conv_w, a_log, dt_bias):
    b, l, _ = z.shape
    tl = min(256, l)
    wq = 3 * B_WIDTH
    nb16 = l // 16
    pad = LANE - 2 * B_HEADS
    alog = jnp.pad(a_log.reshape(1, -1), ((0, 0), (0, pad)))
    dtb = jnp.pad(dt_bias.reshape(1, -1), ((0, 0), (0, pad)))
    return pl.pallas_call(
        _dn_prep_kernel, grid=(b, l // tl),
        in_specs=[pl.BlockSpec((1, tl, wq), lambda i, j: (i, j, 0)),
                  pl.BlockSpec((1, 16, wq), lambda i, j: (i, jnp.maximum(j * (tl // 16) - 1, 0), 0)),
                  pl.BlockSpec((1, 16, wq), lambda i, j: (i, jnp.minimum((j + 1) * (tl // 16), nb16 - 1), 0)),
                  pl.BlockSpec((1, tl, LANE), lambda i, j: (i, j, 0)),
                  pl.BlockSpec((DN_CONV, wq), lambda i, j: (0, 0)),
                  pl.BlockSpec((1, LANE), lambda i, j: (0, 0)),
                  pl.BlockSpec((1, LANE), lambda i, j: (0, 0))],
        out_specs=[pl.BlockSpec((1, tl, wq), lambda i, j: (i, j, 0)),
                   pl.BlockSpec((1, tl, LANE), lambda i, j: (i, j, 0))],
        out_shape=[jax.ShapeDtypeStruct((b, l, wq), BF16), jax.ShapeDtypeStruct((b, l, LANE), F32)],
        compiler_params=_cparams("parallel", "parallel"), name="deltanet_prep",
    )(z, z, z, zg, conv_w, alog, dtb)


def _dot_bf16(a, b):
    return _dot(a.astype(BF16), b.astype(BF16))


def _dn_intra_kernel(qkv_ref, gb_ref, u_ref, w_ref, qg_ref, kg_ref, qk_ref, gl_ref, *, chunks, passes):
    c = DN_CHUNK
    row = lax.broadcasted_iota(jnp.int32, (c, LANE), 0)
    lane = lax.broadcasted_iota(jnp.int32, (c, LANE), 1)
    col = jnp.where(lane < c, lane, lane - c)
    fwd = lane < c
    bwd = jnp.logical_not(fwd)
    incl = (fwd & (row >= col)) | (bwd & (row <= col))
    strict = (fwd & (row > col)) | (bwd & (row < col))
    same16 = lax.shift_right_logical(row, 4) == lax.shift_right_logical(col, 4)
    same32 = lax.shift_right_logical(row, 5) == lax.shift_right_logical(col, 5)
    diag = row == col
    eye2 = diag.astype(F32)
    r64 = lax.broadcasted_iota(jnp.int32, (c, c), 0)
    c64 = lax.broadcasted_iota(jnp.int32, (c, c), 1)
    tri_lo = (r64 >= c64).astype(F32)
    tri_up = (r64 <= c64).astype(F32)
    scale = B_HEAD_DIM ** -0.5
    dotp = _dot_x3 if passes == 3 else _dot_bf16

    def pick(x, idx):
        return jnp.sum(jnp.where(lane == idx, x, 0.0), axis=-1, keepdims=True)

    def blockdiag(y2):
        return jnp.concatenate([jnp.where(fwd, y2, 0.0), jnp.where(fwd, 0.0, y2)], axis=0)

    def mm(xs, ys):
        return [dotp(x2, blockdiag(y2)) for x2, y2 in zip(xs, ys)]

    rows = [slice(ci * c, (ci + 1) * c) for ci in range(chunks)]
    gb = [gb_ref[0, rs, :] for rs in rows]
    cum_f = [jnp.dot(tri_lo, g, preferred_element_type=F32, precision=HIGHEST) for g in gb]
    cum_b = [jnp.dot(tri_up, g, preferred_element_type=F32, precision=HIGHEST) for g in gb]
    tot_f = [x[c - 1:c, :] for x in cum_f]
    tot_b = [x[0:1, :] for x in cum_b]

    units = [(ci, h) for ci in range(chunks) for h in range(B_HEADS)]
    q = [qkv_ref[0, rows[ci], h * LANE:(h + 1) * LANE] for ci, h in units]
    k = [qkv_ref[0, rows[ci], B_WIDTH + h * LANE:B_WIDTH + (h + 1) * LANE] for ci, h in units]
    v = [qkv_ref[0, rows[ci], 2 * B_WIDTH + h * LANE:2 * B_WIDTH + (h + 1) * LANE].astype(F32) for ci, h in units]
    k2 = [jnp.concatenate([x, x], axis=0) for x in k]
    kk2 = [_dot_nt(x, y) for x, y in zip(k, k2)]
    qk2 = [_dot_nt(x, y) for x, y in zip(q, k2)]
    cf = [pick(cum_f[ci], h) for ci, h in units]
    cb = [pick(cum_b[ci], B_HEADS + h) for ci, h in units]
    bf = [pick(gb[ci], 2 * B_HEADS + h) for ci, h in units]
    bb = [pick(gb[ci], 3 * B_HEADS + h) for ci, h in units]
    lf = [pick(jnp.broadcast_to(tot_f[ci], (c, LANE)), h) for ci, h in units]
    lb = [pick(jnp.broadcast_to(tot_b[ci], (c, LANE)), B_HEADS + h) for ci, h in units]
    c2 = [jnp.where(fwd, x, y) for x, y in zip(cf, cb)]
    r2 = [jnp.sum(jnp.where(diag, x, 0.0), axis=0, keepdims=True) for x in c2]
    decay2 = [jnp.where(incl, jnp.exp(jnp.where(incl, x - y, 0.0)), 0.0) for x, y in zip(c2, r2)]
    l2 = [jnp.where(strict, jnp.where(fwd, x, y) * kk * dc, 0.0) for x, y, kk, dc in zip(bf, bb, kk2, decay2)]
    mp = [jnp.where(same16, -x, 0.0) for x in l2]
    p = [eye2 + x for x in mp]
    for _ in range(3):
        mp = mm(mp, mp)
        p = [x + y for x, y in zip(p, mm(p, mp))]
    off = [jnp.where(same32 & jnp.logical_not(same16), x, 0.0) for x in l2]
    p = [x - y for x, y in zip(p, mm(mm(p, off), p))]
    off = [jnp.where(same32, 0.0, x) for x in l2]
    p = [x - y for x, y in zip(p, mm(mm(p, off), p))]
    ef = [jnp.exp(x) for x in cf]
    eb = [jnp.exp(x) for x in cb]
    zero = jnp.zeros((c, LANE), F32)
    sol = []
    for i in range(len(units)):
        kf = k[i].astype(F32)
        rhs = jnp.concatenate([
            jnp.concatenate([v[i] * bf[i], kf * (bf[i] * ef[i]), zero, zero], axis=1),
            jnp.concatenate([zero, zero, v[i] * bb[i], kf * (bb[i] * eb[i])], axis=1)], axis=0)
        sol.append(dotp(p[i], rhs))
    for i, (ci, h) in enumerate(units):
        hc = slice(h * LANE, (h + 1) * LANE)
        rs = rows[ci]
        kf = k[i].astype(F32)
        qf = q[i].astype(F32)
        u_ref[0, 0, rs, hc] = sol[i][:, 0:LANE]
        w_ref[0, 0, rs, hc] = sol[i][:, LANE:2 * LANE].astype(w_ref.dtype)
        u_ref[1, 0, rs, hc] = sol[i][:, 2 * LANE:3 * LANE]
        w_ref[1, 0, rs, hc] = sol[i][:, 3 * LANE:4 * LANE].astype(w_ref.dtype)
        qg_ref[0, 0, rs, hc] = (qf * (ef[i] * scale)).astype(qg_ref.dtype)
        qg_ref[1, 0, rs, hc] = (qf * (eb[i] * scale)).astype(qg_ref.dtype)
        kg_ref[0, 0, rs, hc] = (kf * jnp.exp(lf[i] - cf[i])).astype(kg_ref.dtype)
        kg_ref[1, 0, rs, hc] = (kf * jnp.exp(lb[i] - cb[i])).astype(kg_ref.dtype)
        qk_ref[0, rs, hc] = (qk2[i] * decay2[i] * scale).astype(qk_ref.dtype)
    r8 = lax.broadcasted_iota(jnp.int32, (2 * 8, LANE), 0)
    l8 = lax.broadcasted_iota(jnp.int32, (2 * 8, LANE), 1)
    want = jnp.where(r8 < 8, r8, r8 - 8 + B_HEADS)
    for ci in range(chunks):
        src = jnp.concatenate([jnp.broadcast_to(tot_f[ci], (8, LANE)), jnp.broadcast_to(tot_b[ci], (8, LANE))], axis=0)
        tot = jnp.sum(jnp.where(l8 == want, src, 0.0), axis=-1, keepdims=True)
        gl_ref[0, ci] = jnp.broadcast_to(jnp.exp(tot), (2 * 8, LANE))


DN_INTRA_CHUNKS = 2
DN_INTRA_PASSES = 1
DN_SCAN_CHUNKS = 4


def deltanet_intra(qkv, gb):
    b, l, _ = qkv.shape
    nc = l // DN_CHUNK
    chunks = math.gcd(DN_INTRA_CHUNKS, nc)
    tl = chunks * DN_CHUNK
    dir_spec = pl.BlockSpec((2, 1, tl, B_WIDTH), lambda i, j: (0, i, j, 0))
    return pl.pallas_call(
        functools.partial(_dn_intra_kernel, chunks=chunks, passes=DN_INTRA_PASSES), grid=(b, nc // chunks),
        in_specs=[pl.BlockSpec((1, tl, 3 * B_WIDTH), lambda i, j: (i, j, 0)),
                  pl.BlockSpec((1, tl, LANE), lambda i, j: (i, j, 0))],
        out_specs=[dir_spec, dir_spec, dir_spec, dir_spec,
                   pl.BlockSpec((1, tl, B_WIDTH), lambda i, j: (i, j, 0)),
                   pl.BlockSpec((1, chunks, 16, LANE), lambda i, j: (i, j, 0, 0))],
        out_shape=[jax.ShapeDtypeStruct((2, b, l, B_WIDTH), F32),
                   jax.ShapeDtypeStruct((2, b, l, B_WIDTH), BF16),
                   jax.ShapeDtypeStruct((2, b, l, B_WIDTH), BF16),
                   jax.ShapeDtypeStruct((2, b, l, B_WIDTH), BF16),
                   jax.ShapeDtypeStruct((b, l, B_WIDTH), BF16),
                   jax.ShapeDtypeStruct((b, nc, 16, LANE), F32)],
        compiler_params=_cparams("parallel", "parallel"), name="deltanet_intra",
    )(qkv, gb)


def _dn_scan_kernel(uf_ref, wf_ref, qgf_ref, kgf_ref, qkf_ref, glf_ref,
                    ub_ref, wb_ref, qgb_ref, kgb_ref, qkb_ref, glb_ref, s0_ref,
                    of_ref, ob_ref, sfin_ref, s_ref, *, chunks):
    j = pl.program_id(1)
    c = DN_CHUNK

    @pl.when(j == 0)
    def _():
        s_ref[...] = s0_ref[0]

    lane = lax.broadcasted_iota(jnp.int32, (c, LANE), 1)
    zeros_b = jnp.zeros((c, LANE), BF16)
    dirs = ((uf_ref, wf_ref, qgf_ref, kgf_ref, qkf_ref, glf_ref, of_ref),
            (ub_ref, wb_ref, qgb_ref, kgb_ref, qkb_ref, glb_ref, ob_ref))
    units = [(d, h) for d in range(2) for h in range(B_HEADS)]
    cols = [slice(h * LANE, (h + 1) * LANE) for _, h in units]
    state = [s_ref[d * B_HEADS + h] for d, h in units]
    for step in range(chunks):
        chunk_of = (step, chunks - 1 - step)
        rows = [slice(chunk_of[d] * c, (chunk_of[d] + 1) * c) for d, _ in units]
        r = [_dot(jnp.concatenate([dirs[d][1][0, 0, rs, hc], dirs[d][2][0, 0, rs, hc]], axis=0), s.astype(BF16))
             for (d, _), hc, rs, s in zip(units, cols, rows, state)]
        vb = [(dirs[d][0][0, 0, rs, hc] - ri[0:c]).astype(BF16) for (d, _), hc, rs, ri in zip(units, cols, rows, r)]
        new_state = []
        for (d, h), hc, rs, s, ri, vi in zip(units, cols, rows, state, r, vb):
            qk2 = dirs[d][4][0, rs, hc]
            if d == 0:
                intra = _dot(jnp.where(lane < c, qk2, jnp.zeros_like(qk2)), jnp.concatenate([vi, zeros_b], axis=0))
            else:
                intra = _dot(jnp.where(lane < c, jnp.zeros_like(qk2), qk2), jnp.concatenate([zeros_b, vi], axis=0))
            dirs[d][6][0, rs, hc] = ri[c:2 * c] + intra
            gl = dirs[d][5][0, chunk_of[d], d * 8 + h:d * 8 + h + 1, :]
            new_state.append(s * gl + _dot_tn(dirs[d][3][0, 0, rs, hc], vi))
        state = new_state
    for (d, h), s in zip(units, state):
        s_ref[d * B_HEADS + h] = s

    @pl.when(j == pl.num_programs(1) - 1)
    def _():
        sfin_ref[0] = s_ref[...]


def deltanet_scan(u, w, qg, kg, qk, gl, s0):
    _, b, l, _ = u.shape
    chunks = math.gcd(DN_SCAN_CHUNKS, l // DN_CHUNK)
    tl = chunks * DN_CHUNK
    nb = l // tl
    fdir = pl.BlockSpec((1, 1, tl, B_WIDTH), lambda i, j: (0, i, j, 0))
    bdir = pl.BlockSpec((1, 1, tl, B_WIDTH), lambda i, j: (1, i, nb - 1 - j, 0))
    fqk = pl.BlockSpec((1, tl, B_WIDTH), lambda i, j: (i, j, 0))
    bqk = pl.BlockSpec((1, tl, B_WIDTH), lambda i, j: (i, nb - 1 - j, 0))
    fgl = pl.BlockSpec((1, chunks, 16, LANE), lambda i, j: (i, j, 0, 0))
    bgl = pl.BlockSpec((1, chunks, 16, LANE), lambda i, j: (i, nb - 1 - j, 0, 0))
    st = pl.BlockSpec((1, 2 * B_HEADS, B_HEAD_DIM, B_HEAD_DIM), lambda i, j: (i, 0, 0, 0))
    return pl.pallas_call(
        functools.partial(_dn_scan_kernel, chunks=chunks), grid=(b, nb),
        in_specs=[fdir, fdir, fdir, fdir, fqk, fgl, bdir, bdir, bdir, bdir, bqk, bgl, st],
        out_specs=[fqk, bqk, st],
        out_shape=[jax.ShapeDtypeStruct((b, l, B_WIDTH), F32),
                   jax.ShapeDtypeStruct((b, l, B_WIDTH), F32),
                   jax.ShapeDtypeStruct((b, 2 * B_HEADS, B_HEAD_DIM, B_HEAD_DIM), F32)],
        scratch_shapes=[pltpu.VMEM((2 * B_HEADS, B_HEAD_DIM, B_HEAD_DIM), F32)],
        compiler_params=_cparams("parallel", "arbitrary"), name="deltanet_scan",
    )(u, w, qg, kg, qk, gl, u, w, qg, kg, qk, gl, s0)


def _dn_out_kernel(of_ref, ob_ref, gate_ref, g_ref, o_ref):
    for h in range(B_HEADS):
        hc = slice(h * LANE, (h + 1) * LANE)
        o = of_ref[0, :, hc] + ob_ref[0, :, hc]
        y = o * lax.rsqrt(jnp.mean(o * o, axis=-1, keepdims=True) + NORM_EPS) * g_ref[...]
        o_ref[0, :, hc] = (y * _silu(gate_ref[0, :, hc].astype(F32))).astype(o_ref.dtype)


def deltanet_out(o_f, o_b, z, norm_g):
    b, l, _ = o_f.shape
    tl = min(512, l)
    blk = ZC_GATE // B_WIDTH
    return pl.pallas_call(
        _dn_out_kernel, grid=(b, l // tl),
        in_specs=[pl.BlockSpec((1, tl, B_WIDTH), lambda i, j: (i, j, 0)),
                  pl.BlockSpec((1, tl, B_WIDTH), lambda i, j: (i, j, 0)),
                  pl.BlockSpec((1, tl, B_WIDTH), lambda i, j: (i, j, blk)),
                  pl.BlockSpec((1, B_HEAD_DIM), lambda i, j: (0, 0))],
        out_specs=pl.BlockSpec((1, tl, B_WIDTH), lambda i, j: (i, j, 0)),
        out_shape=jax.ShapeDtypeStruct((b, l, B_WIDTH), BF16),
        compiler_params=_cparams("parallel", "parallel"), name="deltanet_out",
    )(o_f, o_b, z, norm_g.reshape(1, B_HEAD_DIM))


def _route_kernel(lg_ref, info_ref, cnt_ref, carry_ref):
    i = pl.program_id(0)

    @pl.when(i == 0)
    def _():
        carry_ref[...] = jnp.zeros_like(carry_ref)

    lg = lg_ref[...]
    tl = lg.shape[0]
    lane = lax.broadcasted_iota(jnp.int32, lg.shape, 1)
    valid = lane < N_EXPERTS
    lg = jnp.where(valid, lg, -jnp.inf)
    e = jnp.exp(lg - jnp.max(lg, axis=-1, keepdims=True))
    p = e / jnp.sum(e, axis=-1, keepdims=True)
    p = jnp.where(valid, p, -1.0)
    p1 = jnp.max(p, axis=-1, keepdims=True)
    i1 = jnp.min(jnp.where(p == p1, lane, LANE), axis=-1, keepdims=True)
    pm = jnp.where(lane == i1, -1.0, p)
    p2 = jnp.max(pm, axis=-1, keepdims=True)
    i2 = jnp.min(jnp.where(pm == p2, lane, LANE), axis=-1, keepdims=True)
    tot = p1 + p2
    w1, w2 = p1 / tot, p2 / tot
    hit1, hit2 = lane == i1, lane == i2
    onehot = (hit1 | hit2).astype(F32)
    r = lax.broadcasted_iota(jnp.int32, (tl, tl), 0)
    c = lax.broadcasted_iota(jnp.int32, (tl, tl), 1)
    before = _dot((r > c).astype(BF16), onehot.astype(BF16)) + carry_ref[...]
    r1 = jnp.sum(jnp.where(hit1, before, 0.0), axis=-1, keepdims=True)
    r2 = jnp.sum(jnp.where(hit2, before, 0.0), axis=-1, keepdims=True)
    carry_ref[...] += jnp.sum(onehot, axis=0, keepdims=True)
    cnt_ref[...] = jnp.broadcast_to(carry_ref[...], cnt_ref.shape)
    info = jnp.where(lane == 0, i1.astype(F32), 0.0)
    info = jnp.where(lane == 1, i2.astype(F32), info)
    info = jnp.where(lane == 2, r1, info)
    info = jnp.where(lane == 3, r2, info)
    info = jnp.where(lane == 4, w1, info)
    info = jnp.where(lane == 5, w2, info)
    info_ref[...] = info


def moe_route(logits):
    t = logits.shape[0]
    tl = min(512, t)
    return pl.pallas_call(
        _route_kernel, grid=(t // tl,),
        in_specs=[pl.BlockSpec((tl, LANE), lambda i: (i, 0))],
        out_specs=[pl.BlockSpec((tl, LANE), lambda i: (i, 0)), pl.BlockSpec((8, LANE), lambda i: (0, 0))],
        out_shape=[jax.ShapeDtypeStruct((t, LANE), F32), jax.ShapeDtypeStruct((8, LANE), F32)],
        scratch_shapes=[pltpu.VMEM((1, LANE), F32)],
        compiler_params=_cparams("arbitrary"), name="moe_route",
    )(logits)


def _dispatch_kernel(pos_ref, h_ref, xs_in_ref, xs_ref, sem):
    del xs_in_ref
    tb = pos_ref.shape[2] // TOP_K

    def copy(src_row, dst_row):
        return pltpu.make_async_copy(h_ref.at[pl.ds(pl.multiple_of(src_row * ROW_TILE, ROW_TILE), ROW_TILE)],
                                     xs_ref.at[pl.ds(pl.multiple_of(dst_row * ROW_TILE, ROW_TILE), ROW_TILE)], sem)

    def issue(j, carry):
        for k in range(TOP_K):
            copy(j, pos_ref[0, 0, TOP_K * j + k]).start()
        return carry

    lax.fori_loop(0, tb, issue, 0)

    def drain(j, carry):
        copy(0, 0).wait()
        return carry

    lax.fori_loop(0, TOP_K * tb, drain, 0)


def moe_dispatch(h_rows, pos, n_rows):
    t = pos.shape[0]
    tb = min(256, t)
    d_rows = h_rows.shape[0] // t
    assert d_rows == ROW_TILE
    zeros = jnp.zeros((n_rows * ROW_TILE, LANE), h_rows.dtype)
    return pl.pallas_call(
        _dispatch_kernel, grid=(t // tb,),
        in_specs=[pl.BlockSpec((1, 1, TOP_K * tb), lambda i: (i, 0, 0), memory_space=pltpu.SMEM),
                  pl.BlockSpec((tb * ROW_TILE, LANE), lambda i: (i, 0)),
                  pl.BlockSpec(memory_space=pl.ANY)],
        out_specs=pl.BlockSpec(memory_space=pl.ANY),
        out_shape=jax.ShapeDtypeStruct(zeros.shape, zeros.dtype),
        scratch_shapes=[pltpu.SemaphoreType.DMA],
        input_output_aliases={2: 0},
        compiler_params=_cparams("arbitrary"), name="moe_dispatch",
    )(pos.reshape(t // tb, 1, TOP_K * tb), h_rows, zeros)


def _moe_up_kernel(te_ref, tv_ref, xs_ref, wg_ref, wu_ref, o_ref, a_ref):
    del te_ref
    i = pl.program_id(0)
    tm = a_ref.shape[0]

    @pl.when(pl.program_id(1) == 0)
    def _():
        for c in range(ROW_TILE):
            a_ref[:, c * LANE:(c + 1) * LANE] = _load_token_rows(xs_ref, tm, c).astype(a_ref.dtype)

    @pl.when(tv_ref[i] != 0)
    def _():
        a = a_ref[...]
        g = _dot(a, wg_ref[0].astype(BF16))
        u = _dot(a, wu_ref[0].astype(BF16))
        o_ref[...] = (_silu(g) * u).astype(o_ref.dtype)

    @pl.when(tv_ref[i] == 0)
    def _():
        o_ref[...] = jnp.zeros_like(o_ref)


def moe_up(xs_rows, wg, wu, tile_expert, tile_valid, tm, tn):
    r = xs_rows.shape[0] // ROW_TILE
    d, n = wg.shape[1], wg.shape[2]
    tn = _tile(n, tn)
    grid_spec = pltpu.PrefetchScalarGridSpec(
        num_scalar_prefetch=2, grid=(r // tm, n // tn),
        in_specs=[pl.BlockSpec((tm * ROW_TILE, LANE), lambda i, j, te, tv: (i, 0)),
                  pl.BlockSpec((1, d, tn), lambda i, j, te, tv: (te[i], 0, j)),
                  pl.BlockSpec((1, d, tn), lambda i, j, te, tv: (te[i], 0, j))],
        out_specs=pl.BlockSpec((tm, tn), lambda i, j, te, tv: (i, j)),
        scratch_shapes=[pltpu.VMEM((tm, d), BF16)])
    return pl.pallas_call(
        _moe_up_kernel, grid_spec=grid_spec,
        out_shape=jax.ShapeDtypeStruct((r, n), BF16),
        compiler_params=_cparams("parallel", "arbitrary"), name="moe_up",
    )(tile_expert, tile_valid, xs_rows, wg, wu)


def _moe_down_kernel(te_ref, tv_ref, a_ref, w_ref, o_ref, acc_ref):
    del te_ref
    i = pl.program_id(0)
    k = pl.program_id(1)

    valid = tv_ref[i] != 0

    @pl.when(jnp.logical_and(valid, k == 0))
    def _():
        acc_ref[...] = _dot(a_ref[...], w_ref[0].astype(BF16))

    @pl.when(jnp.logical_and(valid, k > 0))
    def _():
        acc_ref[...] += _dot(a_ref[...], w_ref[0].astype(BF16))

    @pl.when(jnp.logical_and(jnp.logical_not(valid), k == 0))
    def _():
        acc_ref[...] = jnp.zeros_like(acc_ref)

    @pl.when(k == pl.num_programs(1) - 1)
    def _():
        _store_token_rows(o_ref, acc_ref[...])


def moe_down(hid, wd, tile_expert, tile_valid, tm, tk):
    r, kdim = hid.shape
    n = wd.shape[2]
    assert n == ROW_TILE * LANE
    tk = _tile(kdim, tk)
    grid_spec = pltpu.PrefetchScalarGridSpec(
        num_scalar_prefetch=2, grid=(r // tm, kdim // tk),
        in_specs=[pl.BlockSpec((tm, tk), lambda i, k, te, tv: (i, k)),
                  pl.BlockSpec((1, tk, n), lambda i, k, te, tv: (te[i], k, 0))],
        out_specs=pl.BlockSpec((tm * ROW_TILE, LANE), lambda i, k, te, tv: (i, 0)),
        scratch_shapes=[pltpu.VMEM((tm, n), F32)])
    return pl.pallas_call(
        _moe_down_kernel, grid_spec=grid_spec,
        out_shape=jax.ShapeDtypeStruct((r * ROW_TILE, LANE), F32),
        compiler_params=_cparams("parallel", "arbitrary"), name="moe_down",
    )(tile_expert, tile_valid, hid, wd)


def _combine_kernel(pos_ref, ys_ref, x_ref, gate_ref, wt_ref, ng_ref, o_ref, buf_ref, sem, *, final_norm):
    tb = x_ref.shape[0]

    def copy(k, j, src_row):
        return pltpu.make_async_copy(ys_ref.at[pl.ds(pl.multiple_of(src_row * ROW_TILE, ROW_TILE), ROW_TILE)],
                                     buf_ref.at[k, pl.ds(pl.multiple_of(j * ROW_TILE, ROW_TILE), ROW_TILE)], sem)

    def issue(j, carry):
        for k in range(TOP_K):
            copy(k, j, pos_ref[0, 0, TOP_K * j + k]).start()
        return carry

    lax.fori_loop(0, tb, issue, 0)

    def drain(j, carry):
        copy(0, 0, 0).wait()
        return carry

    lax.fori_loop(0, TOP_K * tb, drain, 0)
    w0 = wt_ref[:, 0:1]
    w1 = wt_ref[:, 1:2]
    sq = jnp.zeros((tb, 1), F32)
    for c in range(ROW_TILE):
        cols = slice(c * LANE, (c + 1) * LANE)
        f = _load_token_rows(buf_ref.at[0], tb, c) * w0 + _load_token_rows(buf_ref.at[1], tb, c) * w1
        y = x_ref[:, cols] + gate_ref[0, :, cols] * f
        o_ref[:, cols] = y
        sq = sq + jnp.sum(y * y, axis=-1, keepdims=True)
    if final_norm:
        d = ROW_TILE * LANE
        o_ref[...] = o_ref[...] * lax.rsqrt(sq * (1.0 / d) + NORM_EPS) * ng_ref[...]


def moe_combine(ys_rows, pos, wts, x, gate, rows_per_batch, final_norm_g=None):
    t, d = x.shape
    tb = min(128, t)
    bsz = gate.shape[0]
    final_norm = final_norm_g is not None
    ng = (final_norm_g if final_norm else jnp.ones((d,), F32)).reshape(1, d)
    return pl.pallas_call(
        functools.partial(_combine_kernel, final_norm=final_norm), grid=(t // tb,),
        in_specs=[pl.BlockSpec((1, 1, TOP_K * tb), lambda i: (i, 0, 0), memory_space=pltpu.SMEM),
                  pl.BlockSpec(memory_space=pl.ANY),
                  pl.BlockSpec((tb, d), lambda i: (i, 0)),
                  pl.BlockSpec((1, 1, d), lambda i: ((i * tb) // rows_per_batch, 0, 0)),
                  pl.BlockSpec((tb, TOP_K), lambda i: (i, 0)),
                  pl.BlockSpec((1, d), lambda i: (0, 0))],
        out_specs=pl.BlockSpec((tb, d), lambda i: (i, 0)),
        out_shape=jax.ShapeDtypeStruct((t, d), F32),
        scratch_shapes=[pltpu.VMEM((TOP_K, tb * ROW_TILE, LANE), F32), pltpu.SemaphoreType.DMA],
        compiler_params=_cparams("arbitrary"), name="moe_combine",
    )(pos.reshape(t // tb, 1, TOP_K * tb), ys_rows, x, gate.reshape(bsz, 1, d), wts, ng)


MOE_TM = 1024


def moe_ffn(h_rows, logits, wg, wu, wd, x, gate, rows_per_batch, final_norm_g=None):
    t, d = x.shape
    tm = min(MOE_TM, t)
    info, counts = moe_route(logits)
    sizes = counts[0, :N_EXPERTS].astype(jnp.int32)
    padded = ((sizes + tm - 1) // tm) * tm
    ends = jnp.cumsum(padded)
    starts = ends - padded
    n_tiles = (t * TOP_K) // tm + N_EXPERTS
    n_rows = n_tiles * tm
    experts = info[:, 0:TOP_K].astype(jnp.int32)
    pos = starts[experts] + info[:, 2:2 + TOP_K].astype(jnp.int32)
    wts = info[:, 4:4 + TOP_K]
    tile_start = jnp.arange(n_tiles, dtype=jnp.int32) * tm
    tile_expert = jnp.minimum(jnp.sum((tile_start[:, None] >= ends[None, :]).astype(jnp.int32), axis=1), N_EXPERTS - 1)
    tile_valid = (tile_start < ends[-1]).astype(jnp.int32)
    xs_rows = moe_dispatch(h_rows, pos, n_rows)
    hid = moe_up(xs_rows, wg, wu, tile_expert, tile_valid, tm, 512)
    ys_rows = moe_down(hid, wd, tile_expert, tile_valid, tm, 896)
    return moe_combine(ys_rows, pos, wts, x, gate, rows_per_batch, final_norm_g)


def _rot_cols(w):
    f = ROPE_AXIS_FREQS
    return jnp.concatenate([-w[:, f:2 * f], w[:, 0:f], -w[:, 3 * f:4 * f], w[:, 2 * f:3 * f]], axis=1)


def _arrange_w_in(w):
    d = w.shape[0]
    a = w[:, 0:2 * A_WIDTH]
    off = 2 * A_WIDTH
    qkv_gate = w[:, off:off + 4 * B_WIDTH]
    logit = w[:, off + 4 * B_WIDTH:off + 4 * B_WIDTH + 4 * B_HEADS]
    off = off + 4 * B_WIDTH + 4 * B_HEADS
    lat = w[:, off:off + Q_LORA + KV_LORA]
    k_pe = w[:, off + Q_LORA + KV_LORA:off + Q_LORA + KV_LORA + C_ROPE]
    z64 = jnp.zeros((d, LANE - C_ROPE), w.dtype)
    main = jnp.concatenate([qkv_gate, a, lat, k_pe, z64, _rot_cols(k_pe), z64], axis=1).astype(BF16)
    logit = jnp.pad(logit, ((0, 0), (0, LANE - 4 * B_HEADS))).astype(BF16)
    return main, logit


def _arrange_w_uq(w):
    k = w.shape[0]
    w = w.reshape(k, C_HEADS, C_NOPE + C_ROPE)
    z64 = jnp.zeros((k, C_HEADS, LANE - C_ROPE), w.dtype)
    pe = w[:, :, C_NOPE:]
    pe_rot = jnp.stack([_rot_cols(pe[:, h]) for h in range(C_HEADS)], axis=1)
    return jnp.concatenate([w[:, :, :C_NOPE], pe, z64, pe_rot, z64], axis=2).reshape(k, C_HEADS * C_HEAD_COLS).astype(BF16)


def _rope_tables(n):
    rows = n // GRID_W
    row = jnp.repeat(jnp.arange(rows, dtype=F32), GRID_W)
    col = jnp.tile(jnp.arange(GRID_W, dtype=F32), rows)
    inv = jnp.power(ROPE_BASE, -jnp.arange(ROPE_AXIS_FREQS, dtype=F32) / ROPE_AXIS_FREQS)
    ar = row[:, None] * inv
    ac = col[:, None] * inv
    ang = jnp.concatenate([ar, ar, ac, ac], axis=-1)
    pad = ((0, 0), (0, LANE - C_ROPE))
    return jnp.pad(jnp.cos(ang), pad), jnp.pad(jnp.sin(ang), pad)


def _mixer_branches(z, zg, p, cos2, sin2):
    out_a = chunk_sgu(z, p["sgu_norm_g"], p["sgu_w"], p["sgu_b"])
    qkv, gb = deltanet_prep(z, zg, p["dn_conv_w"], p["dn_a_log"], p["dn_dt_bias"])
    intra = deltanet_intra(qkv, gb)
    q, k, v = mla_prep(z, p["mla_q_norm_g"], p["mla_kv_norm_g"], p["wq_arr"], p["wkv_arr"], cos2, sin2)
    return out_a, intra, (q, k, v)


def kernel(x, c, ctx, c_ctx, ada_w, ada_b, norm1_g, norm2_g, w_in, sgu_norm_g, sgu_w, sgu_b, dn_conv_w, dn_a_log, dn_dt_bias, dn_norm_g, mla_q_norm_g, mla_w_uq, mla_kv_norm_g, mla_w_ukv, w_out, ffn_w_gate, ffn_w_up, ffn_w_down, moe_router, moe_w_gate, moe_w_up, moe_w_down, final_norm_g):
    b, n, d = x.shape
    lc = ctx.shape[1]
    depth = ada_w.shape[0]
    assert d == ROW_TILE * LANE
    cos_lat, sin_lat = _rope_tables(n)
    cos_ctx = jnp.pad(jnp.ones((lc, C_ROPE), F32), ((0, 0), (0, LANE - C_ROPE)))
    sin_ctx = jnp.zeros((lc, LANE), F32)

    cc = jnp.zeros((8, d), F32).at[0:b].set(c).at[b].set(c_ctx)
    mod_all = ada_modulation(cc, ada_w, ada_b)

    xc = ctx
    x_rows = None
    for i in range(depth):
        last = i == depth - 1
        mod = mod_all[i, 0:b].reshape(b, 6, d)
        mod_c = jnp.broadcast_to(mod_all[i, b].reshape(1, 6, d), (b, 6, d))
        w_main, w_logit = _arrange_w_in(w_in[i])
        p = dict(sgu_norm_g=sgu_norm_g[i], sgu_w=sgu_w[i], sgu_b=sgu_b[i], dn_conv_w=dn_conv_w[i],
                 dn_a_log=dn_a_log[i], dn_dt_bias=dn_dt_bias[i], mla_q_norm_g=mla_q_norm_g[i],
                 mla_kv_norm_g=mla_kv_norm_g[i], wq_arr=_arrange_w_uq(mla_w_uq[i]),
                 wkv_arr=mla_w_ukv[i].astype(BF16))
        w_out_b = w_out[i].astype(BF16)

        h = norm_mod(x, norm1_g[i], mod[:, 0], mod[:, 1]).reshape(b * n, d)
        hc = norm_mod(xc, norm1_g[i], mod_c[:, 0], mod_c[:, 1]).reshape(b * lc, d)
        z = matmul(h, w_main, BF16, 1024, 1024).reshape(b, n, Z_COLS)
        zg = matmul(h, w_logit, F32, 1024, LANE).reshape(b, n, LANE)
        zc = matmul(hc, w_main, BF16, 1024, 1024).reshape(b, lc, Z_COLS)
        zgc = matmul(hc, w_logit, F32, 1024, LANE).reshape(b, lc, LANE)

        oa_c, intra_c, (q_c, k_c, v_c) = _mixer_branches(zc, zgc, p, cos_ctx, sin_ctx)
        out_a, intra, (q_l, k_l, v_l) = _mixer_branches(z, zg, p, cos_lat, sin_lat)
        s_zero = jnp.zeros((b, 2 * B_HEADS, B_HEAD_DIM, B_HEAD_DIM), F32)
        ocf, ocb, s_ctx = deltanet_scan(*intra_c, s_zero)
        o_f, o_b, _ = deltanet_scan(*intra, s_ctx)
        out_b = deltanet_out(o_f, o_b, z, dn_norm_g[i])
        out_c = flash_attention(q_l, jnp.concatenate([k_l, k_c], axis=2), jnp.concatenate([v_l, v_c], axis=2))
        x2 = mix_out_residual(out_a.reshape(b * n, -1), out_b.reshape(b * n, -1), out_c.reshape(b * n, -1),
                              w_out_b, x.reshape(b * n, d), mod[:, 2], n, 1024, 1024)
        if not last:
            ob_c = deltanet_out(ocf, ocb, zc, dn_norm_g[i])
            oc_c = flash_attention(q_c, k_c, v_c)
            xc2 = mix_out_residual(oa_c.reshape(b * lc, -1), ob_c.reshape(b * lc, -1), oc_c.reshape(b * lc, -1),
                                   w_out_b, xc.reshape(b * lc, d), mod_c[:, 2], lc, 1024, 1024)

        if i % 2 == 0:
            wg, wu, wd = (ffn_w_gate[i // 2].astype(BF16), ffn_w_up[i // 2].astype(BF16), ffn_w_down[i // 2].astype(BF16))
            h2 = norm_mod(x2.reshape(b, n, d), norm2_g[i], mod[:, 3], mod[:, 4]).reshape(b * n, d)
            hid = swiglu_up(h2, wg, wu, 1024, 512)
            x = down_residual(hid, wd, x2, mod[:, 5], n, 1024, 1024, 1408).reshape(b, n, d)
            if not last:
                hc2 = norm_mod(xc2.reshape(b, lc, d), norm2_g[i], mod_c[:, 3], mod_c[:, 4]).reshape(b * lc, d)
                hid_c = swiglu_up(hc2, wg, wu, 1024, 512)
                xc = down_residual(hid_c, wd, xc2, mod_c[:, 5], lc, 1024, 1024, 1408).reshape(b, lc, d)
        else:
            e = i // 2
            router = jnp.pad(moe_router[e], ((0, 0), (0, LANE - N_EXPERTS)))
            if not last:
                raise NotImplementedError("an expert layer followed by another layer is not part of this model")
            h_rows, logits = norm_mod(x2.reshape(b, n, d), norm2_g[i], mod[:, 3], mod[:, 4], router=router)
            return moe_ffn(h_rows, logits.reshape(b * n, LANE), moe_w_gate[e], moe_w_up[e], moe_w_down[e],
                           x2, mod[:, 5], n, final_norm_g).reshape(b, n, d)
    return rmsnorm_rows(x.reshape(b * n, d), final_norm_g).reshape(b, n, d)
```

```python
import functools
import math

import jax
import jax.numpy as jnp
import numpy as np
from jax import lax
from jax.experimental import pallas as pl
from jax.experimental.pallas import tpu as pltpu

F32 = jnp.float32
BF16 = jnp.bfloat16
HIGHEST = lax.Precision.HIGHEST

NORM_EPS = 1e-6
GRID_W = 64
A_GROUPS = 4
A_GROUP_DIM = 128
A_WIDTH = A_GROUPS * A_GROUP_DIM
A_CHUNK = 128
B_HEADS = 6
B_HEAD_DIM = 128
B_WIDTH = B_HEADS * B_HEAD_DIM
DN_CHUNK = 64
DN_CONV = 5
C_HEADS = 6
C_NOPE = 128
C_ROPE = 64
C_V = 128
C_WIDTH = C_HEADS * C_V
Q_LORA = 512
KV_LORA = 256
ROPE_BASE = 10000.0
ROPE_AXIS_FREQS = C_ROPE // 4
N_EXPERTS = 8
TOP_K = 2

LANE = 128
ROW_TILE = 8
D_TOKEN = 2 * ROW_TILE * LANE
U32 = jnp.uint32
VMEM_LIMIT = 56 * 1024 * 1024

ZC_QKV = 0
ZC_GATE = 3 * B_WIDTH
ZC_A = 4 * B_WIDTH
ZC_C = 4 * B_WIDTH + 2 * A_WIDTH
Z_COLS = ZC_C + 1024
C_HEAD_COLS = 3 * LANE


def _cparams(*sem):
    return pltpu.CompilerParams(dimension_semantics=sem, vmem_limit_bytes=VMEM_LIMIT)


def _tile(n, pref):
    if n <= pref:
        return n
    t = (pref // LANE) * LANE
    while n % t:
        t -= LANE
    return t


def _sigmoid(x):
    return 1.0 / (1.0 + jnp.exp(-x))


def _silu(x):
    return x * _sigmoid(x)


def _dot(a, b):
    return jnp.dot(a, b, preferred_element_type=F32)


def _dot_nt(a, b):
    return lax.dot_general(a, b, (((1,), (1,)), ((), ())), preferred_element_type=F32)


def _dot_tn(a, b):
    return lax.dot_general(a, b, (((0,), (0,)), ((), ())), preferred_element_type=F32)


def _split_bf16(a):
    hi = a.astype(BF16)
    lo = (a - hi.astype(F32)).astype(BF16)
    return hi, lo


def _dot_x3(a, b):
    ah, al = _split_bf16(a)
    bh, bl = _split_bf16(b)
    return _dot(ah, bh) + (_dot(ah, bl) + _dot(al, bh))


def _ada_kernel(c_ref, w_ref, b_ref, o_ref):
    c = c_ref[...]
    o_ref[0] = jnp.dot(_silu(c), w_ref[0], preferred_element_type=F32, precision=HIGHEST) + b_ref[0]


def ada_modulation(cc, ada_w, ada_b):
    depth, d, n = ada_w.shape
    tn = 1024
    return pl.pallas_call(
        _ada_kernel,
        grid=(depth, n // tn),
        in_specs=[
            pl.BlockSpec((8, d), lambda i, j: (0, 0)),
            pl.BlockSpec((1, d, tn), lambda i, j: (i, 0, j)),
            pl.BlockSpec((1, 1, tn), lambda i, j: (i, 0, j)),
        ],
        out_specs=pl.BlockSpec((1, 8, tn), lambda i, j: (i, 0, j)),
        out_shape=jax.ShapeDtypeStruct((depth, 8, n), F32),
        compiler_params=_cparams("parallel", "parallel"),
        name="ada_modulation",
    )(cc, ada_w, ada_b.reshape(depth, 1, n))


def _norm_mod_kernel(x_ref, g_ref, sh_ref, sc_ref, o_ref):
    x = x_ref[0]
    y = x * lax.rsqrt(jnp.mean(x * x, axis=-1, keepdims=True) + NORM_EPS) * g_ref[...]
    o_ref[0] = (y * (1.0 + sc_ref[0]) + sh_ref[0]).astype(o_ref.dtype)


def _norm_mod_router_kernel(x_ref, g_ref, sh_ref, sc_ref, r_ref, o_ref, lg_ref):
    x = x_ref[0]
    y = x * lax.rsqrt(jnp.mean(x * x, axis=-1, keepdims=True) + NORM_EPS) * g_ref[...]
    h = y * (1.0 + sc_ref[0]) + sh_ref[0]
    _store_token_rows(o_ref, h)
    lg_ref[0] = jnp.dot(h, r_ref[...], preferred_element_type=F32, precision=HIGHEST)


def _store_token_rows(rows_ref, x):
    n = x.shape[0]
    for c in range(ROW_TILE):
        lo = lax.bitcast_convert_type(x[:, c * LANE:(c + 1) * LANE].astype(BF16).astype(F32), U32)
        hi = lax.bitcast_convert_type(x[:, (c + ROW_TILE) * LANE:(c + ROW_TILE + 1) * LANE].astype(BF16).astype(F32), U32)
        rows_ref[pl.ds(c, n, stride=ROW_TILE), :] = lax.shift_right_logical(lo, jnp.uint32(16)) | (hi & jnp.uint32(0xFFFF0000))


def _load_token_rows(rows_ref, n, c):
    w = rows_ref[pl.ds(c, n, stride=ROW_TILE), :]
    lo = lax.bitcast_convert_type(lax.shift_left(w, jnp.uint32(16)), F32)
    hi = lax.bitcast_convert_type(w & jnp.uint32(0xFFFF0000), F32)
    return lo, hi


def norm_mod(x, g, shift, scale, out_dtype=BF16, router=None):
    b, l, d = x.shape
    tl = min(512, l)
    in_specs = [
        pl.BlockSpec((1, tl, d), lambda i, j: (i, j, 0)),
        pl.BlockSpec((1, d), lambda i, j: (0, 0)),
        pl.BlockSpec((1, 1, d), lambda i, j: (i, 0, 0)),
        pl.BlockSpec((1, 1, d), lambda i, j: (i, 0, 0)),
    ]
    args = [x, g.reshape(1, d), shift.reshape(b, 1, d), scale.reshape(b, 1, d)]
    h_spec = pl.BlockSpec((1, tl, d), lambda i, j: (i, j, 0))
    h_shape = jax.ShapeDtypeStruct((b, l, d), out_dtype)
    if router is None:
        return pl.pallas_call(
            _norm_mod_kernel, grid=(b, l // tl), in_specs=in_specs, out_specs=h_spec, out_shape=h_shape,
            compiler_params=_cparams("parallel", "parallel"), name="norm_mod",
        )(*args)
    assert d == D_TOKEN
    in_specs.append(pl.BlockSpec((d, LANE), lambda i, j: (0, 0)))
    nl = l // tl
    return pl.pallas_call(
        _norm_mod_router_kernel, grid=(b, nl), in_specs=in_specs,
        out_specs=[pl.BlockSpec((tl * ROW_TILE, LANE), lambda i, j: (i * nl + j, 0)),
                   pl.BlockSpec((1, tl, LANE), lambda i, j: (i, j, 0))],
        out_shape=[jax.ShapeDtypeStruct((b * l * ROW_TILE, LANE), U32), jax.ShapeDtypeStruct((b, l, LANE), F32)],
        compiler_params=_cparams("parallel", "parallel"), name="norm_mod_router",
    )(*args, router)


def _rmsnorm_kernel(x_ref, g_ref, o_ref):
    x = x_ref[...]
    o_ref[...] = x * lax.rsqrt(jnp.mean(x * x, axis=-1, keepdims=True) + NORM_EPS) * g_ref[...]


def rmsnorm_rows(x, g):
    m, d = x.shape
    tm = min(512, m)
    return pl.pallas_call(
        _rmsnorm_kernel, grid=(m // tm,),
        in_specs=[pl.BlockSpec((tm, d), lambda i: (i, 0)), pl.BlockSpec((1, d), lambda i: (0, 0))],
        out_specs=pl.BlockSpec((tm, d), lambda i: (i, 0)),
        out_shape=jax.ShapeDtypeStruct((m, d), F32),
        compiler_params=_cparams("parallel"), name="final_rmsnorm",
    )(x, g.reshape(1, d))


def _mm_kernel(a_ref, w_ref, o_ref):
    o_ref[...] = _dot(a_ref[...], w_ref[...]).astype(o_ref.dtype)


def matmul(a, w, out_dtype, tm, tn):
    m, k = a.shape
    n = w.shape[1]
    tm, tn = _tile(m, tm), _tile(n, tn)
    return pl.pallas_call(
        _mm_kernel, grid=(m // tm, n // tn),
        in_specs=[pl.BlockSpec((tm, k), lambda i, j: (i, 0)), pl.BlockSpec((k, tn), lambda i, j: (0, j))],
        out_specs=pl.BlockSpec((tm, tn), lambda i, j: (i, j)),
        out_shape=jax.ShapeDtypeStruct((m, n), out_dtype),
        compiler_params=_cparams("parallel", "parallel"), name="matmul",
    )(a, w)


def _swiglu_up_kernel(a_ref, wg_ref, wu_ref, o_ref):
    a = a_ref[...]
    g = _dot(a, wg_ref[...])
    u = _dot(a, wu_ref[...])
    o_ref[...] = (_silu(g) * u).astype(o_ref.dtype)


def swiglu_up(a, wg, wu, tm, tn):
    m, k = a.shape
    n = wg.shape[1]
    tm, tn = _tile(m, tm), _tile(n, tn)
    return pl.pallas_call(
        _swiglu_up_kernel, grid=(m // tm, n // tn),
        in_specs=[pl.BlockSpec((tm, k), lambda i, j: (i, 0)),
                  pl.BlockSpec((k, tn), lambda i, j: (0, j)),
                  pl.BlockSpec((k, tn), lambda i, j: (0, j))],
        out_specs=pl.BlockSpec((tm, tn), lambda i, j: (i, j)),
        out_shape=jax.ShapeDtypeStruct((m, n), BF16),
        compiler_params=_cparams("parallel", "parallel"), name="swiglu_up",
    )(a, wg, wu)


def _down_res_kernel(a_ref, w_ref, x_ref, gate_ref, o_ref, acc_ref):
    k = pl.program_id(2)

    @pl.when(k == 0)
    def _():
        acc_ref[...] = _dot(a_ref[...], w_ref[...])

    @pl.when(k > 0)
    def _():
        acc_ref[...] += _dot(a_ref[...], w_ref[...])

    @pl.when(k == pl.num_programs(2) - 1)
    def _():
        o_ref[...] = x_ref[...] + gate_ref[0] * acc_ref[...]


def down_residual(a, w, x, gate, rows_per_batch, tm, tn, tk):
    m, kdim = a.shape
    n = w.shape[1]
    tm, tn, tk = _tile(rows_per_batch, tm), _tile(n, tn), _tile(kdim, tk)
    bsz = gate.shape[0]
    return pl.pallas_call(
        _down_res_kernel, grid=(m // tm, n // tn, kdim // tk),
        in_specs=[pl.BlockSpec((tm, tk), lambda i, j, k: (i, k)),
                  pl.BlockSpec((tk, tn), lambda i, j, k: (k, j)),
                  pl.BlockSpec((tm, tn), lambda i, j, k: (i, j)),
                  pl.BlockSpec((1, 1, tn), lambda i, j, k: ((i * tm) // rows_per_batch, 0, j))],
        out_specs=pl.BlockSpec((tm, tn), lambda i, j, k: (i, j)),
        out_shape=jax.ShapeDtypeStruct((m, n), F32),
        scratch_shapes=[pltpu.VMEM((tm, tn), F32)],
        compiler_params=_cparams("parallel", "parallel", "arbitrary"), name="down_residual",
    )(a, w, x, gate.reshape(bsz, 1, n))


def _mix_out_kernel(a_ref, b_ref, c_ref, w_ref, x_ref, gate_ref, o_ref):
    ka, kb = a_ref.shape[1], b_ref.shape[1]
    acc = _dot(a_ref[...], w_ref[0:ka, :])
    acc += _dot(b_ref[...], w_ref[ka:ka + kb, :])
    acc += _dot(c_ref[...], w_ref[ka + kb:, :])
    o_ref[...] = x_ref[...] + gate_ref[0] * acc


def mix_out_residual(oa, ob, oc, w, x, gate, rows_per_batch, tm, tn):
    m = oa.shape[0]
    kdim, n = w.shape
    tm, tn = _tile(rows_per_batch, tm), _tile(n, tn)
    bsz = gate.shape[0]
    return pl.pallas_call(
        _mix_out_kernel, grid=(m // tm, n // tn),
        in_specs=[pl.BlockSpec((tm, oa.shape[1]), lambda i, j: (i, 0)),
                  pl.BlockSpec((tm, ob.shape[1]), lambda i, j: (i, 0)),
                  pl.BlockSpec((tm, oc.shape[1]), lambda i, j: (i, 0)),
                  pl.BlockSpec((kdim, tn), lambda i, j: (0, j)),
                  pl.BlockSpec((tm, tn), lambda i, j: (i, j)),
                  pl.BlockSpec((1, 1, tn), lambda i, j: ((i * tm) // rows_per_batch, 0, j))],
        out_specs=pl.BlockSpec((tm, tn), lambda i, j: (i, j)),
        out_shape=jax.ShapeDtypeStruct((m, n), F32),
        compiler_params=_cparams("parallel", "parallel"), name="mix_out_residual",
    )(oa, ob, oc, w, x, gate.reshape(bsz, 1, n))


def _gelu_tanh(x):
    return 0.5 * x * (1.0 + jnp.tanh(math.sqrt(2.0 / math.pi) * (x + 0.044715 * (x * x * x))))


def _sgu_kernel(z_ref, g_ref, w_ref, b_ref, o_ref):
    tl = z_ref.shape[1]
    for c in range(tl // A_CHUNK):
        rows = slice(c * A_CHUNK, (c + 1) * A_CHUNK)
        for g in range(A_GROUPS):
            cols = slice(g * A_GROUP_DIM, (g + 1) * A_GROUP_DIM)
            u = _gelu_tanh(z_ref[0, rows, cols].astype(F32))
            v = _gelu_tanh(z_ref[0, rows, A_WIDTH + g * A_GROUP_DIM:A_WIDTH + (g + 1) * A_GROUP_DIM].astype(F32))
            vc = v - jnp.mean(v, axis=-1, keepdims=True)
            vn = vc * lax.rsqrt(jnp.mean(vc * vc, axis=-1, keepdims=True) + NORM_EPS) * g_ref[:, cols]
            mixed = _dot(w_ref[g], vn.astype(BF16)) + b_ref[g]
            o_ref[0, rows, cols] = (u * mixed).astype(o_ref.dtype)


def chunk_sgu(z, norm_g, w_s, b_s):
    b, l, _ = z.shape
    tl = min(512, l)
    blk = ZC_A // (2 * A_WIDTH)
    return pl.pallas_call(
        _sgu_kernel, grid=(b, l // tl),
        in_specs=[pl.BlockSpec((1, tl, 2 * A_WIDTH), lambda i, j: (i, j, blk)),
                  pl.BlockSpec((1, A_WIDTH), lambda i, j: (0, 0)),
                  pl.BlockSpec((A_GROUPS, A_CHUNK, A_CHUNK), lambda i, j: (0, 0, 0)),
                  pl.BlockSpec((A_GROUPS, A_CHUNK, 1), lambda i, j: (0, 0, 0))],
        out_specs=pl.BlockSpec((1, tl, A_WIDTH), lambda i, j: (i, j, 0)),
        out_shape=jax.ShapeDtypeStruct((b, l, A_WIDTH), BF16),
        compiler_params=_cparams("parallel", "parallel"), name="chunk_sgu",
    )(z, norm_g.reshape(1, A_WIDTH), w_s.astype(BF16), b_s.reshape(A_GROUPS, A_CHUNK, 1))


def _mla_prep_kernel(z_ref, qg_ref, kvg_ref, wq_ref, wkv_ref, cos_ref, sin_ref, q_ref, k_ref, v_ref, *, q_scale):
    lat = z_ref[0, :, 0:Q_LORA].astype(F32)
    cq = (lat * lax.rsqrt(jnp.mean(lat * lat, axis=-1, keepdims=True) + NORM_EPS) * qg_ref[...]).astype(BF16)
    kvl = z_ref[0, :, Q_LORA:Q_LORA + KV_LORA].astype(F32)
    ckv = (kvl * lax.rsqrt(jnp.mean(kvl * kvl, axis=-1, keepdims=True) + NORM_EPS) * kvg_ref[...]).astype(BF16)
    cos = cos_ref[...]
    sin = sin_ref[...]
    base = Q_LORA + KV_LORA
    k_pe = z_ref[0, :, base:base + LANE].astype(F32) * cos + z_ref[0, :, base + LANE:base + 2 * LANE].astype(F32) * sin
    k_pe = k_pe.astype(k_ref.dtype)
    for h in range(C_HEADS):
        qh = _dot(cq, wq_ref[:, h * C_HEAD_COLS:(h + 1) * C_HEAD_COLS])
        q_pe = qh[:, LANE:2 * LANE] * cos + qh[:, 2 * LANE:3 * LANE] * sin
        q_ref[0, h, :, 0:LANE] = (qh[:, 0:LANE] * q_scale).astype(q_ref.dtype)
        q_ref[0, h, :, LANE:2 * LANE] = (q_pe * q_scale).astype(q_ref.dtype)
        kv = _dot(ckv, wkv_ref[:, h * 2 * LANE:(h + 1) * 2 * LANE])
        k_ref[0, h, :, 0:LANE] = kv[:, 0:LANE].astype(k_ref.dtype)
        k_ref[0, h, :, LANE:2 * LANE] = k_pe
        v_ref[0, h] = kv[:, LANE:2 * LANE].astype(v_ref.dtype)


def mla_prep(z, q_norm_g, kv_norm_g, wq_arr, wkv_arr, cos2, sin2):
    b, l, _ = z.shape
    tl = min(256, l)
    blk = ZC_C // 1024
    q_scale = (C_NOPE + C_ROPE) ** -0.5 * math.log2(math.e)
    return pl.pallas_call(
        functools.partial(_mla_prep_kernel, q_scale=q_scale), grid=(b, l // tl),
        in_specs=[pl.BlockSpec((1, tl, 1024), lambda i, j: (i, j, blk)),
                  pl.BlockSpec((1, Q_LORA), lambda i, j: (0, 0)),
                  pl.BlockSpec((1, KV_LORA), lambda i, j: (0, 0)),
                  pl.BlockSpec((Q_LORA, C_HEADS * C_HEAD_COLS), lambda i, j: (0, 0)),
                  pl.BlockSpec((KV_LORA, C_HEADS * 2 * LANE), lambda i, j: (0, 0)),
                  pl.BlockSpec((tl, LANE), lambda i, j: (j, 0)),
                  pl.BlockSpec((tl, LANE), lambda i, j: (j, 0))],
        out_specs=[pl.BlockSpec((1, C_HEADS, tl, 2 * LANE), lambda i, j: (i, 0, j, 0)),
                   pl.BlockSpec((1, C_HEADS, tl, 2 * LANE), lambda i, j: (i, 0, j, 0)),
                   pl.BlockSpec((1, C_HEADS, tl, LANE), lambda i, j: (i, 0, j, 0))],
        out_shape=[jax.ShapeDtypeStruct((b, C_HEADS, l, 2 * LANE), BF16),
                   jax.ShapeDtypeStruct((b, C_HEADS, l, 2 * LANE), BF16),
                   jax.ShapeDtypeStruct((b, C_HEADS, l, LANE), BF16)],
        compiler_params=_cparams("parallel", "parallel"), name="mla_prep",
    )(z, q_norm_g.reshape(1, Q_LORA), kv_norm_g.reshape(1, KV_LORA), wq_arr, wkv_arr, cos2, sin2)


FLASH_ROWS = 32


def _flash_kernel(q_ref, k_ref, v_ref, o_ref, s_ref, p_ref, m_ref, l_ref, acc_ref, *, tk, nk):
    tq = q_ref.shape[2]
    rb = min(FLASH_ROWS, tq)
    m_ref[...] = jnp.full(m_ref.shape, -jnp.inf, F32)
    l_ref[...] = jnp.zeros(l_ref.shape, F32)
    acc_ref[...] = jnp.zeros(acc_ref.shape, F32)

    for j in range(nk):
        s_buf, p_buf = s_ref.at[j % 2], p_ref.at[j % 2]
        keys = slice(j * tk, (j + 1) * tk)
        s_buf[...] = _dot_nt(q_ref[0, 0], k_ref[0, 0, keys, :])
        for r in range(tq // rb):
            rows = slice(r * rb, (r + 1) * rb)
            blocks = [s_buf[rows, t * LANE:(t + 1) * LANE] for t in range(tk // LANE)]
            mx = blocks[0]
            for blk in blocks[1:]:
                mx = jnp.maximum(mx, blk)
            m_old = m_ref[rows, :]
            m_new = jnp.maximum(m_old, jnp.broadcast_to(jnp.max(mx, axis=-1, keepdims=True), (rb, LANE)))
            alpha = jnp.exp2(m_old - m_new)
            lane_sum = None
            for t, blk in enumerate(blocks):
                p = jnp.exp2(blk - m_new)
                lane_sum = p if lane_sum is None else lane_sum + p
                p_buf[rows, t * LANE:(t + 1) * LANE] = p.astype(BF16)
            l_ref[rows, :] = alpha * l_ref[rows, :] + lane_sum
            m_ref[rows, :] = m_new
            acc_ref[rows, :] = alpha * acc_ref[rows, :]
        acc_ref[...] += _dot(p_buf[...], v_ref[0, 0, keys, :])
    o_ref[0] = (acc_ref[...] / jnp.sum(l_ref[...], axis=-1, keepdims=True)).astype(o_ref.dtype)


def flash_attention(q, k, v):
    b, h, lq, dq = q.shape
    lk = k.shape[2]
    tq = min(512, lq)
    tk = _tile(lk, 256)
    return pl.pallas_call(
        functools.partial(_flash_kernel, tk=tk, nk=lk // tk), grid=(b, h, lq // tq),
        in_specs=[pl.BlockSpec((1, 1, tq, dq), lambda i, j, t: (i, j, t, 0)),
                  pl.BlockSpec((1, 1, lk, dq), lambda i, j, t: (i, j, 0, 0)),
                  pl.BlockSpec((1, 1, lk, C_V), lambda i, j, t: (i, j, 0, 0))],
        out_specs=pl.BlockSpec((1, tq, C_V), lambda i, j, t: (i, t, j)),
        out_shape=jax.ShapeDtypeStruct((b, lq, h * C_V), BF16),
        scratch_shapes=[pltpu.VMEM((2, tq, tk), F32), pltpu.VMEM((2, tq, tk), BF16), pltpu.VMEM((tq, LANE), F32),
                        pltpu.VMEM((tq, LANE), F32), pltpu.VMEM((tq, C_V), F32)],
        compiler_params=_cparams("parallel", "parallel", "parallel"), name="flash_attention",
    )(q, k, v)


def _softplus(x):
    return jnp.maximum(x, 0.0) + jnp.log1p(jnp.exp(-jnp.abs(x)))


def _dn_prep_kernel(zm_ref, zp_ref, zn_ref, zg_ref, w_ref, alog_ref, dtb_ref, qkv_ref, gb_ref):
    j = pl.program_id(1)
    tl = zm_ref.shape[1]
    half = DN_CONV // 2
    keep_prev = (j > 0).astype(F32)
    keep_next = (j < pl.num_programs(1) - 1).astype(F32)
    for c in range(3 * B_HEADS):
        cols = slice(c * LANE, (c + 1) * LANE)
        prev = zp_ref[0, :, cols].astype(F32)[8:16] * keep_prev
        nxt = zn_ref[0, :, cols].astype(F32)[0:8] * keep_next
        ext = jnp.concatenate([prev, zm_ref[0, :, cols].astype(F32), nxt], axis=0)
        y = ext[8 - half:8 - half + tl] * w_ref[0:1, cols]
        for i in range(1, DN_CONV):
            y = y + ext[8 - half + i:8 - half + i + tl] * w_ref[i:i + 1, cols]
        y = _silu(y)
        if c < 2 * B_HEADS:
            y = y * lax.rsqrt(jnp.sum(y * y, axis=-1, keepdims=True) + NORM_EPS)
        qkv_ref[0, :, cols] = y.astype(qkv_ref.dtype)
    zg = zg_ref[0]
    lane = lax.broadcasted_iota(jnp.int32, zg.shape, 1)
    g = -jnp.exp(alog_ref[...]) * _softplus(zg + dtb_ref[...])
    gb_ref[0] = jnp.where(lane < 2 * B_HEADS, g, _sigmoid(zg))


def deltanet_prep(z, zg, conv_w, a_log, dt_bias):
    b, l, _ = z.shape
    tl = min(256, l)
    wq = 3 * B_WIDTH
    nb16 = l // 16
    pad = LANE - 2 * B_HEADS
    alog = jnp.pad(a_log.reshape(1, -1), ((0, 0), (0, pad)))
    dtb = jnp.pad(dt_bias.reshape(1, -1), ((0, 0), (0, pad)))
    return pl.pallas_call(
        _dn_prep_kernel, grid=(b, l // tl),
        in_specs=[pl.BlockSpec((1, tl, wq), lambda i, j: (i, j, 0)),
                  pl.BlockSpec((1, 16, wq), lambda i, j: (i, jnp.maximum(j * (tl // 16) - 1, 0), 0)),
                  pl.BlockSpec((1, 16, wq), lambda i, j: (i, jnp.minimum((j + 1) * (tl // 16), nb16 - 1), 0)),
                  pl.BlockSpec((1, tl, LANE), lambda i, j: (i, j, 0)),
                  pl.BlockSpec((DN_CONV, wq), lambda i, j: (0, 0)),
                  pl.BlockSpec((1, LANE), lambda i, j: (0, 0)),
                  pl.BlockSpec((1, LANE), lambda i, j: (0, 0))],
        out_specs=[pl.BlockSpec((1, tl, wq), lambda i, j: (i, j, 0)),
                   pl.BlockSpec((1, tl, LANE), lambda i, j: (i, j, 0))],
        out_shape=[jax.ShapeDtypeStruct((b, l, wq), BF16), jax.ShapeDtypeStruct((b, l, LANE), F32)],
        compiler_params=_cparams("parallel", "parallel"), name="deltanet_prep",
    )(z, z, z, zg, conv_w, alog, dtb)


def _dot_bf16(a, b):
    return _dot(a.astype(BF16), b.astype(BF16))


def _dn_intra_kernel(qkv_ref, gb_ref, u_ref, w_ref, qg_ref, kg_ref, qk_ref, gl_ref, *, chunks, passes):
    c = DN_CHUNK
    row = lax.broadcasted_iota(jnp.int32, (c, LANE), 0)
    lane = lax.broadcasted_iota(jnp.int32, (c, LANE), 1)
    col = jnp.where(lane < c, lane, lane - c)
    fwd = lane < c
    bwd = jnp.logical_not(fwd)
    incl = (fwd & (row >= col)) | (bwd & (row <= col))
    strict = (fwd & (row > col)) | (bwd & (row < col))
    same16 = lax.shift_right_logical(row, 4) == lax.shift_right_logical(col, 4)
    same32 = lax.shift_right_logical(row, 5) == lax.shift_right_logical(col, 5)
    diag = row == col
    eye2 = diag.astype(F32)
    r64 = lax.broadcasted_iota(jnp.int32, (c, c), 0)
    c64 = lax.broadcasted_iota(jnp.int32, (c, c), 1)
    tri_lo = (r64 >= c64).astype(F32)
    tri_up = (r64 <= c64).astype(F32)
    scale = B_HEAD_DIM ** -0.5
    dotp = _dot_x3 if passes == 3 else _dot_bf16

    def pick(x, idx):
        return jnp.sum(jnp.where(lane == idx, x, 0.0), axis=-1, keepdims=True)

    def blockdiag(y2):
        return jnp.concatenate([jnp.where(fwd, y2, 0.0), jnp.where(fwd, 0.0, y2)], axis=0)

    def mm(xs, ys):
        return [dotp(x2, blockdiag(y2)) for x2, y2 in zip(xs, ys)]

    rows = [slice(ci * c, (ci + 1) * c) for ci in range(chunks)]
    gb = [gb_ref[0, rs, :] for rs in rows]
    cum_f = [jnp.dot(tri_lo, g, preferred_element_type=F32, precision=HIGHEST) for g in gb]
    cum_b = [jnp.dot(tri_up, g, preferred_element_type=F32, precision=HIGHEST) for g in gb]
    tot_f = [x[c - 1:c, :] for x in cum_f]
    tot_b = [x[0:1, :] for x in cum_b]

    units = [(ci, h) for ci in range(chunks) for h in range(B_HEADS)]
    q = [qkv_ref[0, rows[ci], h * LANE:(h + 1) * LANE] for ci, h in units]
    k = [qkv_ref[0, rows[ci], B_WIDTH + h * LANE:B_WIDTH + (h + 1) * LANE] for ci, h in units]
    v = [qkv_ref[0, rows[ci], 2 * B_WIDTH + h * LANE:2 * B_WIDTH + (h + 1) * LANE].astype(F32) for ci, h in units]
    k2 = [jnp.concatenate([x, x], axis=0) for x in k]
    kk2 = [_dot_nt(x, y) for x, y in zip(k, k2)]
    qk2 = [_dot_nt(x, y) for x, y in zip(q, k2)]
    cf = [pick(cum_f[ci], h) for ci, h in units]
    cb = [pick(cum_b[ci], B_HEADS + h) for ci, h in units]
    bf = [pick(gb[ci], 2 * B_HEADS + h) for ci, h in units]
    bb = [pick(gb[ci], 3 * B_HEADS + h) for ci, h in units]
    lf = [pick(jnp.broadcast_to(tot_f[ci], (c, LANE)), h) for ci, h in units]
    lb = [pick(jnp.broadcast_to(tot_b[ci], (c, LANE)), B_HEADS + h) for ci, h in units]
    c2 = [jnp.where(fwd, x, y) for x, y in zip(cf, cb)]
    r2 = [jnp.sum(jnp.where(diag, x, 0.0), axis=0, keepdims=True) for x in c2]
    decay2 = [jnp.where(incl, jnp.exp(jnp.where(incl, x - y, 0.0)), 0.0) for x, y in zip(c2, r2)]
    l2 = [jnp.where(strict, jnp.where(fwd, x, y) * kk * dc, 0.0) for x, y, kk, dc in zip(bf, bb, kk2, decay2)]
    mp = [jnp.where(same16, -x, 0.0) for x in l2]
    p = [eye2 + x for x in mp]
    for _ in range(3):
        mp = mm(mp, mp)
        p = [x + y for x, y in zip(p, mm(p, mp))]
    off = [jnp.where(same32 & jnp.logical_not(same16), x, 0.0) for x in l2]
    p = [x - y for x, y in zip(p, mm(mm(p, off), p))]
    off = [jnp.where(same32, 0.0, x) for x in l2]
    p = [x - y for x, y in zip(p, mm(mm(p, off), p))]
    ef = [jnp.exp(x) for x in cf]
    eb = [jnp.exp(x) for x in cb]
    zero = jnp.zeros((c, LANE), F32)
    sol = []
    for i in range(len(units)):
        kf = k[i].astype(F32)
        rhs = jnp.concatenate([
            jnp.concatenate([v[i] * bf[i], kf * (bf[i] * ef[i]), zero, zero], axis=1),
            jnp.concatenate([zero, zero, v[i] * bb[i], kf * (bb[i] * eb[i])], axis=1)], axis=0)
        sol.append(dotp(p[i], rhs))
    for i, (ci, h) in enumerate(units):
        hc = slice(h * LANE, (h + 1) * LANE)
        rs = rows[ci]
        kf = k[i].astype(F32)
        qf = q[i].astype(F32)
        u_ref[0, 0, rs, hc] = sol[i][:, 0:LANE]
        w_ref[0, 0, rs, hc] = sol[i][:, LANE:2 * LANE].astype(w_ref.dtype)
        u_ref[1, 0, rs, hc] = sol[i][:, 2 * LANE:3 * LANE]
        w_ref[1, 0, rs, hc] = sol[i][:, 3 * LANE:4 * LANE].astype(w_ref.dtype)
        qg_ref[0, 0, rs, hc] = (qf * (ef[i] * scale)).astype(qg_ref.dtype)
        qg_ref[1, 0, rs, hc] = (qf * (eb[i] * scale)).astype(qg_ref.dtype)
        kg_ref[0, 0, rs, hc] = (kf * jnp.exp(lf[i] - cf[i])).astype(kg_ref.dtype)
        kg_ref[1, 0, rs, hc] = (kf * jnp.exp(lb[i] - cb[i])).astype(kg_ref.dtype)
        qk_ref[0, rs, hc] = (qk2[i] * decay2[i] * scale).astype(qk_ref.dtype)
    r8 = lax.broadcasted_iota(jnp.int32, (2 * 8, LANE), 0)
    l8 = lax.broadcasted_iota(jnp.int32, (2 * 8, LANE), 1)
    want = jnp.where(r8 < 8, r8, r8 - 8 + B_HEADS)
    for ci in range(chunks):
        src = jnp.concatenate([jnp.broadcast_to(tot_f[ci], (8, LANE)), jnp.broadcast_to(tot_b[ci], (8, LANE))], axis=0)
        tot = jnp.sum(jnp.where(l8 == want, src, 0.0), axis=-1, keepdims=True)
        gl_ref[0, ci] = jnp.broadcast_to(jnp.exp(tot), (2 * 8, LANE))


DN_INTRA_CHUNKS = 4
DN_INTRA_PASSES = 1
DN_SCAN_CHUNKS = 4


def deltanet_intra(qkv, gb):
    b, l, _ = qkv.shape
    nc = l // DN_CHUNK
    chunks = math.gcd(DN_INTRA_CHUNKS, nc)
    tl = chunks * DN_CHUNK
    dir_spec = pl.BlockSpec((2, 1, tl, B_WIDTH), lambda i, j: (0, i, j, 0))
    return pl.pallas_call(
        functools.partial(_dn_intra_kernel, chunks=chunks, passes=DN_INTRA_PASSES), grid=(b, nc // chunks),
        in_specs=[pl.BlockSpec((1, tl, 3 * B_WIDTH), lambda i, j: (i, j, 0)),
                  pl.BlockSpec((1, tl, LANE), lambda i, j: (i, j, 0))],
        out_specs=[dir_spec, dir_spec, dir_spec, dir_spec,
                   pl.BlockSpec((1, tl, B_WIDTH), lambda i, j: (i, j, 0)),
                   pl.BlockSpec((1, chunks, 16, LANE), lambda i, j: (i, j, 0, 0))],
        out_shape=[jax.ShapeDtypeStruct((2, b, l, B_WIDTH), F32),
                   jax.ShapeDtypeStruct((2, b, l, B_WIDTH), BF16),
                   jax.ShapeDtypeStruct((2, b, l, B_WIDTH), BF16),
                   jax.ShapeDtypeStruct((2, b, l, B_WIDTH), BF16),
                   jax.ShapeDtypeStruct((b, l, B_WIDTH), BF16),
                   jax.ShapeDtypeStruct((b, nc, 16, LANE), F32)],
        compiler_params=_cparams("parallel", "parallel"), name="deltanet_intra",
    )(qkv, gb)


def _dn_scan_kernel(uf_ref, wf_ref, qgf_ref, kgf_ref, qkf_ref, glf_ref,
                    ub_ref, wb_ref, qgb_ref, kgb_ref, qkb_ref, glb_ref, s0_ref,
                    of_ref, ob_ref, sfin_ref, s_ref, *, chunks):
    j = pl.program_id(1)
    c = DN_CHUNK

    @pl.when(j == 0)
    def _():
        s_ref[...] = s0_ref[0]

    lane = lax.broadcasted_iota(jnp.int32, (c, LANE), 1)
    zeros_b = jnp.zeros((c, LANE), BF16)
    dirs = ((uf_ref, wf_ref, qgf_ref, kgf_ref, qkf_ref, glf_ref, of_ref),
            (ub_ref, wb_ref, qgb_ref, kgb_ref, qkb_ref, glb_ref, ob_ref))
    units = [(d, h) for d in range(2) for h in range(B_HEADS)]
    cols = [slice(h * LANE, (h + 1) * LANE) for _, h in units]
    state = [s_ref[d * B_HEADS + h] for d, h in units]
    for step in range(chunks):
        chunk_of = (step, chunks - 1 - step)
        rows = [slice(chunk_of[d] * c, (chunk_of[d] + 1) * c) for d, _ in units]
        r = [_dot(jnp.concatenate([dirs[d][1][0, 0, rs, hc], dirs[d][2][0, 0, rs, hc]], axis=0), s.astype(BF16))
             for (d, _), hc, rs, s in zip(units, cols, rows, state)]
        vb = [(dirs[d][0][0, 0, rs, hc] - ri[0:c]).astype(BF16) for (d, _), hc, rs, ri in zip(units, cols, rows, r)]
        new_state = []
        for (d, h), hc, rs, s, ri, vi in zip(units, cols, rows, state, r, vb):
            qk2 = dirs[d][4][0, rs, hc]
            if d == 0:
                intra = _dot(jnp.where(lane < c, qk2, jnp.zeros_like(qk2)), jnp.concatenate([vi, zeros_b], axis=0))
            else:
                intra = _dot(jnp.where(lane < c, jnp.zeros_like(qk2), qk2), jnp.concatenate([zeros_b, vi], axis=0))
            dirs[d][6][0, rs, hc] = ri[c:2 * c] + intra
            gl = dirs[d][5][0, chunk_of[d], d * 8 + h:d * 8 + h + 1, :]
            new_state.append(s * gl + _dot_tn(dirs[d][3][0, 0, rs, hc], vi))
        state = new_state
    for (d, h), s in zip(units, state):
        s_ref[d * B_HEADS + h] = s

    @pl.when(j == pl.num_programs(1) - 1)
    def _():
        sfin_ref[0] = s_ref[...]


def deltanet_scan(u, w, qg, kg, qk, gl, s0):
    _, b, l, _ = u.shape
    chunks = math.gcd(DN_SCAN_CHUNKS, l // DN_CHUNK)
    tl = chunks * DN_CHUNK
    nb = l // tl
    fdir = pl.BlockSpec((1, 1, tl, B_WIDTH), lambda i, j: (0, i, j, 0))
    bdir = pl.BlockSpec((1, 1, tl, B_WIDTH), lambda i, j: (1, i, nb - 1 - j, 0))
    fqk = pl.BlockSpec((1, tl, B_WIDTH), lambda i, j: (i, j, 0))
    bqk = pl.BlockSpec((1, tl, B_WIDTH), lambda i, j: (i, nb - 1 - j, 0))
    fgl = pl.BlockSpec((1, chunks, 16, LANE), lambda i, j: (i, j, 0, 0))
    bgl = pl.BlockSpec((1, chunks, 16, LANE), lambda i, j: (i, nb - 1 - j, 0, 0))
    st = pl.BlockSpec((1, 2 * B_HEADS, B_HEAD_DIM, B_HEAD_DIM), lambda i, j: (i, 0, 0, 0))
    return pl.pallas_call(
        functools.partial(_dn_scan_kernel, chunks=chunks), grid=(b, nb),
        in_specs=[fdir, fdir, fdir, fdir, fqk, fgl, bdir, bdir, bdir, bdir, bqk, bgl, st],
        out_specs=[fqk, bqk, st],
        out_shape=[jax.ShapeDtypeStruct((b, l, B_WIDTH), F32),
                   jax.ShapeDtypeStruct((b, l, B_WIDTH), F32),
                   jax.ShapeDtypeStruct((b, 2 * B_HEADS, B_HEAD_DIM, B_HEAD_DIM), F32)],
        scratch_shapes=[pltpu.VMEM((2 * B_HEADS, B_HEAD_DIM, B_HEAD_DIM), F32)],
        compiler_params=_cparams("parallel", "arbitrary"), name="deltanet_scan",
    )(u, w, qg, kg, qk, gl, u, w, qg, kg, qk, gl, s0)


def _dn_out_kernel(of_ref, ob_ref, gate_ref, g_ref, o_ref):
    for h in range(B_HEADS):
        hc = slice(h * LANE, (h + 1) * LANE)
        o = of_ref[0, :, hc] + ob_ref[0, :, hc]
        y = o * lax.rsqrt(jnp.mean(o * o, axis=-1, keepdims=True) + NORM_EPS) * g_ref[...]
        o_ref[0, :, hc] = (y * _silu(gate_ref[0, :, hc].astype(F32))).astype(o_ref.dtype)


def deltanet_out(o_f, o_b, z, norm_g):
    b, l, _ = o_f.shape
    tl = min(512, l)
    blk = ZC_GATE // B_WIDTH
    return pl.pallas_call(
        _dn_out_kernel, grid=(b, l // tl),
        in_specs=[pl.BlockSpec((1, tl, B_WIDTH), lambda i, j: (i, j, 0)),
                  pl.BlockSpec((1, tl, B_WIDTH), lambda i, j: (i, j, 0)),
                  pl.BlockSpec((1, tl, B_WIDTH), lambda i, j: (i, j, blk)),
                  pl.BlockSpec((1, B_HEAD_DIM), lambda i, j: (0, 0))],
        out_specs=pl.BlockSpec((1, tl, B_WIDTH), lambda i, j: (i, j, 0)),
        out_shape=jax.ShapeDtypeStruct((b, l, B_WIDTH), BF16),
        compiler_params=_cparams("parallel", "parallel"), name="deltanet_out",
    )(o_f, o_b, z, norm_g.reshape(1, B_HEAD_DIM))


def _route_kernel(lg_ref, info_ref, cnt_ref, carry_ref):
    i = pl.program_id(0)

    @pl.when(i == 0)
    def _():
        carry_ref[...] = jnp.zeros_like(carry_ref)

    lg = lg_ref[...]
    tl = lg.shape[0]
    lane = lax.broadcasted_iota(jnp.int32, lg.shape, 1)
    valid = lane < N_EXPERTS
    lg = jnp.where(valid, lg, -jnp.inf)
    e = jnp.exp(lg - jnp.max(lg, axis=-1, keepdims=True))
    p = e / jnp.sum(e, axis=-1, keepdims=True)
    p = jnp.where(valid, p, -1.0)
    p1 = jnp.max(p, axis=-1, keepdims=True)
    i1 = jnp.min(jnp.where(p == p1, lane, LANE), axis=-1, keepdims=True)
    pm = jnp.where(lane == i1, -1.0, p)
    p2 = jnp.max(pm, axis=-1, keepdims=True)
    i2 = jnp.min(jnp.where(pm == p2, lane, LANE), axis=-1, keepdims=True)
    tot = p1 + p2
    w1, w2 = p1 / tot, p2 / tot
    hit1, hit2 = lane == i1, lane == i2
    onehot = (hit1 | hit2).astype(F32)
    r = lax.broadcasted_iota(jnp.int32, (tl, tl), 0)
    c = lax.broadcasted_iota(jnp.int32, (tl, tl), 1)
    before = _dot((r > c).astype(BF16), onehot.astype(BF16)) + carry_ref[...]
    r1 = jnp.sum(jnp.where(hit1, before, 0.0), axis=-1, keepdims=True)
    r2 = jnp.sum(jnp.where(hit2, before, 0.0), axis=-1, keepdims=True)
    carry_ref[...] += jnp.sum(onehot, axis=0, keepdims=True)
    cnt_ref[...] = jnp.broadcast_to(carry_ref[...], cnt_ref.shape)
    info = jnp.where(lane == 0, i1.astype(F32), 0.0)
    info = jnp.where(lane == 1, i2.astype(F32), info)
    info = jnp.where(lane == 2, r1, info)
    info = jnp.where(lane == 3, r2, info)
    info = jnp.where(lane == 4, w1, info)
    info = jnp.where(lane == 5, w2, info)
    info_ref[...] = info


def moe_route(logits):
    t = logits.shape[0]
    tl = min(512, t)
    return pl.pallas_call(
        _route_kernel, grid=(t // tl,),
        in_specs=[pl.BlockSpec((tl, LANE), lambda i: (i, 0))],
        out_specs=[pl.BlockSpec((tl, LANE), lambda i: (i, 0)), pl.BlockSpec((8, LANE), lambda i: (0, 0))],
        out_shape=[jax.ShapeDtypeStruct((t, LANE), F32), jax.ShapeDtypeStruct((8, LANE), F32)],
        scratch_shapes=[pltpu.VMEM((1, LANE), F32)],
        compiler_params=_cparams("arbitrary"), name="moe_route",
    )(logits)


def _dispatch_kernel(pos_ref, h_ref, xs_in_ref, xs_ref, sem):
    del xs_in_ref
    tb = pos_ref.shape[2] // TOP_K

    def copy(src_row, dst_row):
        return pltpu.make_async_copy(h_ref.at[pl.ds(pl.multiple_of(src_row * ROW_TILE, ROW_TILE), ROW_TILE)],
                                     xs_ref.at[pl.ds(pl.multiple_of(dst_row * ROW_TILE, ROW_TILE), ROW_TILE)], sem)

    def issue(j, carry):
        for k in range(TOP_K):
            copy(j, pos_ref[0, 0, TOP_K * j + k]).start()
        return carry

    lax.fori_loop(0, tb, issue, 0)

    def drain(j, carry):
        copy(0, 0).wait()
        return carry

    lax.fori_loop(0, TOP_K * tb, drain, 0)


def moe_dispatch(h_rows, pos, n_rows):
    t = pos.shape[0]
    tb = min(256, t)
    d_rows = h_rows.shape[0] // t
    assert d_rows == ROW_TILE
    zeros = jnp.zeros((n_rows * ROW_TILE, LANE), h_rows.dtype)
    return pl.pallas_call(
        _dispatch_kernel, grid=(t // tb,),
        in_specs=[pl.BlockSpec((1, 1, TOP_K * tb), lambda i: (i, 0, 0), memory_space=pltpu.SMEM),
                  pl.BlockSpec((tb * ROW_TILE, LANE), lambda i: (i, 0)),
                  pl.BlockSpec(memory_space=pl.ANY)],
        out_specs=pl.BlockSpec(memory_space=pl.ANY),
        out_shape=jax.ShapeDtypeStruct(zeros.shape, zeros.dtype),
        scratch_shapes=[pltpu.SemaphoreType.DMA],
        input_output_aliases={2: 0},
        compiler_params=_cparams("arbitrary"), name="moe_dispatch",
    )(pos.reshape(t // tb, 1, TOP_K * tb), h_rows, zeros)


def _moe_up_kernel(te_ref, tv_ref, xs_ref, wg_ref, wu_ref, o_ref, a_ref):
    del te_ref
    i = pl.program_id(0)
    tm = a_ref.shape[0]

    @pl.when(pl.program_id(1) == 0)
    def _():
        for c in range(ROW_TILE):
            lo, hi = _load_token_rows(xs_ref, tm, c)
            a_ref[:, c * LANE:(c + 1) * LANE] = lo.astype(a_ref.dtype)
            a_ref[:, (c + ROW_TILE) * LANE:(c + ROW_TILE + 1) * LANE] = hi.astype(a_ref.dtype)

    @pl.when(tv_ref[i] != 0)
    def _():
        a = a_ref[...]
        g = _dot(a, wg_ref[0].astype(BF16))
        u = _dot(a, wu_ref[0].astype(BF16))
        o_ref[...] = (_silu(g) * u).astype(o_ref.dtype)

    @pl.when(tv_ref[i] == 0)
    def _():
        o_ref[...] = jnp.zeros_like(o_ref)


def moe_up(xs_rows, wg, wu, tile_expert, tile_valid, tm, tn):
    r = xs_rows.shape[0] // ROW_TILE
    d, n = wg.shape[1], wg.shape[2]
    tn = _tile(n, tn)
    grid_spec = pltpu.PrefetchScalarGridSpec(
        num_scalar_prefetch=2, grid=(r // tm, n // tn),
        in_specs=[pl.BlockSpec((tm * ROW_TILE, LANE), lambda i, j, te, tv: (i, 0)),
                  pl.BlockSpec((1, d, tn), lambda i, j, te, tv: (te[i], 0, j)),
                  pl.BlockSpec((1, d, tn), lambda i, j, te, tv: (te[i], 0, j))],
        out_specs=pl.BlockSpec((tm, tn), lambda i, j, te, tv: (i, j)),
        scratch_shapes=[pltpu.VMEM((tm, d), BF16)])
    return pl.pallas_call(
        _moe_up_kernel, grid_spec=grid_spec,
        out_shape=jax.ShapeDtypeStruct((r, n), BF16),
        compiler_params=_cparams("parallel", "arbitrary"), name="moe_up",
    )(tile_expert, tile_valid, xs_rows, wg, wu)


def _moe_down_kernel(te_ref, tv_ref, a_ref, w_ref, o_ref, acc_ref):
    del te_ref
    i = pl.program_id(0)
    k = pl.program_id(1)

    valid = tv_ref[i] != 0

    @pl.when(jnp.logical_and(valid, k == 0))
    def _():
        acc_ref[...] = _dot(a_ref[...], w_ref[0].astype(BF16))

    @pl.when(jnp.logical_and(valid, k > 0))
    def _():
        acc_ref[...] += _dot(a_ref[...], w_ref[0].astype(BF16))

    @pl.when(jnp.logical_and(jnp.logical_not(valid), k == 0))
    def _():
        acc_ref[...] = jnp.zeros_like(acc_ref)

    @pl.when(k == pl.num_programs(1) - 1)
    def _():
        _store_token_rows(o_ref, acc_ref[...])


def moe_down(hid, wd, tile_expert, tile_valid, tm, tk):
    r, kdim = hid.shape
    n = wd.shape[2]
    assert n == D_TOKEN
    tk = _tile(kdim, tk)
    grid_spec = pltpu.PrefetchScalarGridSpec(
        num_scalar_prefetch=2, grid=(r // tm, kdim // tk),
        in_specs=[pl.BlockSpec((tm, tk), lambda i, k, te, tv: (i, k)),
                  pl.BlockSpec((1, tk, n), lambda i, k, te, tv: (te[i], k, 0))],
        out_specs=pl.BlockSpec((tm * ROW_TILE, LANE), lambda i, k, te, tv: (i, 0)),
        scratch_shapes=[pltpu.VMEM((tm, n), F32)])
    return pl.pallas_call(
        _moe_down_kernel, grid_spec=grid_spec,
        out_shape=jax.ShapeDtypeStruct((r * ROW_TILE, LANE), U32),
        compiler_params=_cparams("parallel", "arbitrary"), name="moe_down",
    )(tile_expert, tile_valid, hid, wd)


def _combine_kernel(pos_ref, ys_ref, x_ref, gate_ref, wt_ref, ng_ref, o_ref, buf_ref, sem, *, final_norm):
    tb = x_ref.shape[0]

    def copy(k, j, src_row):
        return pltpu.make_async_copy(ys_ref.at[pl.ds(pl.multiple_of(src_row * ROW_TILE, ROW_TILE), ROW_TILE)],
                                     buf_ref.at[k, pl.ds(pl.multiple_of(j * ROW_TILE, ROW_TILE), ROW_TILE)], sem)

    def issue(j, carry):
        for k in range(TOP_K):
            copy(k, j, pos_ref[0, 0, TOP_K * j + k]).start()
        return carry

    lax.fori_loop(0, tb, issue, 0)

    def drain(j, carry):
        copy(0, 0, 0).wait()
        return carry

    lax.fori_loop(0, TOP_K * tb, drain, 0)
    w0 = wt_ref[:, 0:1]
    w1 = wt_ref[:, 1:2]
    sq = jnp.zeros((tb, 1), F32)
    for c in range(ROW_TILE):
        first = _load_token_rows(buf_ref.at[0], tb, c)
        second = _load_token_rows(buf_ref.at[1], tb, c)
        for half in range(2):
            cols = slice((c + half * ROW_TILE) * LANE, (c + half * ROW_TILE + 1) * LANE)
            f = first[half] * w0 + second[half] * w1
            y = x_ref[:, cols] + gate_ref[0, :, cols] * f
            o_ref[:, cols] = y
            sq = sq + jnp.sum(y * y, axis=-1, keepdims=True)
    if final_norm:
        o_ref[...] = o_ref[...] * lax.rsqrt(sq * (1.0 / D_TOKEN) + NORM_EPS) * ng_ref[...]


def moe_combine(ys_rows, pos, wts, x, gate, rows_per_batch, final_norm_g=None):
    t, d = x.shape
    tb = min(128, t)
    bsz = gate.shape[0]
    final_norm = final_norm_g is not None
    ng = (final_norm_g if final_norm else jnp.ones((d,), F32)).reshape(1, d)
    return pl.pallas_call(
        functools.partial(_combine_kernel, final_norm=final_norm), grid=(t // tb,),
        in_specs=[pl.BlockSpec((1, 1, TOP_K * tb), lambda i: (i, 0, 0), memory_space=pltpu.SMEM),
                  pl.BlockSpec(memory_space=pl.ANY),
                  pl.BlockSpec((tb, d), lambda i: (i, 0)),
                  pl.BlockSpec((1, 1, d), lambda i: ((i * tb) // rows_per_batch, 0, 0)),
                  pl.BlockSpec((tb, TOP_K), lambda i: (i, 0)),
                  pl.BlockSpec((1, d), lambda i: (0, 0))],
        out_specs=pl.BlockSpec((tb, d), lambda i: (i, 0)),
        out_shape=jax.ShapeDtypeStruct((t, d), F32),
        scratch_shapes=[pltpu.VMEM((TOP_K, tb * ROW_TILE, LANE), U32), pltpu.SemaphoreType.DMA],
        compiler_params=_cparams("arbitrary"), name="moe_combine",
    )(pos.reshape(t // tb, 1, TOP_K * tb), ys_rows, x, gate.reshape(bsz, 1, d), wts, ng)


MOE_TM = 1024


def moe_ffn(h_rows, logits, wg, wu, wd, x, gate, rows_per_batch, final_norm_g=None):
    t, d = x.shape
    tm = min(MOE_TM, t)
    info, counts = moe_route(logits)
    sizes = counts[0, :N_EXPERTS].astype(jnp.int32)
    padded = ((sizes + tm - 1) // tm) * tm
    ends = jnp.cumsum(padded)
    starts = ends - padded
    n_tiles = (t * TOP_K) // tm + N_EXPERTS
    n_rows = n_tiles * tm
    experts = info[:, 0:TOP_K].astype(jnp.int32)
    pos = starts[experts] + info[:, 2:2 + TOP_K].astype(jnp.int32)
    wts = info[:, 4:4 + TOP_K]
    tile_start = jnp.arange(n_tiles, dtype=jnp.int32) * tm
    tile_expert = jnp.minimum(jnp.sum((tile_start[:, None] >= ends[None, :]).astype(jnp.int32), axis=1), N_EXPERTS - 1)
    tile_valid = (tile_start < ends[-1]).astype(jnp.int32)
    xs_rows = moe_dispatch(h_rows, pos, n_rows)
    hid = moe_up(xs_rows, wg, wu, tile_expert, tile_valid, tm, 512)
    ys_rows = moe_down(hid, wd.astype(BF16), tile_expert, tile_valid, tm, 1792)
    return moe_combine(ys_rows, pos, wts, x, gate, rows_per_batch, final_norm_g)


def _rot_cols(w):
    f = ROPE_AXIS_FREQS
    return jnp.concatenate([-w[:, f:2 * f], w[:, 0:f], -w[:, 3 * f:4 * f], w[:, 2 * f:3 * f]], axis=1)


def _arrange_w_in(w):
    d = w.shape[0]
    a = w[:, 0:2 * A_WIDTH]
    off = 2 * A_WIDTH
    qkv_gate = w[:, off:off + 4 * B_WIDTH]
    logit = w[:, off + 4 * B_WIDTH:off + 4 * B_WIDTH + 4 * B_HEADS]
    off = off + 4 * B_WIDTH + 4 * B_HEADS
    lat = w[:, off:off + Q_LORA + KV_LORA]
    k_pe = w[:, off + Q_LORA + KV_LORA:off + Q_LORA + KV_LORA + C_ROPE]
    z64 = jnp.zeros((d, LANE - C_ROPE), w.dtype)
    main = jnp.concatenate([qkv_gate, a, lat, k_pe, z64, _rot_cols(k_pe), z64], axis=1).astype(BF16)
    logit = jnp.pad(logit, ((0, 0), (0, LANE - 4 * B_HEADS))).astype(BF16)
    return main, logit


def _arrange_w_uq(w):
    k = w.shape[0]
    w = w.reshape(k, C_HEADS, C_NOPE + C_ROPE)
    z64 = jnp.zeros((k, C_HEADS, LANE - C_ROPE), w.dtype)
    pe = w[:, :, C_NOPE:]
    pe_rot = jnp.stack([_rot_cols(pe[:, h]) for h in range(C_HEADS)], axis=1)
    return jnp.concatenate([w[:, :, :C_NOPE], pe, z64, pe_rot, z64], axis=2).reshape(k, C_HEADS * C_HEAD_COLS).astype(BF16)


def _rope_tables(n):
    rows = n // GRID_W
    row = np.repeat(np.arange(rows, dtype=np.float32), GRID_W)
    col = np.tile(np.arange(GRID_W, dtype=np.float32), rows)
    inv = np.power(np.float32(ROPE_BASE), -np.arange(ROPE_AXIS_FREQS, dtype=np.float32) / np.float32(ROPE_AXIS_FREQS))
    ar = row[:, None] * inv.astype(np.float32)
    ac = col[:, None] * inv.astype(np.float32)
    ang = np.concatenate([ar, ar, ac, ac], axis=-1).astype(np.float32)
    pad = ((0, 0), (0, LANE - C_ROPE))
    return (jnp.asarray(np.pad(np.cos(ang).astype(np.float32), pad)),
            jnp.asarray(np.pad(np.sin(ang).astype(np.float32), pad)))


def _mixer_branches(z, zg, p, cos2, sin2):
    out_a = chunk_sgu(z, p["sgu_norm_g"], p["sgu_w"], p["sgu_b"])
    qkv, gb = deltanet_prep(z, zg, p["dn_conv_w"], p["dn_a_log"], p["dn_dt_bias"])
    intra = deltanet_intra(qkv, gb)
    q, k, v = mla_prep(z, p["mla_q_norm_g"], p["mla_kv_norm_g"], p["wq_arr"], p["wkv_arr"], cos2, sin2)
    return out_a, intra, (q, k, v)


def kernel(x, c, ctx, c_ctx, ada_w, ada_b, norm1_g, norm2_g, w_in, sgu_norm_g, sgu_w, sgu_b, dn_conv_w, dn_a_log, dn_dt_bias, dn_norm_g, mla_q_norm_g, mla_w_uq, mla_kv_norm_g, mla_w_ukv, w_out, ffn_w_gate, ffn_w_up, ffn_w_down, moe_router, moe_w_gate, moe_w_up, moe_w_down, final_norm_g):
    b, n, d = x.shape
    lc = ctx.shape[1]
    depth = ada_w.shape[0]
    assert d == D_TOKEN
    cos_lat, sin_lat = _rope_tables(n)
    cos_ctx = jnp.pad(jnp.ones((lc, C_ROPE), F32), ((0, 0), (0, LANE - C_ROPE)))
    sin_ctx = jnp.zeros((lc, LANE), F32)

    cc = jnp.zeros((8, d), F32).at[0:b].set(c).at[b].set(c_ctx)
    mod_all = ada_modulation(cc, ada_w, ada_b)

    xc = ctx
    x_rows = None
    for i in range(depth):
        last = i == depth - 1
        mod = mod_all[i, 0:b].reshape(b, 6, d)
        mod_c = jnp.broadcast_to(mod_all[i, b].reshape(1, 6, d), (b, 6, d))
        w_main, w_logit = _arrange_w_in(w_in[i])
        p = dict(sgu_norm_g=sgu_norm_g[i], sgu_w=sgu_w[i], sgu_b=sgu_b[i], dn_conv_w=dn_conv_w[i],
                 dn_a_log=dn_a_log[i], dn_dt_bias=dn_dt_bias[i], mla_q_norm_g=mla_q_norm_g[i],
                 mla_kv_norm_g=mla_kv_norm_g[i], wq_arr=_arrange_w_uq(mla_w_uq[i]),
                 wkv_arr=mla_w_ukv[i].astype(BF16))
        w_out_b = w_out[i].astype(BF16)

        h = norm_mod(x, norm1_g[i], mod[:, 0], mod[:, 1]).reshape(b * n, d)
        hc = norm_mod(xc, norm1_g[i], mod_c[:, 0], mod_c[:, 1]).reshape(b * lc, d)
        z = matmul(h, w_main, BF16, 1024, 1024).reshape(b, n, Z_COLS)
        zg = matmul(h, w_logit, F32, 1024, LANE).reshape(b, n, LANE)
        zc = matmul(hc, w_main, BF16, 1024, 1024).reshape(b, lc, Z_COLS)
        zgc = matmul(hc, w_logit, F32, 1024, LANE).reshape(b, lc, LANE)

        oa_c, intra_c, (q_c, k_c, v_c) = _mixer_branches(zc, zgc, p, cos_ctx, sin_ctx)
        out_a, intra, (q_l, k_l, v_l) = _mixer_branches(z, zg, p, cos_lat, sin_lat)
        s_zero = jnp.zeros((b, 2 * B_HEADS, B_HEAD_DIM, B_HEAD_DIM), F32)
        ocf, ocb, s_ctx = deltanet_scan(*intra_c, s_zero)
        o_f, o_b, _ = deltanet_scan(*intra, s_ctx)
        out_b = deltanet_out(o_f, o_b, z, dn_norm_g[i])
        out_c = flash_attention(q_l, jnp.concatenate([k_l, k_c], axis=2), jnp.concatenate([v_l, v_c], axis=2))
        x2 = mix_out_residual(out_a.reshape(b * n, -1), out_b.reshape(b * n, -1), out_c.reshape(b * n, -1),
                              w_out_b, x.reshape(b * n, d), mod[:, 2], n, 1024, 1024)
        if not last:
            ob_c = deltanet_out(ocf, ocb, zc, dn_norm_g[i])
            oc_c = flash_attention(q_c, k_c, v_c)
            xc2 = mix_out_residual(oa_c.reshape(b * lc, -1), ob_c.reshape(b * lc, -1), oc_c.reshape(b * lc, -1),
                                   w_out_b, xc.reshape(b * lc, d), mod_c[:, 2], lc, 1024, 1024)

        if i % 2 == 0:
            wg, wu, wd = (ffn_w_gate[i // 2].astype(BF16), ffn_w_up[i // 2].astype(BF16), ffn_w_down[i // 2].astype(BF16))
            h2 = norm_mod(x2.reshape(b, n, d), norm2_g[i], mod[:, 3], mod[:, 4]).reshape(b * n, d)
            hid = swiglu_up(h2, wg, wu, 1024, 512)
            x = down_residual(hid, wd, x2, mod[:, 5], n, 1024, 1024, 1408).reshape(b, n, d)
            if not last:
                hc2 = norm_mod(xc2.reshape(b, lc, d), norm2_g[i], mod_c[:, 3], mod_c[:, 4]).reshape(b * lc, d)
                hid_c = swiglu_up(hc2, wg, wu, 1024, 512)
                xc = down_residual(hid_c, wd, xc2, mod_c[:, 5], lc, 1024, 1024, 1408).reshape(b, lc, d)
        else:
            e = i // 2
            router = jnp.pad(moe_router[e], ((0, 0), (0, LANE - N_EXPERTS)))
            if not last:
                raise NotImplementedError("an expert layer followed by another layer is not part of this model")
            h_rows, logits = norm_mod(x2.reshape(b, n, d), norm2_g[i], mod[:, 3], mod[:, 4], router=router)
            return moe_ffn(h_rows, logits.reshape(b * n, LANE), moe_w_gate[e], moe_w_up[e], moe_w_down[e],
                           x2, mod[:, 5], n, final_norm_g).reshape(b, n, d)
    return rmsnorm_rows(x.reshape(b * n, d), final_norm_g).reshape(b, n, d)
```

```python
import functools
import math

import jax
import jax.numpy as jnp
import numpy as np
from jax import lax
from jax.experimental import pallas as pl
from jax.experimental.pallas import tpu as pltpu

F32 = jnp.float32
BF16 = jnp.bfloat16
HIGHEST = lax.Precision.HIGHEST

NORM_EPS = 1e-6
GRID_W = 64
A_GROUPS = 4
A_GROUP_DIM = 128
A_WIDTH = A_GROUPS * A_GROUP_DIM
A_CHUNK = 128
B_HEADS = 6
B_HEAD_DIM = 128
B_WIDTH = B_HEADS * B_HEAD_DIM
DN_CHUNK = 64
DN_CONV = 5
C_HEADS = 6
C_NOPE = 128
C_ROPE = 64
C_V = 128
C_WIDTH = C_HEADS * C_V
Q_LORA = 512
KV_LORA = 256
ROPE_BASE = 10000.0
ROPE_AXIS_FREQS = C_ROPE // 4
N_EXPERTS = 8
TOP_K = 2

LANE = 128
ROW_TILE = 8
D_TOKEN = 2 * ROW_TILE * LANE
U32 = jnp.uint32
VMEM_LIMIT = 56 * 1024 * 1024

ZC_QKV = 0
ZC_GATE = 3 * B_WIDTH
ZC_A = 4 * B_WIDTH
ZC_C = 4 * B_WIDTH + 2 * A_WIDTH
Z_COLS = ZC_C + 1024
C_HEAD_COLS = 3 * LANE


def _cparams(*sem):
    return pltpu.CompilerParams(dimension_semantics=sem, vmem_limit_bytes=VMEM_LIMIT)


def _tile(n, pref):
    if n <= pref:
        return n
    t = (pref // LANE) * LANE
    while n % t:
        t -= LANE
    return t


def _sigmoid(x):
    return 1.0 / (1.0 + jnp.exp(-x))


def _silu(x):
    return x * _sigmoid(x)


def _dot(a, b):
    return jnp.dot(a, b, preferred_element_type=F32)


def _dot_nt(a, b):
    return lax.dot_general(a, b, (((1,), (1,)), ((), ())), preferred_element_type=F32)


def _dot_tn(a, b):
    return lax.dot_general(a, b, (((0,), (0,)), ((), ())), preferred_element_type=F32)


def _split_bf16(a):
    hi = a.astype(BF16)
    lo = (a - hi.astype(F32)).astype(BF16)
    return hi, lo


def _dot_x3(a, b):
    ah, al = _split_bf16(a)
    bh, bl = _split_bf16(b)
    return _dot(ah, bh) + (_dot(ah, bl) + _dot(al, bh))


def _ada_kernel(c_ref, w_ref, b_ref, o_ref):
    c = c_ref[...]
    o_ref[0] = jnp.dot(_silu(c), w_ref[0], preferred_element_type=F32, precision=HIGHEST) + b_ref[0]


def ada_modulation(cc, ada_w, ada_b):
    depth, d, n = ada_w.shape
    tn = 1024
    return pl.pallas_call(
        _ada_kernel,
        grid=(depth, n // tn),
        in_specs=[
            pl.BlockSpec((8, d), lambda i, j: (0, 0)),
            pl.BlockSpec((1, d, tn), lambda i, j: (i, 0, j)),
            pl.BlockSpec((1, 1, tn), lambda i, j: (i, 0, j)),
        ],
        out_specs=pl.BlockSpec((1, 8, tn), lambda i, j: (i, 0, j)),
        out_shape=jax.ShapeDtypeStruct((depth, 8, n), F32),
        compiler_params=_cparams("parallel", "parallel"),
        name="ada_modulation",
    )(cc, ada_w, ada_b.reshape(depth, 1, n))


def _norm_mod_kernel(x_ref, g_ref, sh_ref, sc_ref, o_ref):
    x = x_ref[0]
    y = x * lax.rsqrt(jnp.mean(x * x, axis=-1, keepdims=True) + NORM_EPS) * g_ref[...]
    o_ref[0] = (y * (1.0 + sc_ref[0]) + sh_ref[0]).astype(o_ref.dtype)


def _norm_mod_router_kernel(x_ref, g_ref, sh_ref, sc_ref, r_ref, o_ref, lg_ref):
    x = x_ref[0]
    y = x * lax.rsqrt(jnp.mean(x * x, axis=-1, keepdims=True) + NORM_EPS) * g_ref[...]
    h = y * (1.0 + sc_ref[0]) + sh_ref[0]
    _store_token_rows(o_ref, h)
    lg_ref[0] = jnp.dot(h, r_ref[...], preferred_element_type=F32, precision=HIGHEST)


def _store_token_rows(rows_ref, x):
    n = x.shape[0]
    for c in range(ROW_TILE):
        lo = lax.bitcast_convert_type(x[:, c * LANE:(c + 1) * LANE].astype(BF16).astype(F32), U32)
        hi = lax.bitcast_convert_type(x[:, (c + ROW_TILE) * LANE:(c + ROW_TILE + 1) * LANE].astype(BF16).astype(F32), U32)
        rows_ref[pl.ds(c, n, stride=ROW_TILE), :] = lax.shift_right_logical(lo, jnp.uint32(16)) | (hi & jnp.uint32(0xFFFF0000))


def _load_token_rows(rows_ref, n, c):
    w = rows_ref[pl.ds(c, n, stride=ROW_TILE), :]
    lo = lax.bitcast_convert_type(lax.shift_left(w, jnp.uint32(16)), F32)
    hi = lax.bitcast_convert_type(w & jnp.uint32(0xFFFF0000), F32)
    return lo, hi


def norm_mod(x, g, shift, scale, out_dtype=BF16, router=None):
    b, l, d = x.shape
    tl = min(512, l)
    in_specs = [
        pl.BlockSpec((1, tl, d), lambda i, j: (i, j, 0)),
        pl.BlockSpec((1, d), lambda i, j: (0, 0)),
        pl.BlockSpec((1, 1, d), lambda i, j: (i, 0, 0)),
        pl.BlockSpec((1, 1, d), lambda i, j: (i, 0, 0)),
    ]
    args = [x, g.reshape(1, d), shift.reshape(b, 1, d), scale.reshape(b, 1, d)]
    h_spec = pl.BlockSpec((1, tl, d), lambda i, j: (i, j, 0))
    h_shape = jax.ShapeDtypeStruct((b, l, d), out_dtype)
    if router is None:
        return pl.pallas_call(
            _norm_mod_kernel, grid=(b, l // tl), in_specs=in_specs, out_specs=h_spec, out_shape=h_shape,
            compiler_params=_cparams("parallel", "parallel"), name="norm_mod",
        )(*args)
    assert d == D_TOKEN
    in_specs.append(pl.BlockSpec((d, LANE), lambda i, j: (0, 0)))
    nl = l // tl
    return pl.pallas_call(
        _norm_mod_router_kernel, grid=(b, nl), in_specs=in_specs,
        out_specs=[pl.BlockSpec((tl * ROW_TILE, LANE), lambda i, j: (i * nl + j, 0)),
                   pl.BlockSpec((1, tl, LANE), lambda i, j: (i, j, 0))],
        out_shape=[jax.ShapeDtypeStruct((b * l * ROW_TILE, LANE), U32), jax.ShapeDtypeStruct((b, l, LANE), F32)],
        compiler_params=_cparams("parallel", "parallel"), name="norm_mod_router",
    )(*args, router)


def _rmsnorm_kernel(x_ref, g_ref, o_ref):
    x = x_ref[...]
    o_ref[...] = x * lax.rsqrt(jnp.mean(x * x, axis=-1, keepdims=True) + NORM_EPS) * g_ref[...]


def rmsnorm_rows(x, g):
    m, d = x.shape
    tm = min(512, m)
    return pl.pallas_call(
        _rmsnorm_kernel, grid=(m // tm,),
        in_specs=[pl.BlockSpec((tm, d), lambda i: (i, 0)), pl.BlockSpec((1, d), lambda i: (0, 0))],
        out_specs=pl.BlockSpec((tm, d), lambda i: (i, 0)),
        out_shape=jax.ShapeDtypeStruct((m, d), F32),
        compiler_params=_cparams("parallel"), name="final_rmsnorm",
    )(x, g.reshape(1, d))


def _mm_kernel(a_ref, w_ref, o_ref):
    o_ref[...] = _dot(a_ref[...], w_ref[...]).astype(o_ref.dtype)


def matmul(a, w, out_dtype, tm, tn):
    m, k = a.shape
    n = w.shape[1]
    tm, tn = _tile(m, tm), _tile(n, tn)
    return pl.pallas_call(
        _mm_kernel, grid=(m // tm, n // tn),
        in_specs=[pl.BlockSpec((tm, k), lambda i, j: (i, 0)), pl.BlockSpec((k, tn), lambda i, j: (0, j))],
        out_specs=pl.BlockSpec((tm, tn), lambda i, j: (i, j)),
        out_shape=jax.ShapeDtypeStruct((m, n), out_dtype),
        compiler_params=_cparams("parallel", "parallel"), name="matmul",
    )(a, w)


def _swiglu_up_kernel(a_ref, wg_ref, wu_ref, o_ref):
    a = a_ref[...]
    g = _dot(a, wg_ref[...])
    u = _dot(a, wu_ref[...])
    o_ref[...] = (_silu(g) * u).astype(o_ref.dtype)


def swiglu_up(a, wg, wu, tm, tn):
    m, k = a.shape
    n = wg.shape[1]
    tm, tn = _tile(m, tm), _tile(n, tn)
    return pl.pallas_call(
        _swiglu_up_kernel, grid=(m // tm, n // tn),
        in_specs=[pl.BlockSpec((tm, k), lambda i, j: (i, 0)),
                  pl.BlockSpec((k, tn), lambda i, j: (0, j)),
                  pl.BlockSpec((k, tn), lambda i, j: (0, j))],
        out_specs=pl.BlockSpec((tm, tn), lambda i, j: (i, j)),
        out_shape=jax.ShapeDtypeStruct((m, n), BF16),
        compiler_params=_cparams("parallel", "parallel"), name="swiglu_up",
    )(a, wg, wu)


def _down_res_kernel(a_ref, w_ref, x_ref, gate_ref, o_ref, acc_ref):
    k = pl.program_id(2)

    @pl.when(k == 0)
    def _():
        acc_ref[...] = _dot(a_ref[...], w_ref[...])

    @pl.when(k > 0)
    def _():
        acc_ref[...] += _dot(a_ref[...], w_ref[...])

    @pl.when(k == pl.num_programs(2) - 1)
    def _():
        o_ref[...] = x_ref[...] + gate_ref[0] * acc_ref[...]


def down_residual(a, w, x, gate, rows_per_batch, tm, tn, tk):
    m, kdim = a.shape
    n = w.shape[1]
    tm, tn, tk = _tile(rows_per_batch, tm), _tile(n, tn), _tile(kdim, tk)
    bsz = gate.shape[0]
    return pl.pallas_call(
        _down_res_kernel, grid=(m // tm, n // tn, kdim // tk),
        in_specs=[pl.BlockSpec((tm, tk), lambda i, j, k: (i, k)),
                  pl.BlockSpec((tk, tn), lambda i, j, k: (k, j)),
                  pl.BlockSpec((tm, tn), lambda i, j, k: (i, j)),
                  pl.BlockSpec((1, 1, tn), lambda i, j, k: ((i * tm) // rows_per_batch, 0, j))],
        out_specs=pl.BlockSpec((tm, tn), lambda i, j, k: (i, j)),
        out_shape=jax.ShapeDtypeStruct((m, n), F32),
        scratch_shapes=[pltpu.VMEM((tm, tn), F32)],
        compiler_params=_cparams("parallel", "parallel", "arbitrary"), name="down_residual",
    )(a, w, x, gate.reshape(bsz, 1, n))


def _mix_out_kernel(a_ref, b_ref, c_ref, w_ref, x_ref, gate_ref, o_ref):
    ka, kb = a_ref.shape[1], b_ref.shape[1]
    acc = _dot(a_ref[...], w_ref[0:ka, :])
    acc += _dot(b_ref[...], w_ref[ka:ka + kb, :])
    acc += _dot(c_ref[...], w_ref[ka + kb:, :])
    o_ref[...] = x_ref[...] + gate_ref[0] * acc


def mix_out_residual(oa, ob, oc, w, x, gate, rows_per_batch, tm, tn):
    m = oa.shape[0]
    kdim, n = w.shape
    tm, tn = _tile(rows_per_batch, tm), _tile(n, tn)
    bsz = gate.shape[0]
    return pl.pallas_call(
        _mix_out_kernel, grid=(m // tm, n // tn),
        in_specs=[pl.BlockSpec((tm, oa.shape[1]), lambda i, j: (i, 0)),
                  pl.BlockSpec((tm, ob.shape[1]), lambda i, j: (i, 0)),
                  pl.BlockSpec((tm, oc.shape[1]), lambda i, j: (i, 0)),
                  pl.BlockSpec((kdim, tn), lambda i, j: (0, j)),
                  pl.BlockSpec((tm, tn), lambda i, j: (i, j)),
                  pl.BlockSpec((1, 1, tn), lambda i, j: ((i * tm) // rows_per_batch, 0, j))],
        out_specs=pl.BlockSpec((tm, tn), lambda i, j: (i, j)),
        out_shape=jax.ShapeDtypeStruct((m, n), F32),
        compiler_params=_cparams("parallel", "parallel"), name="mix_out_residual",
    )(oa, ob, oc, w, x, gate.reshape(bsz, 1, n))


def _gelu_tanh(x):
    return 0.5 * x * (1.0 + jnp.tanh(math.sqrt(2.0 / math.pi) * (x + 0.044715 * (x * x * x))))


def _sgu_kernel(z_ref, g_ref, w_ref, b_ref, o_ref):
    tl = z_ref.shape[1]
    for c in range(tl // A_CHUNK):
        rows = slice(c * A_CHUNK, (c + 1) * A_CHUNK)
        for g in range(A_GROUPS):
            cols = slice(g * A_GROUP_DIM, (g + 1) * A_GROUP_DIM)
            u = _gelu_tanh(z_ref[0, rows, cols].astype(F32))
            v = _gelu_tanh(z_ref[0, rows, A_WIDTH + g * A_GROUP_DIM:A_WIDTH + (g + 1) * A_GROUP_DIM].astype(F32))
            vc = v - jnp.mean(v, axis=-1, keepdims=True)
            vn = vc * lax.rsqrt(jnp.mean(vc * vc, axis=-1, keepdims=True) + NORM_EPS) * g_ref[:, cols]
            mixed = _dot(w_ref[g], vn.astype(BF16)) + b_ref[g]
            o_ref[0, rows, cols] = (u * mixed).astype(o_ref.dtype)


def chunk_sgu(z, norm_g, w_s, b_s):
    b, l, _ = z.shape
    tl = min(512, l)
    blk = ZC_A // (2 * A_WIDTH)
    return pl.pallas_call(
        _sgu_kernel, grid=(b, l // tl),
        in_specs=[pl.BlockSpec((1, tl, 2 * A_WIDTH), lambda i, j: (i, j, blk)),
                  pl.BlockSpec((1, A_WIDTH), lambda i, j: (0, 0)),
                  pl.BlockSpec((A_GROUPS, A_CHUNK, A_CHUNK), lambda i, j: (0, 0, 0)),
                  pl.BlockSpec((A_GROUPS, A_CHUNK, 1), lambda i, j: (0, 0, 0))],
        out_specs=pl.BlockSpec((1, tl, A_WIDTH), lambda i, j: (i, j, 0)),
        out_shape=jax.ShapeDtypeStruct((b, l, A_WIDTH), BF16),
        compiler_params=_cparams("parallel", "parallel"), name="chunk_sgu",
    )(z, norm_g.reshape(1, A_WIDTH), w_s.astype(BF16), b_s.reshape(A_GROUPS, A_CHUNK, 1))


def _mla_prep_kernel(z_ref, qg_ref, kvg_ref, wq_ref, wkv_ref, cos_ref, sin_ref, q_ref, k_ref, v_ref, *, q_scale):
    lat = z_ref[0, :, 0:Q_LORA].astype(F32)
    cq = (lat * lax.rsqrt(jnp.mean(lat * lat, axis=-1, keepdims=True) + NORM_EPS) * qg_ref[...]).astype(BF16)
    kvl = z_ref[0, :, Q_LORA:Q_LORA + KV_LORA].astype(F32)
    ckv = (kvl * lax.rsqrt(jnp.mean(kvl * kvl, axis=-1, keepdims=True) + NORM_EPS) * kvg_ref[...]).astype(BF16)
    cos = cos_ref[...]
    sin = sin_ref[...]
    base = Q_LORA + KV_LORA
    k_pe = z_ref[0, :, base:base + LANE].astype(F32) * cos + z_ref[0, :, base + LANE:base + 2 * LANE].astype(F32) * sin
    k_pe = k_pe.astype(k_ref.dtype)
    for h in range(C_HEADS):
        qh = _dot(cq, wq_ref[:, h * C_HEAD_COLS:(h + 1) * C_HEAD_COLS])
        q_pe = qh[:, LANE:2 * LANE] * cos + qh[:, 2 * LANE:3 * LANE] * sin
        q_ref[0, h, :, 0:LANE] = (qh[:, 0:LANE] * q_scale).astype(q_ref.dtype)
        q_ref[0, h, :, LANE:2 * LANE] = (q_pe * q_scale).astype(q_ref.dtype)
        kv = _dot(ckv, wkv_ref[:, h * 2 * LANE:(h + 1) * 2 * LANE])
        k_ref[0, h, :, 0:LANE] = kv[:, 0:LANE].astype(k_ref.dtype)
        k_ref[0, h, :, LANE:2 * LANE] = k_pe
        v_ref[0, h] = kv[:, LANE:2 * LANE].astype(v_ref.dtype)


def mla_prep(z, q_norm_g, kv_norm_g, wq_arr, wkv_arr, cos2, sin2):
    b, l, _ = z.shape
    tl = min(256, l)
    blk = ZC_C // 1024
    q_scale = (C_NOPE + C_ROPE) ** -0.5 * math.log2(math.e)
    return pl.pallas_call(
        functools.partial(_mla_prep_kernel, q_scale=q_scale), grid=(b, l // tl),
        in_specs=[pl.BlockSpec((1, tl, 1024), lambda i, j: (i, j, blk)),
                  pl.BlockSpec((1, Q_LORA), lambda i, j: (0, 0)),
                  pl.BlockSpec((1, KV_LORA), lambda i, j: (0, 0)),
                  pl.BlockSpec((Q_LORA, C_HEADS * C_HEAD_COLS), lambda i, j: (0, 0)),
                  pl.BlockSpec((KV_LORA, C_HEADS * 2 * LANE), lambda i, j: (0, 0)),
                  pl.BlockSpec((tl, LANE), lambda i, j: (j, 0)),
                  pl.BlockSpec((tl, LANE), lambda i, j: (j, 0))],
        out_specs=[pl.BlockSpec((1, C_HEADS, tl, 2 * LANE), lambda i, j: (i, 0, j, 0)),
                   pl.BlockSpec((1, C_HEADS, tl, 2 * LANE), lambda i, j: (i, 0, j, 0)),
                   pl.BlockSpec((1, C_HEADS, tl, LANE), lambda i, j: (i, 0, j, 0))],
        out_shape=[jax.ShapeDtypeStruct((b, C_HEADS, l, 2 * LANE), BF16),
                   jax.ShapeDtypeStruct((b, C_HEADS, l, 2 * LANE), BF16),
                   jax.ShapeDtypeStruct((b, C_HEADS, l, LANE), BF16)],
        compiler_params=_cparams("parallel", "parallel"), name="mla_prep",
    )(z, q_norm_g.reshape(1, Q_LORA), kv_norm_g.reshape(1, KV_LORA), wq_arr, wkv_arr, cos2, sin2)


FLASH_ROWS = 32


def _flash_kernel(q_ref, k_ref, v_ref, o_ref, s_ref, p_ref, m_ref, l_ref, acc_ref, *, tk, nk):
    tq = q_ref.shape[2]
    rb = min(FLASH_ROWS, tq)
    m_ref[...] = jnp.full(m_ref.shape, -jnp.inf, F32)
    l_ref[...] = jnp.zeros(l_ref.shape, F32)
    acc_ref[...] = jnp.zeros(acc_ref.shape, F32)

    for j in range(nk):
        s_buf, p_buf = s_ref.at[j % 2], p_ref.at[j % 2]
        keys = slice(j * tk, (j + 1) * tk)
        s_buf[...] = _dot_nt(q_ref[0, 0], k_ref[0, 0, keys, :])
        for r in range(tq // rb):
            rows = slice(r * rb, (r + 1) * rb)
            blocks = [s_buf[rows, t * LANE:(t + 1) * LANE] for t in range(tk // LANE)]
            mx = blocks[0]
            for blk in blocks[1:]:
                mx = jnp.maximum(mx, blk)
            m_old = m_ref[rows, :]
            m_new = jnp.maximum(m_old, jnp.broadcast_to(jnp.max(mx, axis=-1, keepdims=True), (rb, LANE)))
            alpha = jnp.exp2(m_old - m_new)
            lane_sum = None
            for t, blk in enumerate(blocks):
                p = jnp.exp2(blk - m_new)
                lane_sum = p if lane_sum is None else lane_sum + p
                p_buf[rows, t * LANE:(t + 1) * LANE] = p.astype(BF16)
            l_ref[rows, :] = alpha * l_ref[rows, :] + lane_sum
            m_ref[rows, :] = m_new
            acc_ref[rows, :] = alpha * acc_ref[rows, :]
        acc_ref[...] += _dot(p_buf[...], v_ref[0, 0, keys, :])
    o_ref[0] = (acc_ref[...] / jnp.sum(l_ref[...], axis=-1, keepdims=True)).astype(o_ref.dtype)


def flash_attention(q, k, v):
    b, h, lq, dq = q.shape
    lk = k.shape[2]
    tq = min(512, lq)
    tk = _tile(lk, 256)
    return pl.pallas_call(
        functools.partial(_flash_kernel, tk=tk, nk=lk // tk), grid=(b, h, lq // tq),
        in_specs=[pl.BlockSpec((1, 1, tq, dq), lambda i, j, t: (i, j, t, 0)),
                  pl.BlockSpec((1, 1, lk, dq), lambda i, j, t: (i, j, 0, 0)),
                  pl.BlockSpec((1, 1, lk, C_V), lambda i, j, t: (i, j, 0, 0))],
        out_specs=pl.BlockSpec((1, tq, C_V), lambda i, j, t: (i, t, j)),
        out_shape=jax.ShapeDtypeStruct((b, lq, h * C_V), BF16),
        scratch_shapes=[pltpu.VMEM((2, tq, tk), F32), pltpu.VMEM((2, tq, tk), BF16), pltpu.VMEM((tq, LANE), F32),
                        pltpu.VMEM((tq, LANE), F32), pltpu.VMEM((tq, C_V), F32)],
        compiler_params=_cparams("parallel", "parallel", "parallel"), name="flash_attention",
    )(q, k, v)


def _softplus(x):
    return jnp.maximum(x, 0.0) + jnp.log1p(jnp.exp(-jnp.abs(x)))


def _dn_prep_kernel(zm_ref, zp_ref, zn_ref, zg_ref, w_ref, alog_ref, dtb_ref, qkv_ref, gb_ref):
    j = pl.program_id(1)
    tl = zm_ref.shape[1]
    half = DN_CONV // 2
    keep_prev = (j > 0).astype(F32)
    keep_next = (j < pl.num_programs(1) - 1).astype(F32)
    for c in range(3 * B_HEADS):
        cols = slice(c * LANE, (c + 1) * LANE)
        prev = zp_ref[0, :, cols].astype(F32)[8:16] * keep_prev
        nxt = zn_ref[0, :, cols].astype(F32)[0:8] * keep_next
        ext = jnp.concatenate([prev, zm_ref[0, :, cols].astype(F32), nxt], axis=0)
        y = ext[8 - half:8 - half + tl] * w_ref[0:1, cols]
        for i in range(1, DN_CONV):
            y = y + ext[8 - half + i:8 - half + i + tl] * w_ref[i:i + 1, cols]
        y = _silu(y)
        if c < 2 * B_HEADS:
            y = y * lax.rsqrt(jnp.sum(y * y, axis=-1, keepdims=True) + NORM_EPS)
        qkv_ref[0, :, cols] = y.astype(qkv_ref.dtype)
    zg = zg_ref[0]
    lane = lax.broadcasted_iota(jnp.int32, zg.shape, 1)
    g = -jnp.exp(alog_ref[...]) * _softplus(zg + dtb_ref[...])
    gb_ref[0] = jnp.where(lane < 2 * B_HEADS, g, _sigmoid(zg))


def deltanet_prep(z, zg, conv_w, a_log, dt_bias):
    b, l, _ = z.shape
    tl = min(256, l)
    wq = 3 * B_WIDTH
    nb16 = l // 16
    pad = LANE - 2 * B_HEADS
    alog = jnp.pad(a_log.reshape(1, -1), ((0, 0), (0, pad)))
    dtb = jnp.pad(dt_bias.reshape(1, -1), ((0, 0), (0, pad)))
    return pl.pallas_call(
        _dn_prep_kernel, grid=(b, l // tl),
        in_specs=[pl.BlockSpec((1, tl, wq), lambda i, j: (i, j, 0)),
                  pl.BlockSpec((1, 16, wq), lambda i, j: (i, jnp.maximum(j * (tl // 16) - 1, 0), 0)),
                  pl.BlockSpec((1, 16, wq), lambda i, j: (i, jnp.minimum((j + 1) * (tl // 16), nb16 - 1), 0)),
                  pl.BlockSpec((1, tl, LANE), lambda i, j: (i, j, 0)),
                  pl.BlockSpec((DN_CONV, wq), lambda i, j: (0, 0)),
                  pl.BlockSpec((1, LANE), lambda i, j: (0, 0)),
                  pl.BlockSpec((1, LANE), lambda i, j: (0, 0))],
        out_specs=[pl.BlockSpec((1, tl, wq), lambda i, j: (i, j, 0)),
                   pl.BlockSpec((1, tl, LANE), lambda i, j: (i, j, 0))],
        out_shape=[jax.ShapeDtypeStruct((b, l, wq), BF16), jax.ShapeDtypeStruct((b, l, LANE), F32)],
        compiler_params=_cparams("parallel", "parallel"), name="deltanet_prep",
    )(z, z, z, zg, conv_w, alog, dtb)


def _dot_bf16(a, b):
    return _dot(a.astype(BF16), b.astype(BF16))


def _dn_intra_kernel(qkv_ref, gb_ref, u_ref, w_ref, qg_ref, kg_ref, qk_ref, gl_ref, *, chunks, passes):
    c = DN_CHUNK
    row = lax.broadcasted_iota(jnp.int32, (c, LANE), 0)
    lane = lax.broadcasted_iota(jnp.int32, (c, LANE), 1)
    col = jnp.where(lane < c, lane, lane - c)
    fwd = lane < c
    bwd = jnp.logical_not(fwd)
    incl = (fwd & (row >= col)) | (bwd & (row <= col))
    strict = (fwd & (row > col)) | (bwd & (row < col))
    same16 = lax.shift_right_logical(row, 4) == lax.shift_right_logical(col, 4)
    same32 = lax.shift_right_logical(row, 5) == lax.shift_right_logical(col, 5)
    diag = row == col
    eye2 = diag.astype(F32)
    r64 = lax.broadcasted_iota(jnp.int32, (c, c), 0)
    c64 = lax.broadcasted_iota(jnp.int32, (c, c), 1)
    tri_lo = (r64 >= c64).astype(F32)
    tri_up = (r64 <= c64).astype(F32)
    scale = B_HEAD_DIM ** -0.5
    dotp = _dot_x3 if passes == 3 else _dot_bf16

    def pick(x, idx):
        return jnp.sum(jnp.where(lane == idx, x, 0.0), axis=-1, keepdims=True)

    def blockdiag(y2):
        return jnp.concatenate([jnp.where(fwd, y2, 0.0), jnp.where(fwd, 0.0, y2)], axis=0)

    def mm(xs, ys):
        return [dotp(x2, blockdiag(y2)) for x2, y2 in zip(xs, ys)]

    rows = [slice(ci * c, (ci + 1) * c) for ci in range(chunks)]
    gb = [gb_ref[0, rs, :] for rs in rows]
    cum_f = [jnp.dot(tri_lo, g, preferred_element_type=F32, precision=HIGHEST) for g in gb]
    cum_b = [jnp.dot(tri_up, g, preferred_element_type=F32, precision=HIGHEST) for g in gb]
    tot_f = [x[c - 1:c, :] for x in cum_f]
    tot_b = [x[0:1, :] for x in cum_b]

    units = [(ci, h) for ci in range(chunks) for h in range(B_HEADS)]
    q = [qkv_ref[0, rows[ci], h * LANE:(h + 1) * LANE] for ci, h in units]
    k = [qkv_ref[0, rows[ci], B_WIDTH + h * LANE:B_WIDTH + (h + 1) * LANE] for ci, h in units]
    v = [qkv_ref[0, rows[ci], 2 * B_WIDTH + h * LANE:2 * B_WIDTH + (h + 1) * LANE].astype(F32) for ci, h in units]
    k2 = [jnp.concatenate([x, x], axis=0) for x in k]
    kk2 = [_dot_nt(x, y) for x, y in zip(k, k2)]
    qk2 = [_dot_nt(x, y) for x, y in zip(q, k2)]
    cf = [pick(cum_f[ci], h) for ci, h in units]
    cb = [pick(cum_b[ci], B_HEADS + h) for ci, h in units]
    bf = [pick(gb[ci], 2 * B_HEADS + h) for ci, h in units]
    bb = [pick(gb[ci], 3 * B_HEADS + h) for ci, h in units]
    lf = [pick(jnp.broadcast_to(tot_f[ci], (c, LANE)), h) for ci, h in units]
    lb = [pick(jnp.broadcast_to(tot_b[ci], (c, LANE)), B_HEADS + h) for ci, h in units]
    c2 = [jnp.where(fwd, x, y) for x, y in zip(cf, cb)]
    r2 = [jnp.sum(jnp.where(diag, x, 0.0), axis=0, keepdims=True) for x in c2]
    decay2 = [jnp.where(incl, jnp.exp(jnp.where(incl, x - y, 0.0)), 0.0) for x, y in zip(c2, r2)]
    l2 = [jnp.where(strict, jnp.where(fwd, x, y) * kk * dc, 0.0) for x, y, kk, dc in zip(bf, bb, kk2, decay2)]
    mp = [jnp.where(same16, -x, 0.0) for x in l2]
    p = [eye2 + x for x in mp]
    for _ in range(3):
        mp = mm(mp, mp)
        p = [x + y for x, y in zip(p, mm(p, mp))]
    off = [jnp.where(same32 & jnp.logical_not(same16), x, 0.0) for x in l2]
    p = [x - y for x, y in zip(p, mm(mm(p, off), p))]
    off = [jnp.where(same32, 0.0, x) for x in l2]
    p = [x - y for x, y in zip(p, mm(mm(p, off), p))]
    ef = [jnp.exp(x) for x in cf]
    eb = [jnp.exp(x) for x in cb]
    zero = jnp.zeros((c, LANE), F32)
    sol = []
    for i in range(len(units)):
        kf = k[i].astype(F32)
        rhs = jnp.concatenate([
            jnp.concatenate([v[i] * bf[i], kf * (bf[i] * ef[i]), zero, zero], axis=1),
            jnp.concatenate([zero, zero, v[i] * bb[i], kf * (bb[i] * eb[i])], axis=1)], axis=0)
        sol.append(dotp(p[i], rhs))
    for i, (ci, h) in enumerate(units):
        hc = slice(h * LANE, (h + 1) * LANE)
        rs = rows[ci]
        kf = k[i].astype(F32)
        qf = q[i].astype(F32)
        u_ref[0, 0, rs, hc] = sol[i][:, 0:LANE]
        w_ref[0, 0, rs, hc] = sol[i][:, LANE:2 * LANE].astype(w_ref.dtype)
        u_ref[1, 0, rs, hc] = sol[i][:, 2 * LANE:3 * LANE]
        w_ref[1, 0, rs, hc] = sol[i][:, 3 * LANE:4 * LANE].astype(w_ref.dtype)
        qg_ref[0, 0, rs, hc] = (qf * (ef[i] * scale)).astype(qg_ref.dtype)
        qg_ref[1, 0, rs, hc] = (qf * (eb[i] * scale)).astype(qg_ref.dtype)
        kg_ref[0, 0, rs, hc] = (kf * jnp.exp(lf[i] - cf[i])).astype(kg_ref.dtype)
        kg_ref[1, 0, rs, hc] = (kf * jnp.exp(lb[i] - cb[i])).astype(kg_ref.dtype)
        qk_ref[0, rs, hc] = (qk2[i] * decay2[i] * scale).astype(qk_ref.dtype)
    r8 = lax.broadcasted_iota(jnp.int32, (2 * 8, LANE), 0)
    l8 = lax.broadcasted_iota(jnp.int32, (2 * 8, LANE), 1)
    want = jnp.where(r8 < 8, r8, r8 - 8 + B_HEADS)
    for ci in range(chunks):
        src = jnp.concatenate([jnp.broadcast_to(tot_f[ci], (8, LANE)), jnp.broadcast_to(tot_b[ci], (8, LANE))], axis=0)
        tot = jnp.sum(jnp.where(l8 == want, src, 0.0), axis=-1, keepdims=True)
        gl_ref[0, ci] = jnp.broadcast_to(jnp.exp(tot), (2 * 8, LANE))


DN_INTRA_CHUNKS = 4
DN_INTRA_PASSES = 1
DN_SCAN_CHUNKS = 4


def deltanet_intra(qkv, gb):
    b, l, _ = qkv.shape
    nc = l // DN_CHUNK
    chunks = math.gcd(DN_INTRA_CHUNKS, nc)
    tl = chunks * DN_CHUNK
    dir_spec = pl.BlockSpec((2, 1, tl, B_WIDTH), lambda i, j: (0, i, j, 0))
    return pl.pallas_call(
        functools.partial(_dn_intra_kernel, chunks=chunks, passes=DN_INTRA_PASSES), grid=(b, nc // chunks),
        in_specs=[pl.BlockSpec((1, tl, 3 * B_WIDTH), lambda i, j: (i, j, 0)),
                  pl.BlockSpec((1, tl, LANE), lambda i, j: (i, j, 0))],
        out_specs=[dir_spec, dir_spec, dir_spec, dir_spec,
                   pl.BlockSpec((1, tl, B_WIDTH), lambda i, j: (i, j, 0)),
                   pl.BlockSpec((1, chunks, 16, LANE), lambda i, j: (i, j, 0, 0))],
        out_shape=[jax.ShapeDtypeStruct((2, b, l, B_WIDTH), F32),
                   jax.ShapeDtypeStruct((2, b, l, B_WIDTH), BF16),
                   jax.ShapeDtypeStruct((2, b, l, B_WIDTH), BF16),
                   jax.ShapeDtypeStruct((2, b, l, B_WIDTH), BF16),
                   jax.ShapeDtypeStruct((b, l, B_WIDTH), BF16),
                   jax.ShapeDtypeStruct((b, nc, 16, LANE), F32)],
        compiler_params=_cparams("parallel", "parallel"), name="deltanet_intra",
    )(qkv, gb)


def _dn_scan_kernel(uf_ref, wf_ref, qgf_ref, kgf_ref, qkf_ref, glf_ref,
                    ub_ref, wb_ref, qgb_ref, kgb_ref, qkb_ref, glb_ref, s0_ref,
                    of_ref, ob_ref, sfin_ref, s_ref, *, chunks):
    j = pl.program_id(1)
    c = DN_CHUNK

    @pl.when(j == 0)
    def _():
        s_ref[...] = s0_ref[0]

    lane = lax.broadcasted_iota(jnp.int32, (c, LANE), 1)
    zeros_b = jnp.zeros((c, LANE), BF16)
    dirs = ((uf_ref, wf_ref, qgf_ref, kgf_ref, qkf_ref, glf_ref, of_ref),
            (ub_ref, wb_ref, qgb_ref, kgb_ref, qkb_ref, glb_ref, ob_ref))
    units = [(d, h) for d in range(2) for h in range(B_HEADS)]
    cols = [slice(h * LANE, (h + 1) * LANE) for _, h in units]
    state = [s_ref[d * B_HEADS + h] for d, h in units]
    for step in range(chunks):
        chunk_of = (step, chunks - 1 - step)
        rows = [slice(chunk_of[d] * c, (chunk_of[d] + 1) * c) for d, _ in units]
        r = [_dot(jnp.concatenate([dirs[d][1][0, 0, rs, hc], dirs[d][2][0, 0, rs, hc]], axis=0), s.astype(BF16))
             for (d, _), hc, rs, s in zip(units, cols, rows, state)]
        vb = [(dirs[d][0][0, 0, rs, hc] - ri[0:c]).astype(BF16) for (d, _), hc, rs, ri in zip(units, cols, rows, r)]
        new_state = []
        for (d, h), hc, rs, s, ri, vi in zip(units, cols, rows, state, r, vb):
            qk2 = dirs[d][4][0, rs, hc]
            if d == 0:
                intra = _dot(jnp.where(lane < c, qk2, jnp.zeros_like(qk2)), jnp.concatenate([vi, zeros_b], axis=0))
            else:
                intra = _dot(jnp.where(lane < c, jnp.zeros_like(qk2), qk2), jnp.concatenate([zeros_b, vi], axis=0))
            dirs[d][6][0, rs, hc] = ri[c:2 * c] + intra
            gl = dirs[d][5][0, chunk_of[d], d * 8 + h:d * 8 + h + 1, :]
            new_state.append(s * gl + _dot_tn(dirs[d][3][0, 0, rs, hc], vi))
        state = new_state
    for (d, h), s in zip(units, state):
        s_ref[d * B_HEADS + h] = s

    @pl.when(j == pl.num_programs(1) - 1)
    def _():
        sfin_ref[0] = s_ref[...]


def deltanet_scan(u, w, qg, kg, qk, gl, s0):
    _, b, l, _ = u.shape
    chunks = math.gcd(DN_SCAN_CHUNKS, l // DN_CHUNK)
    tl = chunks * DN_CHUNK
    nb = l // tl
    fdir = pl.BlockSpec((1, 1, tl, B_WIDTH), lambda i, j: (0, i, j, 0))
    bdir = pl.BlockSpec((1, 1, tl, B_WIDTH), lambda i, j: (1, i, nb - 1 - j, 0))
    fqk = pl.BlockSpec((1, tl, B_WIDTH), lambda i, j: (i, j, 0))
    bqk = pl.BlockSpec((1, tl, B_WIDTH), lambda i, j: (i, nb - 1 - j, 0))
    fgl = pl.BlockSpec((1, chunks, 16, LANE), lambda i, j: (i, j, 0, 0))
    bgl = pl.BlockSpec((1, chunks, 16, LANE), lambda i, j: (i, nb - 1 - j, 0, 0))
    st = pl.BlockSpec((1, 2 * B_HEADS, B_HEAD_DIM, B_HEAD_DIM), lambda i, j: (i, 0, 0, 0))
    return pl.pallas_call(
        functools.partial(_dn_scan_kernel, chunks=chunks), grid=(b, nb),
        in_specs=[fdir, fdir, fdir, fdir, fqk, fgl, bdir, bdir, bdir, bdir, bqk, bgl, st],
        out_specs=[fqk, bqk, st],
        out_shape=[jax.ShapeDtypeStruct((b, l, B_WIDTH), F32),
                   jax.ShapeDtypeStruct((b, l, B_WIDTH), F32),
                   jax.ShapeDtypeStruct((b, 2 * B_HEADS, B_HEAD_DIM, B_HEAD_DIM), F32)],
        scratch_shapes=[pltpu.VMEM((2 * B_HEADS, B_HEAD_DIM, B_HEAD_DIM), F32)],
        compiler_params=_cparams("parallel", "arbitrary"), name="deltanet_scan",
    )(u, w, qg, kg, qk, gl, u, w, qg, kg, qk, gl, s0)


def _dn_out_kernel(of_ref, ob_ref, gate_ref, g_ref, o_ref):
    for h in range(B_HEADS):
        hc = slice(h * LANE, (h + 1) * LANE)
        o = of_ref[0, :, hc] + ob_ref[0, :, hc]
        y = o * lax.rsqrt(jnp.mean(o * o, axis=-1, keepdims=True) + NORM_EPS) * g_ref[...]
        o_ref[0, :, hc] = (y * _silu(gate_ref[0, :, hc].astype(F32))).astype(o_ref.dtype)


def deltanet_out(o_f, o_b, z, norm_g):
    b, l, _ = o_f.shape
    tl = min(512, l)
    blk = ZC_GATE // B_WIDTH
    return pl.pallas_call(
        _dn_out_kernel, grid=(b, l // tl),
        in_specs=[pl.BlockSpec((1, tl, B_WIDTH), lambda i, j: (i, j, 0)),
                  pl.BlockSpec((1, tl, B_WIDTH), lambda i, j: (i, j, 0)),
                  pl.BlockSpec((1, tl, B_WIDTH), lambda i, j: (i, j, blk)),
                  pl.BlockSpec((1, B_HEAD_DIM), lambda i, j: (0, 0))],
        out_specs=pl.BlockSpec((1, tl, B_WIDTH), lambda i, j: (i, j, 0)),
        out_shape=jax.ShapeDtypeStruct((b, l, B_WIDTH), BF16),
        compiler_params=_cparams("parallel", "parallel"), name="deltanet_out",
    )(o_f, o_b, z, norm_g.reshape(1, B_HEAD_DIM))


def _route_kernel(lg_ref, info_ref, cnt_ref, carry_ref):
    i = pl.program_id(0)

    @pl.when(i == 0)
    def _():
        carry_ref[...] = jnp.zeros_like(carry_ref)

    lg = lg_ref[...]
    tl = lg.shape[0]
    lane = lax.broadcasted_iota(jnp.int32, lg.shape, 1)
    valid = lane < N_EXPERTS
    lg = jnp.where(valid, lg, -jnp.inf)
    e = jnp.exp(lg - jnp.max(lg, axis=-1, keepdims=True))
    p = e / jnp.sum(e, axis=-1, keepdims=True)
    p = jnp.where(valid, p, -1.0)
    p1 = jnp.max(p, axis=-1, keepdims=True)
    i1 = jnp.min(jnp.where(p == p1, lane, LANE), axis=-1, keepdims=True)
    pm = jnp.where(lane == i1, -1.0, p)
    p2 = jnp.max(pm, axis=-1, keepdims=True)
    i2 = jnp.min(jnp.where(pm == p2, lane, LANE), axis=-1, keepdims=True)
    tot = p1 + p2
    w1, w2 = p1 / tot, p2 / tot
    hit1, hit2 = lane == i1, lane == i2
    onehot = (hit1 | hit2).astype(F32)
    r = lax.broadcasted_iota(jnp.int32, (tl, tl), 0)
    c = lax.broadcasted_iota(jnp.int32, (tl, tl), 1)
    before = _dot((r > c).astype(BF16), onehot.astype(BF16)) + carry_ref[...]
    r1 = jnp.sum(jnp.where(hit1, before, 0.0), axis=-1, keepdims=True)
    r2 = jnp.sum(jnp.where(hit2, before, 0.0), axis=-1, keepdims=True)
    carry_ref[...] += jnp.sum(onehot, axis=0, keepdims=True)
    cnt_ref[...] = jnp.broadcast_to(carry_ref[...], cnt_ref.shape)
    info = jnp.where(lane == 0, i1.astype(F32), 0.0)
    info = jnp.where(lane == 1, i2.astype(F32), info)
    info = jnp.where(lane == 2, r1, info)
    info = jnp.where(lane == 3, r2, info)
    info = jnp.where(lane == 4, w1, info)
    info = jnp.where(lane == 5, w2, info)
    info_ref[...] = info


def moe_route(logits):
    t = logits.shape[0]
    tl = min(512, t)
    return pl.pallas_call(
        _route_kernel, grid=(t // tl,),
        in_specs=[pl.BlockSpec((tl, LANE), lambda i: (i, 0))],
        out_specs=[pl.BlockSpec((tl, LANE), lambda i: (i, 0)), pl.BlockSpec((8, LANE), lambda i: (0, 0))],
        out_shape=[jax.ShapeDtypeStruct((t, LANE), F32), jax.ShapeDtypeStruct((8, LANE), F32)],
        scratch_shapes=[pltpu.VMEM((1, LANE), F32)],
        compiler_params=_cparams("arbitrary"), name="moe_route",
    )(logits)


def _dispatch_kernel(pos_ref, h_ref, xs_in_ref, xs_ref, sem):
    del xs_in_ref
    tb = pos_ref.shape[2] // TOP_K

    def copy(src_row, dst_row):
        return pltpu.make_async_copy(h_ref.at[pl.ds(pl.multiple_of(src_row * ROW_TILE, ROW_TILE), ROW_TILE)],
                                     xs_ref.at[pl.ds(pl.multiple_of(dst_row * ROW_TILE, ROW_TILE), ROW_TILE)], sem)

    def issue(j, carry):
        for k in range(TOP_K):
            copy(j, pos_ref[0, 0, TOP_K * j + k]).start()
        return carry

    lax.fori_loop(0, tb, issue, 0, unroll=8)
    for _ in range(TOP_K):
        pltpu.make_async_copy(h_ref, xs_ref.at[pl.ds(0, tb * ROW_TILE)], sem).wait()


def moe_dispatch(h_rows, pos, n_rows):
    t = pos.shape[0]
    tb = min(256, t)
    d_rows = h_rows.shape[0] // t
    assert d_rows == ROW_TILE
    zeros = jnp.zeros((n_rows * ROW_TILE, LANE), h_rows.dtype)
    return pl.pallas_call(
        _dispatch_kernel, grid=(t // tb,),
        in_specs=[pl.BlockSpec((1, 1, TOP_K * tb), lambda i: (i, 0, 0), memory_space=pltpu.SMEM),
                  pl.BlockSpec((tb * ROW_TILE, LANE), lambda i: (i, 0)),
                  pl.BlockSpec(memory_space=pl.ANY)],
        out_specs=pl.BlockSpec(memory_space=pl.ANY),
        out_shape=jax.ShapeDtypeStruct(zeros.shape, zeros.dtype),
        scratch_shapes=[pltpu.SemaphoreType.DMA],
        input_output_aliases={2: 0},
        compiler_params=_cparams("arbitrary"), name="moe_dispatch",
    )(pos.reshape(t // tb, 1, TOP_K * tb), h_rows, zeros)


def _moe_up_kernel(te_ref, tv_ref, xs_ref, wg_ref, wu_ref, o_ref, a_ref):
    del te_ref
    i = pl.program_id(0)
    tm = a_ref.shape[0]

    @pl.when(pl.program_id(1) == 0)
    def _():
        for c in range(ROW_TILE):
            lo, hi = _load_token_rows(xs_ref, tm, c)
            a_ref[:, c * LANE:(c + 1) * LANE] = lo.astype(a_ref.dtype)
            a_ref[:, (c + ROW_TILE) * LANE:(c + ROW_TILE + 1) * LANE] = hi.astype(a_ref.dtype)

    @pl.when(tv_ref[i] != 0)
    def _():
        a = a_ref[...]
        g = _dot(a, wg_ref[0].astype(BF16))
        u = _dot(a, wu_ref[0].astype(BF16))
        o_ref[...] = (_silu(g) * u).astype(o_ref.dtype)

    @pl.when(tv_ref[i] == 0)
    def _():
        o_ref[...] = jnp.zeros_like(o_ref)


def moe_up(xs_rows, wg, wu, tile_expert, tile_valid, tm, tn):
    r = xs_rows.shape[0] // ROW_TILE
    d, n = wg.shape[1], wg.shape[2]
    tn = _tile(n, tn)
    nj = n // tn

    def w_index(i, j, te, tv):
        return te[i], 0, jnp.where(tv[i] != 0, j, nj - 1)

    grid_spec = pltpu.PrefetchScalarGridSpec(
        num_scalar_prefetch=2, grid=(r // tm, nj),
        in_specs=[pl.BlockSpec((tm * ROW_TILE, LANE), lambda i, j, te, tv: (i, 0)),
                  pl.BlockSpec((1, d, tn), w_index),
                  pl.BlockSpec((1, d, tn), w_index)],
        out_specs=pl.BlockSpec((tm, tn), lambda i, j, te, tv: (i, j)),
        scratch_shapes=[pltpu.VMEM((tm, d), BF16)])
    return pl.pallas_call(
        _moe_up_kernel, grid_spec=grid_spec,
        out_shape=jax.ShapeDtypeStruct((r, n), BF16),
        compiler_params=_cparams("parallel", "arbitrary"), name="moe_up",
    )(tile_expert, tile_valid, xs_rows, wg, wu)


def _moe_down_kernel(te_ref, tv_ref, a_ref, w_ref, o_ref, acc_ref):
    del te_ref
    i = pl.program_id(0)
    k = pl.program_id(1)

    valid = tv_ref[i] != 0

    @pl.when(jnp.logical_and(valid, k == 0))
    def _():
        acc_ref[...] = _dot(a_ref[...], w_ref[0].astype(BF16))

    @pl.when(jnp.logical_and(valid, k > 0))
    def _():
        acc_ref[...] += _dot(a_ref[...], w_ref[0].astype(BF16))

    @pl.when(jnp.logical_and(jnp.logical_not(valid), k == 0))
    def _():
        acc_ref[...] = jnp.zeros_like(acc_ref)

    @pl.when(k == pl.num_programs(1) - 1)
    def _():
        _store_token_rows(o_ref, acc_ref[...])


def moe_down(hid, wd, tile_expert, tile_valid, tm, tk):
    r, kdim = hid.shape
    n = wd.shape[2]
    assert n == D_TOKEN
    tk = _tile(kdim, tk)
    nk = kdim // tk
    grid_spec = pltpu.PrefetchScalarGridSpec(
        num_scalar_prefetch=2, grid=(r // tm, nk),
        in_specs=[pl.BlockSpec((tm, tk), lambda i, k, te, tv: (i, jnp.where(tv[i] != 0, k, nk - 1))),
                  pl.BlockSpec((1, tk, n), lambda i, k, te, tv: (te[i], jnp.where(tv[i] != 0, k, nk - 1), 0))],
        out_specs=pl.BlockSpec((tm * ROW_TILE, LANE), lambda i, k, te, tv: (i, 0)),
        scratch_shapes=[pltpu.VMEM((tm, n), F32)])
    return pl.pallas_call(
        _moe_down_kernel, grid_spec=grid_spec,
        out_shape=jax.ShapeDtypeStruct((r * ROW_TILE, LANE), U32),
        compiler_params=_cparams("parallel", "arbitrary"), name="moe_down",
    )(tile_expert, tile_valid, hid, wd)


def _combine_kernel(pos_ref, pos_next_ref, ys_ref, x_ref, gate_ref, wt_ref, ng_ref, o_ref, buf_ref, sem, *, final_norm):
    i = pl.program_id(0)
    tb = x_ref.shape[0]
    slot = lax.rem(i, 2)

    def gather(p_ref, s):
        def issue(j, carry):
            for k in range(TOP_K):
                src_row = p_ref[0, 0, TOP_K * j + k]
                pltpu.make_async_copy(
                    ys_ref.at[pl.ds(pl.multiple_of(src_row * ROW_TILE, ROW_TILE), ROW_TILE)],
                    buf_ref.at[s, k, pl.ds(pl.multiple_of(j * ROW_TILE, ROW_TILE), ROW_TILE)], sem.at[s]).start()
            return carry

        lax.fori_loop(0, tb, issue, 0, unroll=8)

    @pl.when(i == 0)
    def _():
        gather(pos_ref, 0)

    @pl.when(i + 1 < pl.num_programs(0))
    def _():
        gather(pos_next_ref, 1 - slot)

    for k in range(TOP_K):
        pltpu.make_async_copy(ys_ref.at[pl.ds(0, tb * ROW_TILE)], buf_ref.at[slot, k], sem.at[slot]).wait()
    w0 = wt_ref[:, 0:1]
    w1 = wt_ref[:, 1:2]
    sq = jnp.zeros((tb, 1), F32)
    for c in range(ROW_TILE):
        first = _load_token_rows(buf_ref.at[slot, 0], tb, c)
        second = _load_token_rows(buf_ref.at[slot, 1], tb, c)
        for half in range(2):
            cols = slice((c + half * ROW_TILE) * LANE, (c + half * ROW_TILE + 1) * LANE)
            f = first[half] * w0 + second[half] * w1
            y = x_ref[:, cols] + gate_ref[0, :, cols] * f
            o_ref[:, cols] = y
            sq = sq + jnp.sum(y * y, axis=-1, keepdims=True)
    if final_norm:
        o_ref[...] = o_ref[...] * lax.rsqrt(sq * (1.0 / D_TOKEN) + NORM_EPS) * ng_ref[...]


def moe_combine(ys_rows, pos, wts, x, gate, rows_per_batch, final_norm_g=None):
    t, d = x.shape
    tb = min(128, t)
    bsz = gate.shape[0]
    final_norm = final_norm_g is not None
    ng = (final_norm_g if final_norm else jnp.ones((d,), F32)).reshape(1, d)
    nb = t // tb
    pos_blocks = pos.reshape(nb, 1, TOP_K * tb)
    return pl.pallas_call(
        functools.partial(_combine_kernel, final_norm=final_norm), grid=(nb,),
        in_specs=[pl.BlockSpec((1, 1, TOP_K * tb), lambda i: (i, 0, 0), memory_space=pltpu.SMEM),
                  pl.BlockSpec((1, 1, TOP_K * tb), lambda i: (jnp.minimum(i + 1, nb - 1), 0, 0),
                               memory_space=pltpu.SMEM),
                  pl.BlockSpec(memory_space=pl.ANY),
                  pl.BlockSpec((tb, d), lambda i: (i, 0)),
                  pl.BlockSpec((1, 1, d), lambda i: ((i * tb) // rows_per_batch, 0, 0)),
                  pl.BlockSpec((tb, TOP_K), lambda i: (i, 0)),
                  pl.BlockSpec((1, d), lambda i: (0, 0))],
        out_specs=pl.BlockSpec((tb, d), lambda i: (i, 0)),
        out_shape=jax.ShapeDtypeStruct((t, d), F32),
        scratch_shapes=[pltpu.VMEM((2, TOP_K, tb * ROW_TILE, LANE), U32), pltpu.SemaphoreType.DMA((2,))],
        compiler_params=_cparams("arbitrary"), name="moe_combine",
    )(pos_blocks, pos_blocks, ys_rows, x, gate.reshape(bsz, 1, d), wts, ng)


MOE_TM = 1024


def moe_ffn(h_rows, logits, wg, wu, wd, x, gate, rows_per_batch, final_norm_g=None):
    t, d = x.shape
    tm = min(MOE_TM, t)
    info, counts = moe_route(logits)
    sizes = counts[0, :N_EXPERTS].astype(jnp.int32)
    padded = ((sizes + tm - 1) // tm) * tm
    ends = jnp.cumsum(padded)
    starts = ends - padded
    n_tiles = (t * TOP_K) // tm + N_EXPERTS
    n_rows = n_tiles * tm
    experts = info[:, 0:TOP_K].astype(jnp.int32)
    pos = starts[experts] + info[:, 2:2 + TOP_K].astype(jnp.int32)
    wts = info[:, 4:4 + TOP_K]
    tile_start = jnp.arange(n_tiles, dtype=jnp.int32) * tm
    tile_expert = jnp.minimum(jnp.sum((tile_start[:, None] >= ends[None, :]).astype(jnp.int32), axis=1), N_EXPERTS - 1)
    tile_valid = (tile_start < ends[-1]).astype(jnp.int32)
    last_expert = jnp.max(jnp.where(sizes > 0, jnp.arange(N_EXPERTS, dtype=jnp.int32), 0))
    tile_expert = jnp.where(tile_valid != 0, tile_expert, last_expert)
    xs_rows = moe_dispatch(h_rows, pos, n_rows)
    hid = moe_up(xs_rows, wg, wu, tile_expert, tile_valid, tm, 512)
    ys_rows = moe_down(hid, wd.astype(BF16), tile_expert, tile_valid, tm, 1792)
    return moe_combine(ys_rows, pos, wts, x, gate, rows_per_batch, final_norm_g)


def _rot_cols(w):
    f = ROPE_AXIS_FREQS
    return jnp.concatenate([-w[:, f:2 * f], w[:, 0:f], -w[:, 3 * f:4 * f], w[:, 2 * f:3 * f]], axis=1)


def _arrange_w_in(w):
    d = w.shape[0]
    a = w[:, 0:2 * A_WIDTH]
    off = 2 * A_WIDTH
    qkv_gate = w[:, off:off + 4 * B_WIDTH]
    logit = w[:, off + 4 * B_WIDTH:off + 4 * B_WIDTH + 4 * B_HEADS]
    off = off + 4 * B_WIDTH + 4 * B_HEADS
    lat = w[:, off:off + Q_LORA + KV_LORA]
    k_pe = w[:, off + Q_LORA + KV_LORA:off + Q_LORA + KV_LORA + C_ROPE]
    z64 = jnp.zeros((d, LANE - C_ROPE), w.dtype)
    main = jnp.concatenate([qkv_gate, a, lat, k_pe, z64, _rot_cols(k_pe), z64], axis=1).astype(BF16)
    logit = jnp.pad(logit, ((0, 0), (0, LANE - 4 * B_HEADS))).astype(BF16)
    return main, logit


def _arrange_w_uq(w):
    k = w.shape[0]
    w = w.reshape(k, C_HEADS, C_NOPE + C_ROPE)
    z64 = jnp.zeros((k, C_HEADS, LANE - C_ROPE), w.dtype)
    pe = w[:, :, C_NOPE:]
    pe_rot = jnp.stack([_rot_cols(pe[:, h]) for h in range(C_HEADS)], axis=1)
    return jnp.concatenate([w[:, :, :C_NOPE], pe, z64, pe_rot, z64], axis=2).reshape(k, C_HEADS * C_HEAD_COLS).astype(BF16)


def _rope_tables(n):
    rows = n // GRID_W
    row = np.repeat(np.arange(rows, dtype=np.float32), GRID_W)
    col = np.tile(np.arange(GRID_W, dtype=np.float32), rows)
    inv = np.power(np.float32(ROPE_BASE), -np.arange(ROPE_AXIS_FREQS, dtype=np.float32) / np.float32(ROPE_AXIS_FREQS))
    ar = row[:, None] * inv.astype(np.float32)
    ac = col[:, None] * inv.astype(np.float32)
    ang = np.concatenate([ar, ar, ac, ac], axis=-1).astype(np.float32)
    pad = ((0, 0), (0, LANE - C_ROPE))
    return (jnp.asarray(np.pad(np.cos(ang).astype(np.float32), pad)),
            jnp.asarray(np.pad(np.sin(ang).astype(np.float32), pad)))


def _mixer_branches(z, zg, p, cos2, sin2):
    out_a = chunk_sgu(z, p["sgu_norm_g"], p["sgu_w"], p["sgu_b"])
    qkv, gb = deltanet_prep(z, zg, p["dn_conv_w"], p["dn_a_log"], p["dn_dt_bias"])
    intra = deltanet_intra(qkv, gb)
    q, k, v = mla_prep(z, p["mla_q_norm_g"], p["mla_kv_norm_g"], p["wq_arr"], p["wkv_arr"], cos2, sin2)
    return out_a, intra, (q, k, v)


def kernel(x, c, ctx, c_ctx, ada_w, ada_b, norm1_g, norm2_g, w_in, sgu_norm_g, sgu_w, sgu_b, dn_conv_w, dn_a_log, dn_dt_bias, dn_norm_g, mla_q_norm_g, mla_w_uq, mla_kv_norm_g, mla_w_ukv, w_out, ffn_w_gate, ffn_w_up, ffn_w_down, moe_router, moe_w_gate, moe_w_up, moe_w_down, final_norm_g):
    b, n, d = x.shape
    lc = ctx.shape[1]
    depth = ada_w.shape[0]
    assert d == D_TOKEN
    cos_lat, sin_lat = _rope_tables(n)
    cos_ctx = jnp.pad(jnp.ones((lc, C_ROPE), F32), ((0, 0), (0, LANE - C_ROPE)))
    sin_ctx = jnp.zeros((lc, LANE), F32)

    cc = jnp.zeros((8, d), F32).at[0:b].set(c).at[b].set(c_ctx)
    mod_all = ada_modulation(cc, ada_w, ada_b)

    xc = ctx
    x_rows = None
    for i in range(depth):
        last = i == depth - 1
        mod = mod_all[i, 0:b].reshape(b, 6, d)
        mod_c = jnp.broadcast_to(mod_all[i, b].reshape(1, 6, d), (b, 6, d))
        w_main, w_logit = _arrange_w_in(w_in[i])
        p = dict(sgu_norm_g=sgu_norm_g[i], sgu_w=sgu_w[i], sgu_b=sgu_b[i], dn_conv_w=dn_conv_w[i],
                 dn_a_log=dn_a_log[i], dn_dt_bias=dn_dt_bias[i], mla_q_norm_g=mla_q_norm_g[i],
                 mla_kv_norm_g=mla_kv_norm_g[i], wq_arr=_arrange_w_uq(mla_w_uq[i]),
                 wkv_arr=mla_w_ukv[i].astype(BF16))
        w_out_b = w_out[i].astype(BF16)

        h = norm_mod(x, norm1_g[i], mod[:, 0], mod[:, 1]).reshape(b * n, d)
        hc = norm_mod(xc, norm1_g[i], mod_c[:, 0], mod_c[:, 1]).reshape(b * lc, d)
        z = matmul(h, w_main, BF16, 1024, 1024).reshape(b, n, Z_COLS)
        zg = matmul(h, w_logit, F32, 1024, LANE).reshape(b, n, LANE)
        zc = matmul(hc, w_main, BF16, 1024, 1024).reshape(b, lc, Z_COLS)
        zgc = matmul(hc, w_logit, F32, 1024, LANE).reshape(b, lc, LANE)

        oa_c, intra_c, (q_c, k_c, v_c) = _mixer_branches(zc, zgc, p, cos_ctx, sin_ctx)
        out_a, intra, (q_l, k_l, v_l) = _mixer_branches(z, zg, p, cos_lat, sin_lat)
        s_zero = jnp.zeros((b, 2 * B_HEADS, B_HEAD_DIM, B_HEAD_DIM), F32)
        ocf, ocb, s_ctx = deltanet_scan(*intra_c, s_zero)
        o_f, o_b, _ = deltanet_scan(*intra, s_ctx)
        out_b = deltanet_out(o_f, o_b, z, dn_norm_g[i])
        out_c = flash_attention(q_l, jnp.concatenate([k_l, k_c], axis=2), jnp.concatenate([v_l, v_c], axis=2))
        x2 = mix_out_residual(out_a.reshape(b * n, -1), out_b.reshape(b * n, -1), out_c.reshape(b * n, -1),
                              w_out_b, x.reshape(b * n, d), mod[:, 2], n, 1024, 1024)
        if not last:
            ob_c = deltanet_out(ocf, ocb, zc, dn_norm_g[i])
            oc_c = flash_attention(q_c, k_c, v_c)
            xc2 = mix_out_residual(oa_c.reshape(b * lc, -1), ob_c.reshape(b * lc, -1), oc_c.reshape(b * lc, -1),
                                   w_out_b, xc.reshape(b * lc, d), mod_c[:, 2], lc, 1024, 1024)

        if i % 2 == 0:
            wg, wu, wd = (ffn_w_gate[i // 2].astype(BF16), ffn_w_up[i // 2].astype(BF16), ffn_w_down[i // 2].astype(BF16))
            h2 = norm_mod(x2.reshape(b, n, d), norm2_g[i], mod[:, 3], mod[:, 4]).reshape(b * n, d)
            hid = swiglu_up(h2, wg, wu, 1024, 512)
            x = down_residual(hid, wd, x2, mod[:, 5], n, 1024, 1024, 1408).reshape(b, n, d)
            if not last:
                hc2 = norm_mod(xc2.reshape(b, lc, d), norm2_g[i], mod_c[:, 3], mod_c[:, 4]).reshape(b * lc, d)
                hid_c = swiglu_up(hc2, wg, wu, 1024, 512)
                xc = down_residual(hid_c, wd, xc2, mod_c[:, 5], lc, 1024, 1024, 1408).reshape(b, lc, d)
        else:
            e = i // 2
            router = jnp.pad(moe_router[e], ((0, 0), (0, LANE - N_EXPERTS)))
            if not last:
                raise NotImplementedError("an expert layer followed by another layer is not part of this model")
            h_rows, logits = norm_mod(x2.reshape(b, n, d), norm2_g[i], mod[:, 3], mod[:, 4], router=router)
            return moe_ffn(h_rows, logits.reshape(b * n, LANE), moe_w_gate[e], moe_w_up[e], moe_w_down[e],
                           x2, mod[:, 5], n, final_norm_g).reshape(b, n, d)
    return rmsnorm_rows(x.reshape(b * n, d), final_norm_g).reshape(b, n, d)
```

```python
import functools
import math

import jax
import jax.numpy as jnp
import numpy as np
from jax import lax
from jax.experimental import pallas as pl
from jax.experimental.pallas import tpu as pltpu

F32 = jnp.float32
BF16 = jnp.bfloat16
HIGHEST = lax.Precision.HIGHEST

NORM_EPS = 1e-6
GRID_W = 64
A_GROUPS = 4
A_GROUP_DIM = 128
A_WIDTH = A_GROUPS * A_GROUP_DIM
A_CHUNK = 128
B_HEADS = 6
B_HEAD_DIM = 128
B_WIDTH = B_HEADS * B_HEAD_DIM
DN_CHUNK = 64
DN_CONV = 5
C_HEADS = 6
C_NOPE = 128
C_ROPE = 64
C_V = 128
C_WIDTH = C_HEADS * C_V
Q_LORA = 512
KV_LORA = 256
ROPE_BASE = 10000.0
ROPE_AXIS_FREQS = C_ROPE // 4
N_EXPERTS = 8
TOP_K = 2

LANE = 128
ROW_TILE = 8
D_TOKEN = 2 * ROW_TILE * LANE
U32 = jnp.uint32
VMEM_LIMIT = 56 * 1024 * 1024

ZC_QKV = 0
ZC_GATE = 3 * B_WIDTH
ZC_A = 4 * B_WIDTH
ZC_C = 4 * B_WIDTH + 2 * A_WIDTH
Z_COLS = ZC_C + 1024
C_HEAD_COLS = 3 * LANE


def _cparams(*sem):
    return pltpu.CompilerParams(dimension_semantics=sem, vmem_limit_bytes=VMEM_LIMIT)


def _tile(n, pref):
    if n <= pref:
        return n
    t = (pref // LANE) * LANE
    while n % t:
        t -= LANE
    return t


def _sigmoid(x):
    return 1.0 / (1.0 + jnp.exp(-x))


def _silu(x):
    return x * _sigmoid(x)


def _dot(a, b):
    return jnp.dot(a, b, preferred_element_type=F32)


def _dot_nt(a, b):
    return lax.dot_general(a, b, (((1,), (1,)), ((), ())), preferred_element_type=F32)


def _dot_tn(a, b):
    return lax.dot_general(a, b, (((0,), (0,)), ((), ())), preferred_element_type=F32)


def _split_bf16(a):
    hi = a.astype(BF16)
    lo = (a - hi.astype(F32)).astype(BF16)
    return hi, lo


def _dot_x3(a, b):
    ah, al = _split_bf16(a)
    bh, bl = _split_bf16(b)
    return _dot(ah, bh) + (_dot(ah, bl) + _dot(al, bh))


def _ada_kernel(c_ref, w_ref, b_ref, o_ref):
    c = c_ref[...]
    o_ref[0] = jnp.dot(_silu(c), w_ref[0], preferred_element_type=F32, precision=HIGHEST) + b_ref[0]


def ada_modulation(cc, ada_w, ada_b):
    depth, d, n = ada_w.shape
    tn = 1024
    return pl.pallas_call(
        _ada_kernel,
        grid=(depth, n // tn),
        in_specs=[
            pl.BlockSpec((8, d), lambda i, j: (0, 0)),
            pl.BlockSpec((1, d, tn), lambda i, j: (i, 0, j)),
            pl.BlockSpec((1, 1, tn), lambda i, j: (i, 0, j)),
        ],
        out_specs=pl.BlockSpec((1, 8, tn), lambda i, j: (i, 0, j)),
        out_shape=jax.ShapeDtypeStruct((depth, 8, n), F32),
        compiler_params=_cparams("parallel", "parallel"),
        name="ada_modulation",
    )(cc, ada_w, ada_b.reshape(depth, 1, n))


def _norm_mod_kernel(x_ref, g_ref, sh_ref, sc_ref, o_ref):
    x = x_ref[0]
    y = x * lax.rsqrt(jnp.mean(x * x, axis=-1, keepdims=True) + NORM_EPS) * g_ref[...]
    o_ref[0] = (y * (1.0 + sc_ref[0]) + sh_ref[0]).astype(o_ref.dtype)


def _norm_mod_router_kernel(x_ref, g_ref, sh_ref, sc_ref, r_ref, o_ref, lg_ref):
    x = x_ref[0]
    y = x * lax.rsqrt(jnp.mean(x * x, axis=-1, keepdims=True) + NORM_EPS) * g_ref[...]
    h = y * (1.0 + sc_ref[0]) + sh_ref[0]
    _store_token_rows(o_ref, h)
    lg_ref[0] = _dot_x3(h, r_ref[...])


def _store_token_rows(rows_ref, x):
    n = x.shape[0]
    for c in range(ROW_TILE):
        lo = lax.bitcast_convert_type(x[:, c * LANE:(c + 1) * LANE].astype(BF16).astype(F32), U32)
        hi = lax.bitcast_convert_type(x[:, (c + ROW_TILE) * LANE:(c + ROW_TILE + 1) * LANE].astype(BF16).astype(F32), U32)
        rows_ref[pl.ds(c, n, stride=ROW_TILE), :] = lax.shift_right_logical(lo, jnp.uint32(16)) | (hi & jnp.uint32(0xFFFF0000))


def _load_token_rows(rows_ref, n, c):
    w = rows_ref[pl.ds(c, n, stride=ROW_TILE), :]
    lo = lax.bitcast_convert_type(lax.shift_left(w, jnp.uint32(16)), F32)
    hi = lax.bitcast_convert_type(w & jnp.uint32(0xFFFF0000), F32)
    return lo, hi


def norm_mod(x, g, shift, scale, out_dtype=BF16, router=None):
    b, l, d = x.shape
    tl = min(512, l)
    in_specs = [
        pl.BlockSpec((1, tl, d), lambda i, j: (i, j, 0)),
        pl.BlockSpec((1, d), lambda i, j: (0, 0)),
        pl.BlockSpec((1, 1, d), lambda i, j: (i, 0, 0)),
        pl.BlockSpec((1, 1, d), lambda i, j: (i, 0, 0)),
    ]
    args = [x, g.reshape(1, d), shift.reshape(b, 1, d), scale.reshape(b, 1, d)]
    h_spec = pl.BlockSpec((1, tl, d), lambda i, j: (i, j, 0))
    h_shape = jax.ShapeDtypeStruct((b, l, d), out_dtype)
    if router is None:
        return pl.pallas_call(
            _norm_mod_kernel, grid=(b, l // tl), in_specs=in_specs, out_specs=h_spec, out_shape=h_shape,
            compiler_params=_cparams("parallel", "parallel"), name="norm_mod",
        )(*args)
    assert d == D_TOKEN
    in_specs.append(pl.BlockSpec((d, LANE), lambda i, j: (0, 0)))
    nl = l // tl
    return pl.pallas_call(
        _norm_mod_router_kernel, grid=(b, nl), in_specs=in_specs,
        out_specs=[pl.BlockSpec((tl * ROW_TILE, LANE), lambda i, j: (i * nl + j, 0)),
                   pl.BlockSpec((1, tl, LANE), lambda i, j: (i, j, 0))],
        out_shape=[jax.ShapeDtypeStruct((b * l * ROW_TILE, LANE), U32), jax.ShapeDtypeStruct((b, l, LANE), F32)],
        compiler_params=_cparams("parallel", "parallel"), name="norm_mod_router",
    )(*args, router)


def _rmsnorm_kernel(x_ref, g_ref, o_ref):
    x = x_ref[...]
    o_ref[...] = x * lax.rsqrt(jnp.mean(x * x, axis=-1, keepdims=True) + NORM_EPS) * g_ref[...]


def rmsnorm_rows(x, g):
    m, d = x.shape
    tm = min(512, m)
    return pl.pallas_call(
        _rmsnorm_kernel, grid=(m // tm,),
        in_specs=[pl.BlockSpec((tm, d), lambda i: (i, 0)), pl.BlockSpec((1, d), lambda i: (0, 0))],
        out_specs=pl.BlockSpec((tm, d), lambda i: (i, 0)),
        out_shape=jax.ShapeDtypeStruct((m, d), F32),
        compiler_params=_cparams("parallel"), name="final_rmsnorm",
    )(x, g.reshape(1, d))


def _mm_kernel(a_ref, w_ref, o_ref):
    o_ref[...] = _dot(a_ref[...], w_ref[...]).astype(o_ref.dtype)


def matmul(a, w, out_dtype, tm, tn):
    m, k = a.shape
    n = w.shape[1]
    tm, tn = _tile(m, tm), _tile(n, tn)
    return pl.pallas_call(
        _mm_kernel, grid=(m // tm, n // tn),
        in_specs=[pl.BlockSpec((tm, k), lambda i, j: (i, 0)), pl.BlockSpec((k, tn), lambda i, j: (0, j))],
        out_specs=pl.BlockSpec((tm, tn), lambda i, j: (i, j)),
        out_shape=jax.ShapeDtypeStruct((m, n), out_dtype),
        compiler_params=_cparams("parallel", "parallel"), name="matmul",
    )(a, w)


def _swiglu_up_kernel(a_ref, wg_ref, wu_ref, o_ref):
    a = a_ref[...]
    g = _dot(a, wg_ref[...])
    u = _dot(a, wu_ref[...])
    o_ref[...] = (_silu(g) * u).astype(o_ref.dtype)


def swiglu_up(a, wg, wu, tm, tn):
    m, k = a.shape
    n = wg.shape[1]
    tm, tn = _tile(m, tm), _tile(n, tn)
    return pl.pallas_call(
        _swiglu_up_kernel, grid=(m // tm, n // tn),
        in_specs=[pl.BlockSpec((tm, k), lambda i, j: (i, 0)),
                  pl.BlockSpec((k, tn), lambda i, j: (0, j)),
                  pl.BlockSpec((k, tn), lambda i, j: (0, j))],
        out_specs=pl.BlockSpec((tm, tn), lambda i, j: (i, j)),
        out_shape=jax.ShapeDtypeStruct((m, n), BF16),
        compiler_params=_cparams("parallel", "parallel"), name="swiglu_up",
    )(a, wg, wu)


def _down_res_kernel(a_ref, w_ref, x_ref, gate_ref, o_ref, acc_ref):
    k = pl.program_id(2)

    @pl.when(k == 0)
    def _():
        acc_ref[...] = _dot(a_ref[...], w_ref[...])

    @pl.when(k > 0)
    def _():
        acc_ref[...] += _dot(a_ref[...], w_ref[...])

    @pl.when(k == pl.num_programs(2) - 1)
    def _():
        o_ref[...] = x_ref[...] + gate_ref[0] * acc_ref[...]


def down_residual(a, w, x, gate, rows_per_batch, tm, tn, tk):
    m, kdim = a.shape
    n = w.shape[1]
    tm, tn, tk = _tile(rows_per_batch, tm), _tile(n, tn), _tile(kdim, tk)
    bsz = gate.shape[0]
    return pl.pallas_call(
        _down_res_kernel, grid=(m // tm, n // tn, kdim // tk),
        in_specs=[pl.BlockSpec((tm, tk), lambda i, j, k: (i, k)),
                  pl.BlockSpec((tk, tn), lambda i, j, k: (k, j)),
                  pl.BlockSpec((tm, tn), lambda i, j, k: (i, j)),
                  pl.BlockSpec((1, 1, tn), lambda i, j, k: ((i * tm) // rows_per_batch, 0, j))],
        out_specs=pl.BlockSpec((tm, tn), lambda i, j, k: (i, j)),
        out_shape=jax.ShapeDtypeStruct((m, n), F32),
        scratch_shapes=[pltpu.VMEM((tm, tn), F32)],
        compiler_params=_cparams("parallel", "parallel", "arbitrary"), name="down_residual",
    )(a, w, x, gate.reshape(bsz, 1, n))


def _mix_out_kernel(a_ref, b_ref, c_ref, w_ref, x_ref, gate_ref, o_ref):
    ka, kb = a_ref.shape[1], b_ref.shape[1]
    acc = _dot(a_ref[...], w_ref[0:ka, :])
    acc += _dot(b_ref[...], w_ref[ka:ka + kb, :])
    acc += _dot(c_ref[...], w_ref[ka + kb:, :])
    o_ref[...] = x_ref[...] + gate_ref[0] * acc


def mix_out_residual(oa, ob, oc, w, x, gate, rows_per_batch, tm, tn):
    m = oa.shape[0]
    kdim, n = w.shape
    tm, tn = _tile(rows_per_batch, tm), _tile(n, tn)
    bsz = gate.shape[0]
    return pl.pallas_call(
        _mix_out_kernel, grid=(m // tm, n // tn),
        in_specs=[pl.BlockSpec((tm, oa.shape[1]), lambda i, j: (i, 0)),
                  pl.BlockSpec((tm, ob.shape[1]), lambda i, j: (i, 0)),
                  pl.BlockSpec((tm, oc.shape[1]), lambda i, j: (i, 0)),
                  pl.BlockSpec((kdim, tn), lambda i, j: (0, j)),
                  pl.BlockSpec((tm, tn), lambda i, j: (i, j)),
                  pl.BlockSpec((1, 1, tn), lambda i, j: ((i * tm) // rows_per_batch, 0, j))],
        out_specs=pl.BlockSpec((tm, tn), lambda i, j: (i, j)),
        out_shape=jax.ShapeDtypeStruct((m, n), F32),
        compiler_params=_cparams("parallel", "parallel"), name="mix_out_residual",
    )(oa, ob, oc, w, x, gate.reshape(bsz, 1, n))


def _gelu_tanh(x):
    return 0.5 * x * (1.0 + jnp.tanh(math.sqrt(2.0 / math.pi) * (x + 0.044715 * (x * x * x))))


def _sgu_kernel(z_ref, g_ref, w_ref, b_ref, o_ref):
    tl = z_ref.shape[1]
    for c in range(tl // A_CHUNK):
        rows = slice(c * A_CHUNK, (c + 1) * A_CHUNK)
        for g in range(A_GROUPS):
            cols = slice(g * A_GROUP_DIM, (g + 1) * A_GROUP_DIM)
            u = _gelu_tanh(z_ref[0, rows, cols].astype(F32))
            v = _gelu_tanh(z_ref[0, rows, A_WIDTH + g * A_GROUP_DIM:A_WIDTH + (g + 1) * A_GROUP_DIM].astype(F32))
            vc = v - jnp.mean(v, axis=-1, keepdims=True)
            vn = vc * lax.rsqrt(jnp.mean(vc * vc, axis=-1, keepdims=True) + NORM_EPS) * g_ref[:, cols]
            mixed = _dot(w_ref[g], vn.astype(BF16)) + b_ref[g]
            o_ref[0, rows, cols] = (u * mixed).astype(o_ref.dtype)


def chunk_sgu(z, norm_g, w_s, b_s):
    b, l, _ = z.shape
    tl = min(512, l)
    blk = ZC_A // (2 * A_WIDTH)
    return pl.pallas_call(
        _sgu_kernel, grid=(b, l // tl),
        in_specs=[pl.BlockSpec((1, tl, 2 * A_WIDTH), lambda i, j: (i, j, blk)),
                  pl.BlockSpec((1, A_WIDTH), lambda i, j: (0, 0)),
                  pl.BlockSpec((A_GROUPS, A_CHUNK, A_CHUNK), lambda i, j: (0, 0, 0)),
                  pl.BlockSpec((A_GROUPS, A_CHUNK, 1), lambda i, j: (0, 0, 0))],
        out_specs=pl.BlockSpec((1, tl, A_WIDTH), lambda i, j: (i, j, 0)),
        out_shape=jax.ShapeDtypeStruct((b, l, A_WIDTH), BF16),
        compiler_params=_cparams("parallel", "parallel"), name="chunk_sgu",
    )(z, norm_g.reshape(1, A_WIDTH), w_s.astype(BF16), b_s.reshape(A_GROUPS, A_CHUNK, 1))


def _mla_prep_kernel(z_ref, qg_ref, kvg_ref, wq_ref, wkv_ref, cos_ref, sin_ref, *rest, q_scale):
    q_ref, k_ref, v_ref = rest[-3:]
    lat = z_ref[0, :, 0:Q_LORA].astype(F32)
    cq = (lat * lax.rsqrt(jnp.mean(lat * lat, axis=-1, keepdims=True) + NORM_EPS) * qg_ref[...]).astype(BF16)
    kvl = z_ref[0, :, Q_LORA:Q_LORA + KV_LORA].astype(F32)
    ckv = (kvl * lax.rsqrt(jnp.mean(kvl * kvl, axis=-1, keepdims=True) + NORM_EPS) * kvg_ref[...]).astype(BF16)
    cos = cos_ref[...]
    sin = sin_ref[...]
    base = Q_LORA + KV_LORA
    k_pe = z_ref[0, :, base:base + LANE].astype(F32) * cos + z_ref[0, :, base + LANE:base + 2 * LANE].astype(F32) * sin
    k_pe = k_pe.astype(k_ref.dtype)
    for h in range(C_HEADS):
        qh = _dot(cq, wq_ref[:, h * C_HEAD_COLS:(h + 1) * C_HEAD_COLS])
        q_pe = qh[:, LANE:2 * LANE] * cos + qh[:, 2 * LANE:3 * LANE] * sin
        q_ref[0, h, :, 0:LANE] = (qh[:, 0:LANE] * q_scale).astype(q_ref.dtype)
        q_ref[0, h, :, LANE:2 * LANE] = (q_pe * q_scale).astype(q_ref.dtype)
        kv = _dot(ckv, wkv_ref[:, h * 2 * LANE:(h + 1) * 2 * LANE])
        k_ref[0, h, :, 0:LANE] = kv[:, 0:LANE].astype(k_ref.dtype)
        k_ref[0, h, :, LANE:2 * LANE] = k_pe
        v_ref[0, h] = kv[:, LANE:2 * LANE].astype(v_ref.dtype)


def mla_prep(z, q_norm_g, kv_norm_g, wq_arr, wkv_arr, cos2, sin2, kv_rows=None, kv_row0=0, kv_into=None):
    b, l, _ = z.shape
    tl = min(256, l)
    blk = ZC_C // 1024
    kv_rows = l if kv_rows is None else kv_rows
    assert kv_row0 % tl == 0
    blk0 = kv_row0 // tl
    q_scale = (C_NOPE + C_ROPE) ** -0.5 * math.log2(math.e)
    in_specs = [pl.BlockSpec((1, tl, 1024), lambda i, j: (i, j, blk)),
                pl.BlockSpec((1, Q_LORA), lambda i, j: (0, 0)),
                pl.BlockSpec((1, KV_LORA), lambda i, j: (0, 0)),
                pl.BlockSpec((Q_LORA, C_HEADS * C_HEAD_COLS), lambda i, j: (0, 0)),
                pl.BlockSpec((KV_LORA, C_HEADS * 2 * LANE), lambda i, j: (0, 0)),
                pl.BlockSpec((tl, LANE), lambda i, j: (j, 0)),
                pl.BlockSpec((tl, LANE), lambda i, j: (j, 0))]
    args = [z, q_norm_g.reshape(1, Q_LORA), kv_norm_g.reshape(1, KV_LORA), wq_arr, wkv_arr, cos2, sin2]
    aliases = {}
    if kv_into is not None:
        in_specs += [pl.BlockSpec(memory_space=pl.ANY), pl.BlockSpec(memory_space=pl.ANY)]
        aliases = {len(args): 1, len(args) + 1: 2}
        args += list(kv_into)
    return pl.pallas_call(
        functools.partial(_mla_prep_kernel, q_scale=q_scale), grid=(b, l // tl),
        in_specs=in_specs,
        out_specs=[pl.BlockSpec((1, C_HEADS, tl, 2 * LANE), lambda i, j: (i, 0, j, 0)),
                   pl.BlockSpec((1, C_HEADS, tl, 2 * LANE), lambda i, j: (i, 0, blk0 + j, 0)),
                   pl.BlockSpec((1, C_HEADS, tl, LANE), lambda i, j: (i, 0, blk0 + j, 0))],
        out_shape=[jax.ShapeDtypeStruct((b, C_HEADS, l, 2 * LANE), BF16),
                   jax.ShapeDtypeStruct((b, C_HEADS, kv_rows, 2 * LANE), BF16),
                   jax.ShapeDtypeStruct((b, C_HEADS, kv_rows, LANE), BF16)],
        input_output_aliases=aliases,
        compiler_params=_cparams("parallel", "parallel"), name="mla_prep",
    )(*args)


FLASH_ROWS = 32


def _flash_kernel(q_ref, k_ref, v_ref, o_ref, s_ref, p_ref, m_ref, l_ref, acc_ref, *, tk, nk):
    tq = q_ref.shape[2]
    rb = min(FLASH_ROWS, tq)
    m_ref[...] = jnp.full(m_ref.shape, -jnp.inf, F32)
    l_ref[...] = jnp.zeros(l_ref.shape, F32)
    acc_ref[...] = jnp.zeros(acc_ref.shape, F32)

    for j in range(nk):
        s_buf, p_buf = s_ref.at[j % 2], p_ref.at[j % 2]
        keys = slice(j * tk, (j + 1) * tk)
        s_buf[...] = _dot_nt(q_ref[0, 0], k_ref[0, 0, keys, :])
        for r in range(tq // rb):
            rows = slice(r * rb, (r + 1) * rb)
            blocks = [s_buf[rows, t * LANE:(t + 1) * LANE] for t in range(tk // LANE)]
            mx = blocks[0]
            for blk in blocks[1:]:
                mx = jnp.maximum(mx, blk)
            m_old = m_ref[rows, :]
            m_new = jnp.maximum(m_old, jnp.broadcast_to(jnp.max(mx, axis=-1, keepdims=True), (rb, LANE)))
            alpha = jnp.exp2(m_old - m_new)
            lane_sum = None
            for t, blk in enumerate(blocks):
                p = jnp.exp2(blk - m_new)
                lane_sum = p if lane_sum is None else lane_sum + p
                p_buf[rows, t * LANE:(t + 1) * LANE] = p.astype(BF16)
            l_ref[rows, :] = alpha * l_ref[rows, :] + lane_sum
            m_ref[rows, :] = m_new
            acc_ref[rows, :] = alpha * acc_ref[rows, :]
        acc_ref[...] += _dot(p_buf[...], v_ref[0, 0, keys, :])
    o_ref[0] = (acc_ref[...] / jnp.sum(l_ref[...], axis=-1, keepdims=True)).astype(o_ref.dtype)


def flash_attention(q, k, v):
    b, h, lq, dq = q.shape
    lk = k.shape[2]
    tq = min(512, lq)
    tk = _tile(lk, 256)
    return pl.pallas_call(
        functools.partial(_flash_kernel, tk=tk, nk=lk // tk), grid=(b, h, lq // tq),
        in_specs=[pl.BlockSpec((1, 1, tq, dq), lambda i, j, t: (i, j, t, 0)),
                  pl.BlockSpec((1, 1, lk, dq), lambda i, j, t: (i, j, 0, 0)),
                  pl.BlockSpec((1, 1, lk, C_V), lambda i, j, t: (i, j, 0, 0))],
        out_specs=pl.BlockSpec((1, tq, C_V), lambda i, j, t: (i, t, j)),
        out_shape=jax.ShapeDtypeStruct((b, lq, h * C_V), BF16),
        scratch_shapes=[pltpu.VMEM((2, tq, tk), F32), pltpu.VMEM((2, tq, tk), BF16), pltpu.VMEM((tq, LANE), F32),
                        pltpu.VMEM((tq, LANE), F32), pltpu.VMEM((tq, C_V), F32)],
        compiler_params=_cparams("parallel", "parallel", "parallel"), name="flash_attention",
    )(q, k, v)


def _softplus(x):
    return jnp.maximum(x, 0.0) + jnp.log1p(jnp.exp(-jnp.abs(x)))


def _dn_prep_kernel(zm_ref, zp_ref, zn_ref, zg_ref, w_ref, alog_ref, dtb_ref, qkv_ref, gb_ref):
    j = pl.program_id(1)
    tl = zm_ref.shape[1]
    half = DN_CONV // 2
    keep_prev = (j > 0).astype(F32)
    keep_next = (j < pl.num_programs(1) - 1).astype(F32)
    for c in range(3 * B_HEADS):
        cols = slice(c * LANE, (c + 1) * LANE)
        prev = zp_ref[0, :, cols].astype(F32)[8:16] * keep_prev
        nxt = zn_ref[0, :, cols].astype(F32)[0:8] * keep_next
        ext = jnp.concatenate([prev, zm_ref[0, :, cols].astype(F32), nxt], axis=0)
        y = ext[8 - half:8 - half + tl] * w_ref[0:1, cols]
        for i in range(1, DN_CONV):
            y = y + ext[8 - half + i:8 - half + i + tl] * w_ref[i:i + 1, cols]
        y = _silu(y)
        if c < 2 * B_HEADS:
            y = y * lax.rsqrt(jnp.sum(y * y, axis=-1, keepdims=True) + NORM_EPS)
        qkv_ref[0, :, cols] = y.astype(qkv_ref.dtype)
    zg = zg_ref[0]
    lane = lax.broadcasted_iota(jnp.int32, zg.shape, 1)
    g = -jnp.exp(alog_ref[...]) * _softplus(zg + dtb_ref[...])
    gb_ref[0] = jnp.where(lane < 2 * B_HEADS, g, _sigmoid(zg))


def deltanet_prep(z, zg, conv_w, a_log, dt_bias):
    b, l, _ = z.shape
    tl = min(256, l)
    wq = 3 * B_WIDTH
    nb16 = l // 16
    pad = LANE - 2 * B_HEADS
    alog = jnp.pad(a_log.reshape(1, -1), ((0, 0), (0, pad)))
    dtb = jnp.pad(dt_bias.reshape(1, -1), ((0, 0), (0, pad)))
    return pl.pallas_call(
        _dn_prep_kernel, grid=(b, l // tl),
        in_specs=[pl.BlockSpec((1, tl, wq), lambda i, j: (i, j, 0)),
                  pl.BlockSpec((1, 16, wq), lambda i, j: (i, jnp.maximum(j * (tl // 16) - 1, 0), 0)),
                  pl.BlockSpec((1, 16, wq), lambda i, j: (i, jnp.minimum((j + 1) * (tl // 16), nb16 - 1), 0)),
                  pl.BlockSpec((1, tl, LANE), lambda i, j: (i, j, 0)),
                  pl.BlockSpec((DN_CONV, wq), lambda i, j: (0, 0)),
                  pl.BlockSpec((1, LANE), lambda i, j: (0, 0)),
                  pl.BlockSpec((1, LANE), lambda i, j: (0, 0))],
        out_specs=[pl.BlockSpec((1, tl, wq), lambda i, j: (i, j, 0)),
                   pl.BlockSpec((1, tl, LANE), lambda i, j: (i, j, 0))],
        out_shape=[jax.ShapeDtypeStruct((b, l, wq), BF16), jax.ShapeDtypeStruct((b, l, LANE), F32)],
        compiler_params=_cparams("parallel", "parallel"), name="deltanet_prep",
    )(z, z, z, zg, conv_w, alog, dtb)


def _dot_bf16(a, b):
    return _dot(a.astype(BF16), b.astype(BF16))


def _dn_intra_kernel(qkv_ref, gb_ref, u_ref, w_ref, qg_ref, kg_ref, qk_ref, gl_ref, *, chunks, passes):
    c = DN_CHUNK
    row = lax.broadcasted_iota(jnp.int32, (c, LANE), 0)
    lane = lax.broadcasted_iota(jnp.int32, (c, LANE), 1)
    col = jnp.where(lane < c, lane, lane - c)
    fwd = lane < c
    bwd = jnp.logical_not(fwd)
    incl = (fwd & (row >= col)) | (bwd & (row <= col))
    strict = (fwd & (row > col)) | (bwd & (row < col))
    same16 = lax.shift_right_logical(row, 4) == lax.shift_right_logical(col, 4)
    same32 = lax.shift_right_logical(row, 5) == lax.shift_right_logical(col, 5)
    diag = row == col
    eye2 = diag.astype(F32)
    r64 = lax.broadcasted_iota(jnp.int32, (c, c), 0)
    c64 = lax.broadcasted_iota(jnp.int32, (c, c), 1)
    tri_lo = (r64 >= c64).astype(F32)
    tri_up = (r64 <= c64).astype(F32)
    scale = B_HEAD_DIM ** -0.5
    dotp = _dot_x3 if passes == 3 else _dot_bf16

    def pick(x, idx):
        return jnp.sum(jnp.where(lane == idx, x, 0.0), axis=-1, keepdims=True)

    def blockdiag(y2):
        return jnp.concatenate([jnp.where(fwd, y2, 0.0), jnp.where(fwd, 0.0, y2)], axis=0)

    def mm(xs, ys):
        return [dotp(x2, blockdiag(y2)) for x2, y2 in zip(xs, ys)]

    rows = [slice(ci * c, (ci + 1) * c) for ci in range(chunks)]
    gb = [gb_ref[0, rs, :] for rs in rows]
    cum_f = [jnp.dot(tri_lo, g, preferred_element_type=F32, precision=HIGHEST) for g in gb]
    cum_b = [jnp.dot(tri_up, g, preferred_element_type=F32, precision=HIGHEST) for g in gb]
    tot_f = [x[c - 1:c, :] for x in cum_f]
    tot_b = [x[0:1, :] for x in cum_b]

    units = [(ci, h) for ci in range(chunks) for h in range(B_HEADS)]
    q = [qkv_ref[0, rows[ci], h * LANE:(h + 1) * LANE] for ci, h in units]
    k = [qkv_ref[0, rows[ci], B_WIDTH + h * LANE:B_WIDTH + (h + 1) * LANE] for ci, h in units]
    v = [qkv_ref[0, rows[ci], 2 * B_WIDTH + h * LANE:2 * B_WIDTH + (h + 1) * LANE].astype(F32) for ci, h in units]
    k2 = [jnp.concatenate([x, x], axis=0) for x in k]
    kk2 = [_dot_nt(x, y) for x, y in zip(k, k2)]
    qk2 = [_dot_nt(x, y) for x, y in zip(q, k2)]
    cf = [pick(cum_f[ci], h) for ci, h in units]
    cb = [pick(cum_b[ci], B_HEADS + h) for ci, h in units]
    bf = [pick(gb[ci], 2 * B_HEADS + h) for ci, h in units]
    bb = [pick(gb[ci], 3 * B_HEADS + h) for ci, h in units]
    lf = [pick(jnp.broadcast_to(tot_f[ci], (c, LANE)), h) for ci, h in units]
    lb = [pick(jnp.broadcast_to(tot_b[ci], (c, LANE)), B_HEADS + h) for ci, h in units]
    c2 = [jnp.where(fwd, x, y) for x, y in zip(cf, cb)]
    r2 = [jnp.sum(jnp.where(diag, x, 0.0), axis=0, keepdims=True) for x in c2]
    decay2 = [jnp.where(incl, jnp.exp(jnp.where(incl, x - y, 0.0)), 0.0) for x, y in zip(c2, r2)]
    l2 = [jnp.where(strict, jnp.where(fwd, x, y) * kk * dc, 0.0) for x, y, kk, dc in zip(bf, bb, kk2, decay2)]
    mp = [jnp.where(same16, -x, 0.0) for x in l2]
    p = [eye2 + x for x in mp]
    for _ in range(3):
        mp = mm(mp, mp)
        p = [x + y for x, y in zip(p, mm(p, mp))]
    off = [jnp.where(same32 & jnp.logical_not(same16), x, 0.0) for x in l2]
    p = [x - y for x, y in zip(p, mm(mm(p, off), p))]
    off = [jnp.where(same32, 0.0, x) for x in l2]
    p = [x - y for x, y in zip(p, mm(mm(p, off), p))]
    ef = [jnp.exp(x) for x in cf]
    eb = [jnp.exp(x) for x in cb]
    zero = jnp.zeros((c, LANE), F32)
    sol = []
    for i in range(len(units)):
        kf = k[i].astype(F32)
        rhs = jnp.concatenate([
            jnp.concatenate([v[i] * bf[i], kf * (bf[i] * ef[i]), zero, zero], axis=1),
            jnp.concatenate([zero, zero, v[i] * bb[i], kf * (bb[i] * eb[i])], axis=1)], axis=0)
        sol.append(dotp(p[i], rhs))
    for i, (ci, h) in enumerate(units):
        hc = slice(h * LANE, (h + 1) * LANE)
        rs = rows[ci]
        kf = k[i].astype(F32)
        qf = q[i].astype(F32)
        u_ref[0, 0, rs, hc] = sol[i][:, 0:LANE]
        w_ref[0, 0, rs, hc] = sol[i][:, LANE:2 * LANE].astype(w_ref.dtype)
        u_ref[1, 0, rs, hc] = sol[i][:, 2 * LANE:3 * LANE]
        w_ref[1, 0, rs, hc] = sol[i][:, 3 * LANE:4 * LANE].astype(w_ref.dtype)
        qg_ref[0, 0, rs, hc] = (qf * (ef[i] * scale)).astype(qg_ref.dtype)
        qg_ref[1, 0, rs, hc] = (qf * (eb[i] * scale)).astype(qg_ref.dtype)
        kg_ref[0, 0, rs, hc] = (kf * jnp.exp(lf[i] - cf[i])).astype(kg_ref.dtype)
        kg_ref[1, 0, rs, hc] = (kf * jnp.exp(lb[i] - cb[i])).astype(kg_ref.dtype)
        qk_ref[0, rs, hc] = (qk2[i] * decay2[i] * scale).astype(qk_ref.dtype)
    r8 = lax.broadcasted_iota(jnp.int32, (2 * 8, LANE), 0)
    l8 = lax.broadcasted_iota(jnp.int32, (2 * 8, LANE), 1)
    want = jnp.where(r8 < 8, r8, r8 - 8 + B_HEADS)
    for ci in range(chunks):
        src = jnp.concatenate([jnp.broadcast_to(tot_f[ci], (8, LANE)), jnp.broadcast_to(tot_b[ci], (8, LANE))], axis=0)
        tot = jnp.sum(jnp.where(l8 == want, src, 0.0), axis=-1, keepdims=True)
        gl_ref[0, ci] = jnp.broadcast_to(jnp.exp(tot), (2 * 8, LANE))


DN_INTRA_CHUNKS = 4
DN_INTRA_PASSES = 1
DN_SCAN_CHUNKS = 4


def deltanet_intra(qkv, gb):
    b, l, _ = qkv.shape
    nc = l // DN_CHUNK
    chunks = math.gcd(DN_INTRA_CHUNKS, nc)
    tl = chunks * DN_CHUNK
    dir_spec = pl.BlockSpec((2, 1, tl, B_WIDTH), lambda i, j: (0, i, j, 0))
    return pl.pallas_call(
        functools.partial(_dn_intra_kernel, chunks=chunks, passes=DN_INTRA_PASSES), grid=(b, nc // chunks),
        in_specs=[pl.BlockSpec((1, tl, 3 * B_WIDTH), lambda i, j: (i, j, 0)),
                  pl.BlockSpec((1, tl, LANE), lambda i, j: (i, j, 0))],
        out_specs=[dir_spec, dir_spec, dir_spec, dir_spec,
                   pl.BlockSpec((1, tl, B_WIDTH), lambda i, j: (i, j, 0)),
                   pl.BlockSpec((1, chunks, 16, LANE), lambda i, j: (i, j, 0, 0))],
        out_shape=[jax.ShapeDtypeStruct((2, b, l, B_WIDTH), F32),
                   jax.ShapeDtypeStruct((2, b, l, B_WIDTH), BF16),
                   jax.ShapeDtypeStruct((2, b, l, B_WIDTH), BF16),
                   jax.ShapeDtypeStruct((2, b, l, B_WIDTH), BF16),
                   jax.ShapeDtypeStruct((b, l, B_WIDTH), BF16),
                   jax.ShapeDtypeStruct((b, nc, 16, LANE), F32)],
        compiler_params=_cparams("parallel", "parallel"), name="deltanet_intra",
    )(qkv, gb)


def _dn_scan_kernel(uf_ref, wf_ref, qgf_ref, kgf_ref, qkf_ref, glf_ref,
                    ub_ref, wb_ref, qgb_ref, kgb_ref, qkb_ref, glb_ref, s0_ref,
                    of_ref, ob_ref, sfin_ref, s_ref, *, chunks):
    j = pl.program_id(1)
    c = DN_CHUNK

    @pl.when(j == 0)
    def _():
        s_ref[...] = s0_ref[0]

    lane = lax.broadcasted_iota(jnp.int32, (c, LANE), 1)
    zeros_b = jnp.zeros((c, LANE), BF16)
    dirs = ((uf_ref, wf_ref, qgf_ref, kgf_ref, qkf_ref, glf_ref, of_ref),
            (ub_ref, wb_ref, qgb_ref, kgb_ref, qkb_ref, glb_ref, ob_ref))
    units = [(d, h) for d in range(2) for h in range(B_HEADS)]
    cols = [slice(h * LANE, (h + 1) * LANE) for _, h in units]
    state = [s_ref[d * B_HEADS + h] for d, h in units]
    for step in range(chunks):
        chunk_of = (step, chunks - 1 - step)
        rows = [slice(chunk_of[d] * c, (chunk_of[d] + 1) * c) for d, _ in units]
        r = [_dot(jnp.concatenate([dirs[d][1][0, 0, rs, hc], dirs[d][2][0, 0, rs, hc]], axis=0), s.astype(BF16))
             for (d, _), hc, rs, s in zip(units, cols, rows, state)]
        vb = [(dirs[d][0][0, 0, rs, hc] - ri[0:c]).astype(BF16) for (d, _), hc, rs, ri in zip(units, cols, rows, r)]
        new_state = []
        for (d, h), hc, rs, s, ri, vi in zip(units, cols, rows, state, r, vb):
            qk2 = dirs[d][4][0, rs, hc]
            if d == 0:
                intra = _dot(jnp.where(lane < c, qk2, jnp.zeros_like(qk2)), jnp.concatenate([vi, zeros_b], axis=0))
            else:
                intra = _dot(jnp.where(lane < c, jnp.zeros_like(qk2), qk2), jnp.concatenate([zeros_b, vi], axis=0))
            dirs[d][6][0, rs, hc] = ri[c:2 * c] + intra
            gl = dirs[d][5][0, chunk_of[d], d * 8 + h:d * 8 + h + 1, :]
            new_state.append(s * gl + _dot_tn(dirs[d][3][0, 0, rs, hc], vi))
        state = new_state
    for (d, h), s in zip(units, state):
        s_ref[d * B_HEADS + h] = s

    @pl.when(j == pl.num_programs(1) - 1)
    def _():
        sfin_ref[0] = s_ref[...]


def deltanet_scan(u, w, qg, kg, qk, gl, s0):
    _, b, l, _ = u.shape
    chunks = math.gcd(DN_SCAN_CHUNKS, l // DN_CHUNK)
    tl = chunks * DN_CHUNK
    nb = l // tl
    fdir = pl.BlockSpec((1, 1, tl, B_WIDTH), lambda i, j: (0, i, j, 0))
    bdir = pl.BlockSpec((1, 1, tl, B_WIDTH), lambda i, j: (1, i, nb - 1 - j, 0))
    fqk = pl.BlockSpec((1, tl, B_WIDTH), lambda i, j: (i, j, 0))
    bqk = pl.BlockSpec((1, tl, B_WIDTH), lambda i, j: (i, nb - 1 - j, 0))
    fgl = pl.BlockSpec((1, chunks, 16, LANE), lambda i, j: (i, j, 0, 0))
    bgl = pl.BlockSpec((1, chunks, 16, LANE), lambda i, j: (i, nb - 1 - j, 0, 0))
    st = pl.BlockSpec((1, 2 * B_HEADS, B_HEAD_DIM, B_HEAD_DIM), lambda i, j: (i, 0, 0, 0))
    return pl.pallas_call(
        functools.partial(_dn_scan_kernel, chunks=chunks), grid=(b, nb),
        in_specs=[fdir, fdir, fdir, fdir, fqk, fgl, bdir, bdir, bdir, bdir, bqk, bgl, st],
        out_specs=[fqk, bqk, st],
        out_shape=[jax.ShapeDtypeStruct((b, l, B_WIDTH), F32),
                   jax.ShapeDtypeStruct((b, l, B_WIDTH), F32),
                   jax.ShapeDtypeStruct((b, 2 * B_HEADS, B_HEAD_DIM, B_HEAD_DIM), F32)],
        scratch_shapes=[pltpu.VMEM((2 * B_HEADS, B_HEAD_DIM, B_HEAD_DIM), F32)],
        compiler_params=_cparams("parallel", "arbitrary"), name="deltanet_scan",
    )(u, w, qg, kg, qk, gl, u, w, qg, kg, qk, gl, s0)


def _dn_out_kernel(of_ref, ob_ref, gate_ref, g_ref, o_ref):
    for h in range(B_HEADS):
        hc = slice(h * LANE, (h + 1) * LANE)
        o = of_ref[0, :, hc] + ob_ref[0, :, hc]
        y = o * lax.rsqrt(jnp.mean(o * o, axis=-1, keepdims=True) + NORM_EPS) * g_ref[...]
        o_ref[0, :, hc] = (y * _silu(gate_ref[0, :, hc].astype(F32))).astype(o_ref.dtype)


def deltanet_out(o_f, o_b, z, norm_g):
    b, l, _ = o_f.shape
    tl = min(512, l)
    blk = ZC_GATE // B_WIDTH
    return pl.pallas_call(
        _dn_out_kernel, grid=(b, l // tl),
        in_specs=[pl.BlockSpec((1, tl, B_WIDTH), lambda i, j: (i, j, 0)),
                  pl.BlockSpec((1, tl, B_WIDTH), lambda i, j: (i, j, 0)),
                  pl.BlockSpec((1, tl, B_WIDTH), lambda i, j: (i, j, blk)),
                  pl.BlockSpec((1, B_HEAD_DIM), lambda i, j: (0, 0))],
        out_specs=pl.BlockSpec((1, tl, B_WIDTH), lambda i, j: (i, j, 0)),
        out_shape=jax.ShapeDtypeStruct((b, l, B_WIDTH), BF16),
        compiler_params=_cparams("parallel", "parallel"), name="deltanet_out",
    )(o_f, o_b, z, norm_g.reshape(1, B_HEAD_DIM))


def _route_kernel(lg_ref, info_ref, cnt_ref, carry_ref):
    i = pl.program_id(0)

    @pl.when(i == 0)
    def _():
        carry_ref[...] = jnp.zeros_like(carry_ref)

    lg = lg_ref[...]
    tl = lg.shape[0]
    lane = lax.broadcasted_iota(jnp.int32, lg.shape, 1)
    valid = lane < N_EXPERTS
    lg = jnp.where(valid, lg, -jnp.inf)
    e = jnp.exp(lg - jnp.max(lg, axis=-1, keepdims=True))
    p = e / jnp.sum(e, axis=-1, keepdims=True)
    p = jnp.where(valid, p, -1.0)
    p1 = jnp.max(p, axis=-1, keepdims=True)
    i1 = jnp.min(jnp.where(p == p1, lane, LANE), axis=-1, keepdims=True)
    pm = jnp.where(lane == i1, -1.0, p)
    p2 = jnp.max(pm, axis=-1, keepdims=True)
    i2 = jnp.min(jnp.where(pm == p2, lane, LANE), axis=-1, keepdims=True)
    tot = p1 + p2
    w1, w2 = p1 / tot, p2 / tot
    hit1, hit2 = lane == i1, lane == i2
    onehot = (hit1 | hit2).astype(F32)
    r = lax.broadcasted_iota(jnp.int32, (tl, tl), 0)
    c = lax.broadcasted_iota(jnp.int32, (tl, tl), 1)
    before = _dot((r > c).astype(BF16), onehot.astype(BF16)) + carry_ref[...]
    r1 = jnp.sum(jnp.where(hit1, before, 0.0), axis=-1, keepdims=True)
    r2 = jnp.sum(jnp.where(hit2, before, 0.0), axis=-1, keepdims=True)
    carry_ref[...] += jnp.sum(onehot, axis=0, keepdims=True)
    cnt_ref[...] = jnp.broadcast_to(carry_ref[...], cnt_ref.shape)
    info = jnp.where(lane == 0, i1.astype(F32), 0.0)
    info = jnp.where(lane == 1, i2.astype(F32), info)
    info = jnp.where(lane == 2, r1, info)
    info = jnp.where(lane == 3, r2, info)
    info = jnp.where(lane == 4, w1, info)
    info = jnp.where(lane == 5, w2, info)
    info_ref[...] = info


def moe_route(logits):
    t = logits.shape[0]
    tl = min(512, t)
    return pl.pallas_call(
        _route_kernel, grid=(t // tl,),
        in_specs=[pl.BlockSpec((tl, LANE), lambda i: (i, 0))],
        out_specs=[pl.BlockSpec((tl, LANE), lambda i: (i, 0)), pl.BlockSpec((8, LANE), lambda i: (0, 0))],
        out_shape=[jax.ShapeDtypeStruct((t, LANE), F32), jax.ShapeDtypeStruct((8, LANE), F32)],
        scratch_shapes=[pltpu.VMEM((1, LANE), F32)],
        compiler_params=_cparams("arbitrary"), name="moe_route",
    )(logits)


def _dispatch_kernel(pos_ref, h_ref, xs_in_ref, xs_ref, sem):
    del xs_in_ref
    tb = pos_ref.shape[2] // TOP_K

    def copy(src_row, dst_row):
        return pltpu.make_async_copy(h_ref.at[pl.ds(pl.multiple_of(src_row * ROW_TILE, ROW_TILE), ROW_TILE)],
                                     xs_ref.at[pl.ds(pl.multiple_of(dst_row * ROW_TILE, ROW_TILE), ROW_TILE)], sem)

    def issue(j, carry):
        for k in range(TOP_K):
            copy(j, pos_ref[0, 0, TOP_K * j + k]).start()
        return carry

    lax.fori_loop(0, tb, issue, 0, unroll=8)
    for _ in range(TOP_K):
        pltpu.make_async_copy(h_ref, xs_ref.at[pl.ds(0, tb * ROW_TILE)], sem).wait()


def moe_dispatch(h_rows, pos, n_rows):
    t = pos.shape[0]
    tb = min(256, t)
    d_rows = h_rows.shape[0] // t
    assert d_rows == ROW_TILE
    zeros = jnp.zeros((n_rows * ROW_TILE, LANE), h_rows.dtype)
    return pl.pallas_call(
        _dispatch_kernel, grid=(t // tb,),
        in_specs=[pl.BlockSpec((1, 1, TOP_K * tb), lambda i: (i, 0, 0), memory_space=pltpu.SMEM),
                  pl.BlockSpec((tb * ROW_TILE, LANE), lambda i: (i, 0)),
                  pl.BlockSpec(memory_space=pl.ANY)],
        out_specs=pl.BlockSpec(memory_space=pl.ANY),
        out_shape=jax.ShapeDtypeStruct(zeros.shape, zeros.dtype),
        scratch_shapes=[pltpu.SemaphoreType.DMA],
        input_output_aliases={2: 0},
        compiler_params=_cparams("arbitrary"), name="moe_dispatch",
    )(pos.reshape(t // tb, 1, TOP_K * tb), h_rows, zeros)


def _moe_up_kernel(te_ref, tv_ref, xs_ref, wg_ref, wu_ref, o_ref, a_ref):
    del te_ref
    i = pl.program_id(0)
    tm = a_ref.shape[0]

    @pl.when(pl.program_id(1) == 0)
    def _():
        for c in range(ROW_TILE):
            lo, hi = _load_token_rows(xs_ref, tm, c)
            a_ref[:, c * LANE:(c + 1) * LANE] = lo.astype(a_ref.dtype)
            a_ref[:, (c + ROW_TILE) * LANE:(c + ROW_TILE + 1) * LANE] = hi.astype(a_ref.dtype)

    @pl.when(tv_ref[i] != 0)
    def _():
        a = a_ref[...]
        g = _dot(a, wg_ref[0].astype(BF16))
        u = _dot(a, wu_ref[0].astype(BF16))
        o_ref[...] = (_silu(g) * u).astype(o_ref.dtype)

    @pl.when(tv_ref[i] == 0)
    def _():
        o_ref[...] = jnp.zeros_like(o_ref)


def moe_up(xs_rows, wg, wu, tile_expert, tile_valid, tm, tn):
    r = xs_rows.shape[0] // ROW_TILE
    d, n = wg.shape[1], wg.shape[2]
    tn = _tile(n, tn)
    nj = n // tn

    def w_index(i, j, te, tv):
        return te[i], 0, jnp.where(tv[i] != 0, j, nj - 1)

    grid_spec = pltpu.PrefetchScalarGridSpec(
        num_scalar_prefetch=2, grid=(r // tm, nj),
        in_specs=[pl.BlockSpec((tm * ROW_TILE, LANE), lambda i, j, te, tv: (i, 0)),
                  pl.BlockSpec((1, d, tn), w_index),
                  pl.BlockSpec((1, d, tn), w_index)],
        out_specs=pl.BlockSpec((tm, tn), lambda i, j, te, tv: (i, j)),
        scratch_shapes=[pltpu.VMEM((tm, d), BF16)])
    return pl.pallas_call(
        _moe_up_kernel, grid_spec=grid_spec,
        out_shape=jax.ShapeDtypeStruct((r, n), BF16),
        compiler_params=_cparams("parallel", "arbitrary"), name="moe_up",
    )(tile_expert, tile_valid, xs_rows, wg, wu)


def _moe_down_kernel(te_ref, tv_ref, a_ref, w_ref, o_ref, acc_ref):
    del te_ref
    i = pl.program_id(0)
    k = pl.program_id(1)

    valid = tv_ref[i] != 0

    @pl.when(jnp.logical_and(valid, k == 0))
    def _():
        acc_ref[...] = _dot(a_ref[...], w_ref[0].astype(BF16))

    @pl.when(jnp.logical_and(valid, k > 0))
    def _():
        acc_ref[...] += _dot(a_ref[...], w_ref[0].astype(BF16))

    @pl.when(jnp.logical_and(jnp.logical_not(valid), k == 0))
    def _():
        acc_ref[...] = jnp.zeros_like(acc_ref)

    @pl.when(k == pl.num_programs(1) - 1)
    def _():
        _store_token_rows(o_ref, acc_ref[...])


def moe_down(hid, wd, tile_expert, tile_valid, tm, tk):
    r, kdim = hid.shape
    n = wd.shape[2]
    assert n == D_TOKEN
    tk = _tile(kdim, tk)
    nk = kdim // tk
    grid_spec = pltpu.PrefetchScalarGridSpec(
        num_scalar_prefetch=2, grid=(r // tm, nk),
        in_specs=[pl.BlockSpec((tm, tk), lambda i, k, te, tv: (i, jnp.where(tv[i] != 0, k, nk - 1))),
                  pl.BlockSpec((1, tk, n), lambda i, k, te, tv: (te[i], jnp.where(tv[i] != 0, k, nk - 1), 0))],
        out_specs=pl.BlockSpec((tm * ROW_TILE, LANE), lambda i, k, te, tv: (i, 0)),
        scratch_shapes=[pltpu.VMEM((tm, n), F32)])
    return pl.pallas_call(
        _moe_down_kernel, grid_spec=grid_spec,
        out_shape=jax.ShapeDtypeStruct((r * ROW_TILE, LANE), U32),
        compiler_params=_cparams("parallel", "arbitrary"), name="moe_down",
    )(tile_expert, tile_valid, hid, wd)


def _combine_kernel(pos_ref, pos_next_ref, ys_ref, x_ref, gate_ref, wt_ref, ng_ref, o_ref, buf_ref, sem, *, final_norm):
    i = pl.program_id(0)
    tb = x_ref.shape[0]
    slot = lax.rem(i, 2)

    def gather(p_ref, s):
        def issue(j, carry):
            for k in range(TOP_K):
                src_row = p_ref[0, 0, TOP_K * j + k]
                pltpu.make_async_copy(
                    ys_ref.at[pl.ds(pl.multiple_of(src_row * ROW_TILE, ROW_TILE), ROW_TILE)],
                    buf_ref.at[s, k, pl.ds(pl.multiple_of(j * ROW_TILE, ROW_TILE), ROW_TILE)], sem.at[s]).start()
            return carry

        lax.fori_loop(0, tb, issue, 0, unroll=8)

    @pl.when(i == 0)
    def _():
        gather(pos_ref, 0)

    @pl.when(i + 1 < pl.num_programs(0))
    def _():
        gather(pos_next_ref, 1 - slot)

    for k in range(TOP_K):
        pltpu.make_async_copy(ys_ref.at[pl.ds(0, tb * ROW_TILE)], buf_ref.at[slot, k], sem.at[slot]).wait()
    w0 = wt_ref[:, 0:1]
    w1 = wt_ref[:, 1:2]
    sq = jnp.zeros((tb, 1), F32)
    for c in range(ROW_TILE):
        first = _load_token_rows(buf_ref.at[slot, 0], tb, c)
        second = _load_token_rows(buf_ref.at[slot, 1], tb, c)
        for half in range(2):
            cols = slice((c + half * ROW_TILE) * LANE, (c + half * ROW_TILE + 1) * LANE)
            f = first[half] * w0 + second[half] * w1
            y = x_ref[:, cols] + gate_ref[0, :, cols] * f
            o_ref[:, cols] = y
            sq = sq + jnp.sum(y * y, axis=-1, keepdims=True)
    if final_norm:
        o_ref[...] = o_ref[...] * lax.rsqrt(sq * (1.0 / D_TOKEN) + NORM_EPS) * ng_ref[...]


def moe_combine(ys_rows, pos, wts, x, gate, rows_per_batch, final_norm_g=None):
    t, d = x.shape
    tb = min(128, t)
    bsz = gate.shape[0]
    final_norm = final_norm_g is not None
    ng = (final_norm_g if final_norm else jnp.ones((d,), F32)).reshape(1, d)
    nb = t // tb
    pos_blocks = pos.reshape(nb, 1, TOP_K * tb)
    return pl.pallas_call(
        functools.partial(_combine_kernel, final_norm=final_norm), grid=(nb,),
        in_specs=[pl.BlockSpec((1, 1, TOP_K * tb), lambda i: (i, 0, 0), memory_space=pltpu.SMEM),
                  pl.BlockSpec((1, 1, TOP_K * tb), lambda i: (jnp.minimum(i + 1, nb - 1), 0, 0),
                               memory_space=pltpu.SMEM),
                  pl.BlockSpec(memory_space=pl.ANY),
                  pl.BlockSpec((tb, d), lambda i: (i, 0)),
                  pl.BlockSpec((1, 1, d), lambda i: ((i * tb) // rows_per_batch, 0, 0)),
                  pl.BlockSpec((tb, TOP_K), lambda i: (i, 0)),
                  pl.BlockSpec((1, d), lambda i: (0, 0))],
        out_specs=pl.BlockSpec((tb, d), lambda i: (i, 0)),
        out_shape=jax.ShapeDtypeStruct((t, d), F32),
        scratch_shapes=[pltpu.VMEM((2, TOP_K, tb * ROW_TILE, LANE), U32), pltpu.SemaphoreType.DMA((2,))],
        compiler_params=_cparams("arbitrary"), name="moe_combine",
    )(pos_blocks, pos_blocks, ys_rows, x, gate.reshape(bsz, 1, d), wts, ng)


MOE_TM = 1024


def moe_ffn(h_rows, logits, wg, wu, wd, x, gate, rows_per_batch, final_norm_g=None):
    t, d = x.shape
    tm = min(MOE_TM, t)
    info, counts = moe_route(logits)
    sizes = counts[0, :N_EXPERTS].astype(jnp.int32)
    padded = ((sizes + tm - 1) // tm) * tm
    ends = jnp.cumsum(padded)
    starts = ends - padded
    n_tiles = (t * TOP_K) // tm + N_EXPERTS
    n_rows = n_tiles * tm
    experts = info[:, 0:TOP_K].astype(jnp.int32)
    pos = starts[experts] + info[:, 2:2 + TOP_K].astype(jnp.int32)
    wts = info[:, 4:4 + TOP_K]
    tile_start = jnp.arange(n_tiles, dtype=jnp.int32) * tm
    tile_expert = jnp.minimum(jnp.sum((tile_start[:, None] >= ends[None, :]).astype(jnp.int32), axis=1), N_EXPERTS - 1)
    tile_valid = (tile_start < ends[-1]).astype(jnp.int32)
    last_expert = jnp.max(jnp.where(sizes > 0, jnp.arange(N_EXPERTS, dtype=jnp.int32), 0))
    tile_expert = jnp.where(tile_valid != 0, tile_expert, last_expert)
    xs_rows = moe_dispatch(h_rows, pos, n_rows)
    hid = moe_up(xs_rows, wg, wu, tile_expert, tile_valid, tm, 512)
    ys_rows = moe_down(hid, wd.astype(BF16), tile_expert, tile_valid, tm, 1792)
    return moe_combine(ys_rows, pos, wts, x, gate, rows_per_batch, final_norm_g)


def _rot_cols(w):
    f = ROPE_AXIS_FREQS
    return jnp.concatenate([-w[:, f:2 * f], w[:, 0:f], -w[:, 3 * f:4 * f], w[:, 2 * f:3 * f]], axis=1)


def _arrange_w_in(w):
    d = w.shape[0]
    a = w[:, 0:2 * A_WIDTH]
    off = 2 * A_WIDTH
    qkv_gate = w[:, off:off + 4 * B_WIDTH]
    logit = w[:, off + 4 * B_WIDTH:off + 4 * B_WIDTH + 4 * B_HEADS]
    off = off + 4 * B_WIDTH + 4 * B_HEADS
    lat = w[:, off:off + Q_LORA + KV_LORA]
    k_pe = w[:, off + Q_LORA + KV_LORA:off + Q_LORA + KV_LORA + C_ROPE]
    z64 = jnp.zeros((d, LANE - C_ROPE), w.dtype)
    main = jnp.concatenate([qkv_gate, a, lat, k_pe, z64, _rot_cols(k_pe), z64], axis=1).astype(BF16)
    logit = jnp.pad(logit, ((0, 0), (0, LANE - 4 * B_HEADS))).astype(BF16)
    return main, logit


def _arrange_w_uq(w):
    k = w.shape[0]
    w = w.reshape(k, C_HEADS, C_NOPE + C_ROPE)
    z64 = jnp.zeros((k, C_HEADS, LANE - C_ROPE), w.dtype)
    pe = w[:, :, C_NOPE:]
    pe_rot = jnp.stack([_rot_cols(pe[:, h]) for h in range(C_HEADS)], axis=1)
    return jnp.concatenate([w[:, :, :C_NOPE], pe, z64, pe_rot, z64], axis=2).reshape(k, C_HEADS * C_HEAD_COLS).astype(BF16)


def _rope_tables(n):
    rows = n // GRID_W
    row = np.repeat(np.arange(rows, dtype=np.float32), GRID_W)
    col = np.tile(np.arange(GRID_W, dtype=np.float32), rows)
    inv = np.power(np.float32(ROPE_BASE), -np.arange(ROPE_AXIS_FREQS, dtype=np.float32) / np.float32(ROPE_AXIS_FREQS))
    ar = row[:, None] * inv.astype(np.float32)
    ac = col[:, None] * inv.astype(np.float32)
    ang = np.concatenate([ar, ar, ac, ac], axis=-1).astype(np.float32)
    pad = ((0, 0), (0, LANE - C_ROPE))
    return (jnp.asarray(np.pad(np.cos(ang).astype(np.float32), pad)),
            jnp.asarray(np.pad(np.sin(ang).astype(np.float32), pad)))


def _mixer_branches(z, zg, p, cos2, sin2, **kv_placement):
    out_a = chunk_sgu(z, p["sgu_norm_g"], p["sgu_w"], p["sgu_b"])
    qkv, gb = deltanet_prep(z, zg, p["dn_conv_w"], p["dn_a_log"], p["dn_dt_bias"])
    intra = deltanet_intra(qkv, gb)
    q, k, v = mla_prep(z, p["mla_q_norm_g"], p["mla_kv_norm_g"], p["wq_arr"], p["wkv_arr"], cos2, sin2, **kv_placement)
    return out_a, intra, (q, k, v)


def kernel(x, c, ctx, c_ctx, ada_w, ada_b, norm1_g, norm2_g, w_in, sgu_norm_g, sgu_w, sgu_b, dn_conv_w, dn_a_log, dn_dt_bias, dn_norm_g, mla_q_norm_g, mla_w_uq, mla_kv_norm_g, mla_w_ukv, w_out, ffn_w_gate, ffn_w_up, ffn_w_down, moe_router, moe_w_gate, moe_w_up, moe_w_down, final_norm_g):
    b, n, d = x.shape
    lc = ctx.shape[1]
    depth = ada_w.shape[0]
    assert d == D_TOKEN
    cos_lat, sin_lat = _rope_tables(n)
    cos_ctx = jnp.pad(jnp.ones((lc, C_ROPE), F32), ((0, 0), (0, LANE - C_ROPE)))
    sin_ctx = jnp.zeros((lc, LANE), F32)

    cc = jnp.zeros((8, d), F32).at[0:b].set(c).at[b].set(c_ctx)
    mod_all = ada_modulation(cc, ada_w, ada_b)

    xc = ctx
    x_rows = None
    for i in range(depth):
        last = i == depth - 1
        mod = mod_all[i, 0:b].reshape(b, 6, d)
        mod_c = jnp.broadcast_to(mod_all[i, b].reshape(1, 6, d), (b, 6, d))
        w_main, w_logit = _arrange_w_in(w_in[i])
        p = dict(sgu_norm_g=sgu_norm_g[i], sgu_w=sgu_w[i], sgu_b=sgu_b[i], dn_conv_w=dn_conv_w[i],
                 dn_a_log=dn_a_log[i], dn_dt_bias=dn_dt_bias[i], mla_q_norm_g=mla_q_norm_g[i],
                 mla_kv_norm_g=mla_kv_norm_g[i], wq_arr=_arrange_w_uq(mla_w_uq[i]),
                 wkv_arr=mla_w_ukv[i].astype(BF16))
        w_out_b = w_out[i].astype(BF16)

        h = norm_mod(x, norm1_g[i], mod[:, 0], mod[:, 1]).reshape(b * n, d)
        hc = norm_mod(xc, norm1_g[i], mod_c[:, 0], mod_c[:, 1]).reshape(b * lc, d)
        z = matmul(h, w_main, BF16, 1024, 1024).reshape(b, n, Z_COLS)
        zg = matmul(h, w_logit, F32, 1024, LANE).reshape(b, n, LANE)
        zc = matmul(hc, w_main, BF16, 1024, 1024).reshape(b, lc, Z_COLS)
        zgc = matmul(hc, w_logit, F32, 1024, LANE).reshape(b, lc, LANE)

        kv_zero = (jnp.zeros((b, C_HEADS, n + lc, 2 * LANE), BF16), jnp.zeros((b, C_HEADS, n + lc, LANE), BF16))
        oa_c, intra_c, (q_c, k_all, v_all) = _mixer_branches(zc, zgc, p, cos_ctx, sin_ctx, kv_rows=n + lc, kv_row0=n,
                                                             kv_into=kv_zero)
        out_a, intra, (q_l, k_all, v_all) = _mixer_branches(z, zg, p, cos_lat, sin_lat, kv_rows=n + lc, kv_row0=0,
                                                            kv_into=(k_all, v_all))
        s_zero = jnp.zeros((b, 2 * B_HEADS, B_HEAD_DIM, B_HEAD_DIM), F32)
        ocf, ocb, s_ctx = deltanet_scan(*intra_c, s_zero)
        o_f, o_b, _ = deltanet_scan(*intra, s_ctx)
        out_b = deltanet_out(o_f, o_b, z, dn_norm_g[i])
        out_c = flash_attention(q_l, k_all, v_all)
        x2 = mix_out_residual(out_a.reshape(b * n, -1), out_b.reshape(b * n, -1), out_c.reshape(b * n, -1),
                              w_out_b, x.reshape(b * n, d), mod[:, 2], n, 1024, 1024)
        if not last:
            ob_c = deltanet_out(ocf, ocb, zc, dn_norm_g[i])
            oc_c = flash_attention(q_c, k_all[:, :, n:], v_all[:, :, n:])
            xc2 = mix_out_residual(oa_c.reshape(b * lc, -1), ob_c.reshape(b * lc, -1), oc_c.reshape(b * lc, -1),
                                   w_out_b, xc.reshape(b * lc, d), mod_c[:, 2], lc, 1024, 1024)

        if i % 2 == 0:
            wg, wu, wd = (ffn_w_gate[i // 2].astype(BF16), ffn_w_up[i // 2].astype(BF16), ffn_w_down[i // 2].astype(BF16))
            h2 = norm_mod(x2.reshape(b, n, d), norm2_g[i], mod[:, 3], mod[:, 4]).reshape(b * n, d)
            hid = swiglu_up(h2, wg, wu, 1024, 512)
            x = down_residual(hid, wd, x2, mod[:, 5], n, 1024, 512, wd.shape[0]).reshape(b, n, d)
            if not last:
                hc2 = norm_mod(xc2.reshape(b, lc, d), norm2_g[i], mod_c[:, 3], mod_c[:, 4]).reshape(b * lc, d)
                hid_c = swiglu_up(hc2, wg, wu, 1024, 512)
                xc = down_residual(hid_c, wd, xc2, mod_c[:, 5], lc, 1024, 512, wd.shape[0]).reshape(b, lc, d)
        else:
            e = i // 2
            router = jnp.pad(moe_router[e], ((0, 0), (0, LANE - N_EXPERTS)))
            if not last:
                raise NotImplementedError("an expert layer followed by another layer is not part of this model")
            h_rows, logits = norm_mod(x2.reshape(b, n, d), norm2_g[i], mod[:, 3], mod[:, 4], router=router)
            return moe_ffn(h_rows, logits.reshape(b * n, LANE), moe_w_gate[e], moe_w_up[e], moe_w_down[e],
                           x2, mod[:, 5], n, final_norm_g).reshape(b, n, d)
    return rmsnorm_rows(x.reshape(b * n, d), final_norm_g).reshape(b, n, d)
```

```python
import functools
import math

import jax
import jax.numpy as jnp
import numpy as np
from jax import lax
from jax.experimental import pallas as pl
from jax.experimental.pallas import tpu as pltpu

F32 = jnp.float32
BF16 = jnp.bfloat16
HIGHEST = lax.Precision.HIGHEST

NORM_EPS = 1e-6
GRID_W = 64
A_GROUPS = 4
A_GROUP_DIM = 128
A_WIDTH = A_GROUPS * A_GROUP_DIM
A_CHUNK = 128
B_HEADS = 6
B_HEAD_DIM = 128
B_WIDTH = B_HEADS * B_HEAD_DIM
DN_CHUNK = 64
DN_CONV = 5
C_HEADS = 6
C_NOPE = 128
C_ROPE = 64
C_V = 128
C_WIDTH = C_HEADS * C_V
Q_LORA = 512
KV_LORA = 256
ROPE_BASE = 10000.0
ROPE_AXIS_FREQS = C_ROPE // 4
N_EXPERTS = 8
TOP_K = 2

LANE = 128
ROW_TILE = 8
D_TOKEN = 2 * ROW_TILE * LANE
U32 = jnp.uint32
VMEM_LIMIT = 56 * 1024 * 1024

ZC_QKV = 0
ZC_GATE = 3 * B_WIDTH
ZC_A = 4 * B_WIDTH
ZC_C = 4 * B_WIDTH + 2 * A_WIDTH
Z_COLS = ZC_C + 1024
C_HEAD_COLS = 3 * LANE


def _cparams(*sem):
    return pltpu.CompilerParams(dimension_semantics=sem, vmem_limit_bytes=VMEM_LIMIT)


def _tile(n, pref):
    if n <= pref:
        return n
    t = (pref // LANE) * LANE
    while n % t:
        t -= LANE
    return t


def _sigmoid(x):
    return 1.0 / (1.0 + jnp.exp(-x))


def _silu(x):
    return x * _sigmoid(x)


def _dot(a, b):
    return jnp.dot(a, b, preferred_element_type=F32)


def _dot_nt(a, b):
    return lax.dot_general(a, b, (((1,), (1,)), ((), ())), preferred_element_type=F32)


def _dot_tn(a, b):
    return lax.dot_general(a, b, (((0,), (0,)), ((), ())), preferred_element_type=F32)


def _split_bf16(a):
    hi = a.astype(BF16)
    lo = (a - hi.astype(F32)).astype(BF16)
    return hi, lo


def _dot_x3(a, b):
    ah, al = _split_bf16(a)
    bh, bl = _split_bf16(b)
    return _dot(ah, bh) + (_dot(ah, bl) + _dot(al, bh))


def _ada_kernel(c_ref, w_ref, b_ref, o_ref):
    c = c_ref[...]
    o_ref[0] = jnp.dot(_silu(c), w_ref[0], preferred_element_type=F32, precision=HIGHEST) + b_ref[0]


def ada_modulation(cc, ada_w, ada_b):
    depth, d, n = ada_w.shape
    tn = 1024
    return pl.pallas_call(
        _ada_kernel,
        grid=(depth, n // tn),
        in_specs=[
            pl.BlockSpec((8, d), lambda i, j: (0, 0)),
            pl.BlockSpec((1, d, tn), lambda i, j: (i, 0, j)),
            pl.BlockSpec((1, 1, tn), lambda i, j: (i, 0, j)),
        ],
        out_specs=pl.BlockSpec((1, 8, tn), lambda i, j: (i, 0, j)),
        out_shape=jax.ShapeDtypeStruct((depth, 8, n), F32),
        compiler_params=_cparams("parallel", "parallel"),
        name="ada_modulation",
    )(cc, ada_w, ada_b.reshape(depth, 1, n))


def _norm_mod_kernel(x_ref, g_ref, sh_ref, sc_ref, o_ref):
    x = x_ref[0]
    y = x * lax.rsqrt(jnp.mean(x * x, axis=-1, keepdims=True) + NORM_EPS) * g_ref[...]
    o_ref[0] = (y * (1.0 + sc_ref[0]) + sh_ref[0]).astype(o_ref.dtype)


def _norm_mod_router_kernel(x_ref, g_ref, sh_ref, sc_ref, r_ref, o_ref, lg_ref):
    x = x_ref[0]
    y = x * lax.rsqrt(jnp.mean(x * x, axis=-1, keepdims=True) + NORM_EPS) * g_ref[...]
    h = y * (1.0 + sc_ref[0]) + sh_ref[0]
    _store_token_rows(o_ref, h)
    lg_ref[0] = _dot_x3(h, r_ref[...])


def _store_token_rows(rows_ref, x):
    n = x.shape[0]
    for c in range(ROW_TILE):
        lo = lax.bitcast_convert_type(x[:, c * LANE:(c + 1) * LANE].astype(BF16).astype(F32), U32)
        hi = lax.bitcast_convert_type(x[:, (c + ROW_TILE) * LANE:(c + ROW_TILE + 1) * LANE].astype(BF16).astype(F32), U32)
        rows_ref[pl.ds(c, n, stride=ROW_TILE), :] = lax.shift_right_logical(lo, jnp.uint32(16)) | (hi & jnp.uint32(0xFFFF0000))


def _load_token_rows(rows_ref, n, c):
    w = rows_ref[pl.ds(c, n, stride=ROW_TILE), :]
    lo = lax.bitcast_convert_type(lax.shift_left(w, jnp.uint32(16)), F32)
    hi = lax.bitcast_convert_type(w & jnp.uint32(0xFFFF0000), F32)
    return lo, hi


def norm_mod(x, g, shift, scale, out_dtype=BF16, router=None):
    b, l, d = x.shape
    tl = min(512, l)
    in_specs = [
        pl.BlockSpec((1, tl, d), lambda i, j: (i, j, 0)),
        pl.BlockSpec((1, d), lambda i, j: (0, 0)),
        pl.BlockSpec((1, 1, d), lambda i, j: (i, 0, 0)),
        pl.BlockSpec((1, 1, d), lambda i, j: (i, 0, 0)),
    ]
    args = [x, g.reshape(1, d), shift.reshape(b, 1, d), scale.reshape(b, 1, d)]
    h_spec = pl.BlockSpec((1, tl, d), lambda i, j: (i, j, 0))
    h_shape = jax.ShapeDtypeStruct((b, l, d), out_dtype)
    if router is None:
        return pl.pallas_call(
            _norm_mod_kernel, grid=(b, l // tl), in_specs=in_specs, out_specs=h_spec, out_shape=h_shape,
            compiler_params=_cparams("parallel", "parallel"), name="norm_mod",
        )(*args)
    assert d == D_TOKEN
    in_specs.append(pl.BlockSpec((d, LANE), lambda i, j: (0, 0)))
    nl = l // tl
    return pl.pallas_call(
        _norm_mod_router_kernel, grid=(b, nl), in_specs=in_specs,
        out_specs=[pl.BlockSpec((tl * ROW_TILE, LANE), lambda i, j: (i * nl + j, 0)),
                   pl.BlockSpec((1, tl, LANE), lambda i, j: (i, j, 0))],
        out_shape=[jax.ShapeDtypeStruct((b * l * ROW_TILE, LANE), U32), jax.ShapeDtypeStruct((b, l, LANE), F32)],
        compiler_params=_cparams("parallel", "parallel"), name="norm_mod_router",
    )(*args, router)


def _rmsnorm_kernel(x_ref, g_ref, o_ref):
    x = x_ref[...]
    o_ref[...] = x * lax.rsqrt(jnp.mean(x * x, axis=-1, keepdims=True) + NORM_EPS) * g_ref[...]


def rmsnorm_rows(x, g):
    m, d = x.shape
    tm = min(512, m)
    return pl.pallas_call(
        _rmsnorm_kernel, grid=(m // tm,),
        in_specs=[pl.BlockSpec((tm, d), lambda i: (i, 0)), pl.BlockSpec((1, d), lambda i: (0, 0))],
        out_specs=pl.BlockSpec((tm, d), lambda i: (i, 0)),
        out_shape=jax.ShapeDtypeStruct((m, d), F32),
        compiler_params=_cparams("parallel"), name="final_rmsnorm",
    )(x, g.reshape(1, d))


def _mm_kernel(a_ref, w_ref, o_ref):
    o_ref[...] = _dot(a_ref[...], w_ref[...]).astype(o_ref.dtype)


def matmul(a, w, out_dtype, tm, tn):
    m, k = a.shape
    n = w.shape[1]
    tm, tn = _tile(m, tm), _tile(n, tn)
    return pl.pallas_call(
        _mm_kernel, grid=(m // tm, n // tn),
        in_specs=[pl.BlockSpec((tm, k), lambda i, j: (i, 0)), pl.BlockSpec((k, tn), lambda i, j: (0, j))],
        out_specs=pl.BlockSpec((tm, tn), lambda i, j: (i, j)),
        out_shape=jax.ShapeDtypeStruct((m, n), out_dtype),
        compiler_params=_cparams("parallel", "parallel"), name="matmul",
    )(a, w)


def _swiglu_up_kernel(a_ref, wg_ref, wu_ref, o_ref):
    a = a_ref[...]
    g = _dot(a, wg_ref[...])
    u = _dot(a, wu_ref[...])
    o_ref[...] = (_silu(g) * u).astype(o_ref.dtype)


def swiglu_up(a, wg, wu, tm, tn):
    m, k = a.shape
    n = wg.shape[1]
    tm, tn = _tile(m, tm), _tile(n, tn)
    return pl.pallas_call(
        _swiglu_up_kernel, grid=(m // tm, n // tn),
        in_specs=[pl.BlockSpec((tm, k), lambda i, j: (i, 0)),
                  pl.BlockSpec((k, tn), lambda i, j: (0, j)),
                  pl.BlockSpec((k, tn), lambda i, j: (0, j))],
        out_specs=pl.BlockSpec((tm, tn), lambda i, j: (i, j)),
        out_shape=jax.ShapeDtypeStruct((m, n), BF16),
        compiler_params=_cparams("parallel", "parallel"), name="swiglu_up",
    )(a, wg, wu)


def _down_res_kernel(a_ref, w_ref, x_ref, gate_ref, o_ref, acc_ref):
    k = pl.program_id(2)

    @pl.when(k == 0)
    def _():
        acc_ref[...] = _dot(a_ref[...], w_ref[...])

    @pl.when(k > 0)
    def _():
        acc_ref[...] += _dot(a_ref[...], w_ref[...])

    @pl.when(k == pl.num_programs(2) - 1)
    def _():
        o_ref[...] = x_ref[...] + gate_ref[0] * acc_ref[...]


def down_residual(a, w, x, gate, rows_per_batch, tm, tn, tk):
    m, kdim = a.shape
    n = w.shape[1]
    tm, tn, tk = _tile(rows_per_batch, tm), _tile(n, tn), _tile(kdim, tk)
    bsz = gate.shape[0]
    return pl.pallas_call(
        _down_res_kernel, grid=(m // tm, n // tn, kdim // tk),
        in_specs=[pl.BlockSpec((tm, tk), lambda i, j, k: (i, k)),
                  pl.BlockSpec((tk, tn), lambda i, j, k: (k, j)),
                  pl.BlockSpec((tm, tn), lambda i, j, k: (i, j)),
                  pl.BlockSpec((1, 1, tn), lambda i, j, k: ((i * tm) // rows_per_batch, 0, j))],
        out_specs=pl.BlockSpec((tm, tn), lambda i, j, k: (i, j)),
        out_shape=jax.ShapeDtypeStruct((m, n), F32),
        scratch_shapes=[pltpu.VMEM((tm, tn), F32)],
        compiler_params=_cparams("parallel", "parallel", "arbitrary"), name="down_residual",
    )(a, w, x, gate.reshape(bsz, 1, n))


def _mix_out_kernel(a_ref, b_ref, c_ref, w_ref, x_ref, gate_ref, o_ref):
    ka, kb = a_ref.shape[1], b_ref.shape[1]
    acc = _dot(a_ref[...], w_ref[0:ka, :])
    acc += _dot(b_ref[...], w_ref[ka:ka + kb, :])
    acc += _dot(c_ref[...], w_ref[ka + kb:, :])
    o_ref[...] = x_ref[...] + gate_ref[0] * acc


def mix_out_residual(oa, ob, oc, w, x, gate, rows_per_batch, tm, tn):
    m = oa.shape[0]
    kdim, n = w.shape
    tm, tn = _tile(rows_per_batch, tm), _tile(n, tn)
    bsz = gate.shape[0]
    return pl.pallas_call(
        _mix_out_kernel, grid=(m // tm, n // tn),
        in_specs=[pl.BlockSpec((tm, oa.shape[1]), lambda i, j: (i, 0)),
                  pl.BlockSpec((tm, ob.shape[1]), lambda i, j: (i, 0)),
                  pl.BlockSpec((tm, oc.shape[1]), lambda i, j: (i, 0)),
                  pl.BlockSpec((kdim, tn), lambda i, j: (0, j)),
                  pl.BlockSpec((tm, tn), lambda i, j: (i, j)),
                  pl.BlockSpec((1, 1, tn), lambda i, j: ((i * tm) // rows_per_batch, 0, j))],
        out_specs=pl.BlockSpec((tm, tn), lambda i, j: (i, j)),
        out_shape=jax.ShapeDtypeStruct((m, n), F32),
        compiler_params=_cparams("parallel", "parallel"), name="mix_out_residual",
    )(oa, ob, oc, w, x, gate.reshape(bsz, 1, n))


def _gelu_tanh(x):
    return 0.5 * x * (1.0 + jnp.tanh(math.sqrt(2.0 / math.pi) * (x + 0.044715 * (x * x * x))))


def _sgu_kernel(z_ref, g_ref, w_ref, b_ref, o_ref):
    tl = z_ref.shape[1]
    for c in range(tl // A_CHUNK):
        rows = slice(c * A_CHUNK, (c + 1) * A_CHUNK)
        for g in range(A_GROUPS):
            cols = slice(g * A_GROUP_DIM, (g + 1) * A_GROUP_DIM)
            u = _gelu_tanh(z_ref[0, rows, cols].astype(F32))
            v = _gelu_tanh(z_ref[0, rows, A_WIDTH + g * A_GROUP_DIM:A_WIDTH + (g + 1) * A_GROUP_DIM].astype(F32))
            vc = v - jnp.mean(v, axis=-1, keepdims=True)
            vn = vc * lax.rsqrt(jnp.mean(vc * vc, axis=-1, keepdims=True) + NORM_EPS) * g_ref[:, cols]
            mixed = _dot(w_ref[g], vn.astype(BF16)) + b_ref[g]
            o_ref[0, rows, cols] = (u * mixed).astype(o_ref.dtype)


def chunk_sgu(z, norm_g, w_s, b_s):
    b, l, _ = z.shape
    tl = min(512, l)
    blk = ZC_A // (2 * A_WIDTH)
    return pl.pallas_call(
        _sgu_kernel, grid=(b, l // tl),
        in_specs=[pl.BlockSpec((1, tl, 2 * A_WIDTH), lambda i, j: (i, j, blk)),
                  pl.BlockSpec((1, A_WIDTH), lambda i, j: (0, 0)),
                  pl.BlockSpec((A_GROUPS, A_CHUNK, A_CHUNK), lambda i, j: (0, 0, 0)),
                  pl.BlockSpec((A_GROUPS, A_CHUNK, 1), lambda i, j: (0, 0, 0))],
        out_specs=pl.BlockSpec((1, tl, A_WIDTH), lambda i, j: (i, j, 0)),
        out_shape=jax.ShapeDtypeStruct((b, l, A_WIDTH), BF16),
        compiler_params=_cparams("parallel", "parallel"), name="chunk_sgu",
    )(z, norm_g.reshape(1, A_WIDTH), w_s.astype(BF16), b_s.reshape(A_GROUPS, A_CHUNK, 1))


def _mla_prep_kernel(z_ref, qg_ref, kvg_ref, wq_ref, wkv_ref, cos_ref, sin_ref, *rest, q_scale):
    q_ref, k_ref, v_ref = rest[-3:]
    lat = z_ref[0, :, 0:Q_LORA].astype(F32)
    cq = (lat * lax.rsqrt(jnp.mean(lat * lat, axis=-1, keepdims=True) + NORM_EPS) * qg_ref[...]).astype(BF16)
    kvl = z_ref[0, :, Q_LORA:Q_LORA + KV_LORA].astype(F32)
    ckv = (kvl * lax.rsqrt(jnp.mean(kvl * kvl, axis=-1, keepdims=True) + NORM_EPS) * kvg_ref[...]).astype(BF16)
    cos = cos_ref[...]
    sin = sin_ref[...]
    base = Q_LORA + KV_LORA
    k_pe = z_ref[0, :, base:base + LANE].astype(F32) * cos + z_ref[0, :, base + LANE:base + 2 * LANE].astype(F32) * sin
    k_pe = k_pe.astype(k_ref.dtype)
    for h in range(C_HEADS):
        qh = _dot(cq, wq_ref[:, h * C_HEAD_COLS:(h + 1) * C_HEAD_COLS])
        q_pe = qh[:, LANE:2 * LANE] * cos + qh[:, 2 * LANE:3 * LANE] * sin
        q_ref[0, h, :, 0:LANE] = (qh[:, 0:LANE] * q_scale).astype(q_ref.dtype)
        q_ref[0, h, :, LANE:2 * LANE] = (q_pe * q_scale).astype(q_ref.dtype)
        kv = _dot(ckv, wkv_ref[:, h * 2 * LANE:(h + 1) * 2 * LANE])
        k_ref[0, h, :, 0:LANE] = kv[:, 0:LANE].astype(k_ref.dtype)
        k_ref[0, h, :, LANE:2 * LANE] = k_pe
        v_ref[0, h] = kv[:, LANE:2 * LANE].astype(v_ref.dtype)


def mla_prep(z, q_norm_g, kv_norm_g, wq_arr, wkv_arr, cos2, sin2, kv_rows=None, kv_row0=0, kv_into=None):
    b, l, _ = z.shape
    tl = min(256, l)
    blk = ZC_C // 1024
    kv_rows = l if kv_rows is None else kv_rows
    assert kv_row0 % tl == 0
    blk0 = kv_row0 // tl
    q_scale = (C_NOPE + C_ROPE) ** -0.5 * math.log2(math.e)
    in_specs = [pl.BlockSpec((1, tl, 1024), lambda i, j: (i, j, blk)),
                pl.BlockSpec((1, Q_LORA), lambda i, j: (0, 0)),
                pl.BlockSpec((1, KV_LORA), lambda i, j: (0, 0)),
                pl.BlockSpec((Q_LORA, C_HEADS * C_HEAD_COLS), lambda i, j: (0, 0)),
                pl.BlockSpec((KV_LORA, C_HEADS * 2 * LANE), lambda i, j: (0, 0)),
                pl.BlockSpec((tl, LANE), lambda i, j: (j, 0)),
                pl.BlockSpec((tl, LANE), lambda i, j: (j, 0))]
    args = [z, q_norm_g.reshape(1, Q_LORA), kv_norm_g.reshape(1, KV_LORA), wq_arr, wkv_arr, cos2, sin2]
    aliases = {}
    if kv_into is not None:
        in_specs += [pl.BlockSpec(memory_space=pl.ANY), pl.BlockSpec(memory_space=pl.ANY)]
        aliases = {len(args): 1, len(args) + 1: 2}
        args += list(kv_into)
    return pl.pallas_call(
        functools.partial(_mla_prep_kernel, q_scale=q_scale), grid=(b, l // tl),
        in_specs=in_specs,
        out_specs=[pl.BlockSpec((1, C_HEADS, tl, 2 * LANE), lambda i, j: (i, 0, j, 0)),
                   pl.BlockSpec((1, C_HEADS, tl, 2 * LANE), lambda i, j: (i, 0, blk0 + j, 0)),
                   pl.BlockSpec((1, C_HEADS, tl, LANE), lambda i, j: (i, 0, blk0 + j, 0))],
        out_shape=[jax.ShapeDtypeStruct((b, C_HEADS, l, 2 * LANE), BF16),
                   jax.ShapeDtypeStruct((b, C_HEADS, kv_rows, 2 * LANE), BF16),
                   jax.ShapeDtypeStruct((b, C_HEADS, kv_rows, LANE), BF16)],
        input_output_aliases=aliases,
        compiler_params=_cparams("parallel", "parallel"), name="mla_prep",
    )(*args)


FLASH_ROWS = 32


def _flash_kernel(q_ref, k_ref, v_ref, o_ref, s_ref, p_ref, m_ref, l_ref, acc_ref, *, tk, nk):
    tq = q_ref.shape[2]
    rb = min(FLASH_ROWS, tq)
    m_ref[...] = jnp.full(m_ref.shape, -jnp.inf, F32)
    l_ref[...] = jnp.zeros(l_ref.shape, F32)
    acc_ref[...] = jnp.zeros(acc_ref.shape, F32)

    for j in range(nk):
        s_buf, p_buf = s_ref.at[j % 2], p_ref.at[j % 2]
        keys = slice(j * tk, (j + 1) * tk)
        s_buf[...] = _dot_nt(q_ref[0, 0], k_ref[0, 0, keys, :])
        for r in range(tq // rb):
            rows = slice(r * rb, (r + 1) * rb)
            blocks = [s_buf[rows, t * LANE:(t + 1) * LANE] for t in range(tk // LANE)]
            mx = blocks[0]
            for blk in blocks[1:]:
                mx = jnp.maximum(mx, blk)
            m_old = m_ref[rows, :]
            m_new = jnp.maximum(m_old, jnp.broadcast_to(jnp.max(mx, axis=-1, keepdims=True), (rb, LANE)))
            alpha = jnp.exp2(m_old - m_new)
            lane_sum = None
            for t, blk in enumerate(blocks):
                p = jnp.exp2(blk - m_new)
                lane_sum = p if lane_sum is None else lane_sum + p
                p_buf[rows, t * LANE:(t + 1) * LANE] = p.astype(BF16)
            l_ref[rows, :] = alpha * l_ref[rows, :] + lane_sum
            m_ref[rows, :] = m_new
            acc_ref[rows, :] = alpha * acc_ref[rows, :]
        acc_ref[...] += _dot(p_buf[...], v_ref[0, 0, keys, :])
    o_ref[0] = (acc_ref[...] / jnp.sum(l_ref[...], axis=-1, keepdims=True)).astype(o_ref.dtype)


def flash_attention(q, k, v):
    b, h, lq, dq = q.shape
    lk = k.shape[2]
    tq = min(512, lq)
    tk = _tile(lk, 256)
    return pl.pallas_call(
        functools.partial(_flash_kernel, tk=tk, nk=lk // tk), grid=(b, h, lq // tq),
        in_specs=[pl.BlockSpec((1, 1, tq, dq), lambda i, j, t: (i, j, t, 0)),
                  pl.BlockSpec((1, 1, lk, dq), lambda i, j, t: (i, j, 0, 0)),
                  pl.BlockSpec((1, 1, lk, C_V), lambda i, j, t: (i, j, 0, 0))],
        out_specs=pl.BlockSpec((1, tq, C_V), lambda i, j, t: (i, t, j)),
        out_shape=jax.ShapeDtypeStruct((b, lq, h * C_V), BF16),
        scratch_shapes=[pltpu.VMEM((2, tq, tk), F32), pltpu.VMEM((2, tq, tk), BF16), pltpu.VMEM((tq, LANE), F32),
                        pltpu.VMEM((tq, LANE), F32), pltpu.VMEM((tq, C_V), F32)],
        compiler_params=_cparams("parallel", "parallel", "parallel"), name="flash_attention",
    )(q, k, v)


def _softplus(x):
    return jnp.maximum(x, 0.0) + jnp.log1p(jnp.exp(-jnp.abs(x)))


def _dn_prep_kernel(zm_ref, zp_ref, zn_ref, zg_ref, w_ref, alog_ref, dtb_ref, qkv_ref, gb_ref):
    j = pl.program_id(1)
    tl = zm_ref.shape[1]
    half = DN_CONV // 2
    keep_prev = (j > 0).astype(F32)
    keep_next = (j < pl.num_programs(1) - 1).astype(F32)
    for c in range(3 * B_HEADS):
        cols = slice(c * LANE, (c + 1) * LANE)
        prev = zp_ref[0, :, cols].astype(F32)[8:16] * keep_prev
        nxt = zn_ref[0, :, cols].astype(F32)[0:8] * keep_next
        ext = jnp.concatenate([prev, zm_ref[0, :, cols].astype(F32), nxt], axis=0)
        y = ext[8 - half:8 - half + tl] * w_ref[0:1, cols]
        for i in range(1, DN_CONV):
            y = y + ext[8 - half + i:8 - half + i + tl] * w_ref[i:i + 1, cols]
        y = _silu(y)
        if c < 2 * B_HEADS:
            y = y * lax.rsqrt(jnp.sum(y * y, axis=-1, keepdims=True) + NORM_EPS)
        qkv_ref[0, :, cols] = y.astype(qkv_ref.dtype)
    zg = zg_ref[0]
    lane = lax.broadcasted_iota(jnp.int32, zg.shape, 1)
    g = -jnp.exp(alog_ref[...]) * _softplus(zg + dtb_ref[...])
    gb_ref[0] = jnp.where(lane < 2 * B_HEADS, g, _sigmoid(zg))


def deltanet_prep(z, zg, conv_w, a_log, dt_bias):
    b, l, _ = z.shape
    tl = min(256, l)
    wq = 3 * B_WIDTH
    nb16 = l // 16
    pad = LANE - 2 * B_HEADS
    alog = jnp.pad(a_log.reshape(1, -1), ((0, 0), (0, pad)))
    dtb = jnp.pad(dt_bias.reshape(1, -1), ((0, 0), (0, pad)))
    return pl.pallas_call(
        _dn_prep_kernel, grid=(b, l // tl),
        in_specs=[pl.BlockSpec((1, tl, wq), lambda i, j: (i, j, 0)),
                  pl.BlockSpec((1, 16, wq), lambda i, j: (i, jnp.maximum(j * (tl // 16) - 1, 0), 0)),
                  pl.BlockSpec((1, 16, wq), lambda i, j: (i, jnp.minimum((j + 1) * (tl // 16), nb16 - 1), 0)),
                  pl.BlockSpec((1, tl, LANE), lambda i, j: (i, j, 0)),
                  pl.BlockSpec((DN_CONV, wq), lambda i, j: (0, 0)),
                  pl.BlockSpec((1, LANE), lambda i, j: (0, 0)),
                  pl.BlockSpec((1, LANE), lambda i, j: (0, 0))],
        out_specs=[pl.BlockSpec((1, tl, wq), lambda i, j: (i, j, 0)),
                   pl.BlockSpec((1, tl, LANE), lambda i, j: (i, j, 0))],
        out_shape=[jax.ShapeDtypeStruct((b, l, wq), BF16), jax.ShapeDtypeStruct((b, l, LANE), F32)],
        compiler_params=_cparams("parallel", "parallel"), name="deltanet_prep",
    )(z, z, z, zg, conv_w, alog, dtb)


def _dot_bf16(a, b):
    return _dot(a.astype(BF16), b.astype(BF16))


def _dn_intra_kernel(qkv_ref, gb_ref, u_ref, w_ref, qg_ref, kg_ref, qk_ref, gl_ref, *, chunks, passes):
    c = DN_CHUNK
    row = lax.broadcasted_iota(jnp.int32, (c, LANE), 0)
    lane = lax.broadcasted_iota(jnp.int32, (c, LANE), 1)
    col = jnp.where(lane < c, lane, lane - c)
    fwd = lane < c
    bwd = jnp.logical_not(fwd)
    incl = (fwd & (row >= col)) | (bwd & (row <= col))
    strict = (fwd & (row > col)) | (bwd & (row < col))
    same16 = lax.shift_right_logical(row, 4) == lax.shift_right_logical(col, 4)
    same32 = lax.shift_right_logical(row, 5) == lax.shift_right_logical(col, 5)
    diag = row == col
    eye2 = diag.astype(F32)
    r64 = lax.broadcasted_iota(jnp.int32, (c, c), 0)
    c64 = lax.broadcasted_iota(jnp.int32, (c, c), 1)
    tri_lo = (r64 >= c64).astype(F32)
    tri_up = (r64 <= c64).astype(F32)
    scale = B_HEAD_DIM ** -0.5
    dotp = _dot_x3 if passes == 3 else _dot_bf16

    def pick(x, idx):
        return jnp.sum(jnp.where(lane == idx, x, 0.0), axis=-1, keepdims=True)

    def blockdiag(y2):
        return jnp.concatenate([jnp.where(fwd, y2, 0.0), jnp.where(fwd, 0.0, y2)], axis=0)

    def mm(xs, ys):
        return [dotp(x2, blockdiag(y2)) for x2, y2 in zip(xs, ys)]

    rows = [slice(ci * c, (ci + 1) * c) for ci in range(chunks)]
    gb = [gb_ref[0, rs, :] for rs in rows]
    cum_f = [jnp.dot(tri_lo, g, preferred_element_type=F32, precision=HIGHEST) for g in gb]
    cum_b = [jnp.dot(tri_up, g, preferred_element_type=F32, precision=HIGHEST) for g in gb]
    tot_f = [x[c - 1:c, :] for x in cum_f]
    tot_b = [x[0:1, :] for x in cum_b]

    units = [(ci, h) for ci in range(chunks) for h in range(B_HEADS)]
    q = [qkv_ref[0, rows[ci], h * LANE:(h + 1) * LANE] for ci, h in units]
    k = [qkv_ref[0, rows[ci], B_WIDTH + h * LANE:B_WIDTH + (h + 1) * LANE] for ci, h in units]
    v = [qkv_ref[0, rows[ci], 2 * B_WIDTH + h * LANE:2 * B_WIDTH + (h + 1) * LANE].astype(F32) for ci, h in units]
    k2 = [jnp.concatenate([x, x], axis=0) for x in k]
    kk2 = [_dot_nt(x, y) for x, y in zip(k, k2)]
    qk2 = [_dot_nt(x, y) for x, y in zip(q, k2)]
    cf = [pick(cum_f[ci], h) for ci, h in units]
    cb = [pick(cum_b[ci], B_HEADS + h) for ci, h in units]
    bf = [pick(gb[ci], 2 * B_HEADS + h) for ci, h in units]
    bb = [pick(gb[ci], 3 * B_HEADS + h) for ci, h in units]
    lf = [pick(jnp.broadcast_to(tot_f[ci], (c, LANE)), h) for ci, h in units]
    lb = [pick(jnp.broadcast_to(tot_b[ci], (c, LANE)), B_HEADS + h) for ci, h in units]
    c2 = [jnp.where(fwd, x, y) for x, y in zip(cf, cb)]
    r2 = [jnp.sum(jnp.where(diag, x, 0.0), axis=0, keepdims=True) for x in c2]
    decay2 = [jnp.where(incl, jnp.exp(jnp.where(incl, x - y, 0.0)), 0.0) for x, y in zip(c2, r2)]
    l2 = [jnp.where(strict, jnp.where(fwd, x, y) * kk * dc, 0.0) for x, y, kk, dc in zip(bf, bb, kk2, decay2)]
    mp = [jnp.where(same16, -x, 0.0) for x in l2]
    p = [eye2 + x for x in mp]
    for _ in range(3):
        mp = mm(mp, mp)
        p = [x + y for x, y in zip(p, mm(p, mp))]
    off = [jnp.where(same32 & jnp.logical_not(same16), x, 0.0) for x in l2]
    p = [x - y for x, y in zip(p, mm(mm(p, off), p))]
    off = [jnp.where(same32, 0.0, x) for x in l2]
    p = [x - y for x, y in zip(p, mm(mm(p, off), p))]
    ef = [jnp.exp(x) for x in cf]
    eb = [jnp.exp(x) for x in cb]
    zero = jnp.zeros((c, LANE), F32)
    sol = []
    for i in range(len(units)):
        kf = k[i].astype(F32)
        rhs = jnp.concatenate([
            jnp.concatenate([v[i] * bf[i], kf * (bf[i] * ef[i]), zero, zero], axis=1),
            jnp.concatenate([zero, zero, v[i] * bb[i], kf * (bb[i] * eb[i])], axis=1)], axis=0)
        sol.append(dotp(p[i], rhs))
    for i, (ci, h) in enumerate(units):
        hc = slice(h * LANE, (h + 1) * LANE)
        rs = rows[ci]
        kf = k[i].astype(F32)
        qf = q[i].astype(F32)
        u_ref[0, 0, rs, hc] = sol[i][:, 0:LANE]
        w_ref[0, 0, rs, hc] = sol[i][:, LANE:2 * LANE].astype(w_ref.dtype)
        u_ref[1, 0, rs, hc] = sol[i][:, 2 * LANE:3 * LANE]
        w_ref[1, 0, rs, hc] = sol[i][:, 3 * LANE:4 * LANE].astype(w_ref.dtype)
        qg_ref[0, 0, rs, hc] = (qf * (ef[i] * scale)).astype(qg_ref.dtype)
        qg_ref[1, 0, rs, hc] = (qf * (eb[i] * scale)).astype(qg_ref.dtype)
        kg_ref[0, 0, rs, hc] = (kf * jnp.exp(lf[i] - cf[i])).astype(kg_ref.dtype)
        kg_ref[1, 0, rs, hc] = (kf * jnp.exp(lb[i] - cb[i])).astype(kg_ref.dtype)
        qk_ref[0, rs, hc] = (qk2[i] * decay2[i] * scale).astype(qk_ref.dtype)
    r8 = lax.broadcasted_iota(jnp.int32, (2 * 8, LANE), 0)
    l8 = lax.broadcasted_iota(jnp.int32, (2 * 8, LANE), 1)
    want = jnp.where(r8 < 8, r8, r8 - 8 + B_HEADS)
    for ci in range(chunks):
        src = jnp.concatenate([jnp.broadcast_to(tot_f[ci], (8, LANE)), jnp.broadcast_to(tot_b[ci], (8, LANE))], axis=0)
        tot = jnp.sum(jnp.where(l8 == want, src, 0.0), axis=-1, keepdims=True)
        gl_ref[0, ci] = jnp.broadcast_to(jnp.exp(tot), (2 * 8, LANE))


DN_INTRA_CHUNKS = 8
DN_INTRA_PASSES = 1
DN_SCAN_CHUNKS = 8


def deltanet_intra(qkv, gb):
    b, l, _ = qkv.shape
    nc = l // DN_CHUNK
    chunks = math.gcd(DN_INTRA_CHUNKS, nc)
    tl = chunks * DN_CHUNK
    dir_spec = pl.BlockSpec((2, 1, tl, B_WIDTH), lambda i, j: (0, i, j, 0))
    return pl.pallas_call(
        functools.partial(_dn_intra_kernel, chunks=chunks, passes=DN_INTRA_PASSES), grid=(b, nc // chunks),
        in_specs=[pl.BlockSpec((1, tl, 3 * B_WIDTH), lambda i, j: (i, j, 0)),
                  pl.BlockSpec((1, tl, LANE), lambda i, j: (i, j, 0))],
        out_specs=[dir_spec, dir_spec, dir_spec, dir_spec,
                   pl.BlockSpec((1, tl, B_WIDTH), lambda i, j: (i, j, 0)),
                   pl.BlockSpec((1, chunks, 16, LANE), lambda i, j: (i, j, 0, 0))],
        out_shape=[jax.ShapeDtypeStruct((2, b, l, B_WIDTH), F32),
                   jax.ShapeDtypeStruct((2, b, l, B_WIDTH), BF16),
                   jax.ShapeDtypeStruct((2, b, l, B_WIDTH), BF16),
                   jax.ShapeDtypeStruct((2, b, l, B_WIDTH), BF16),
                   jax.ShapeDtypeStruct((b, l, B_WIDTH), BF16),
                   jax.ShapeDtypeStruct((b, nc, 16, LANE), F32)],
        compiler_params=_cparams("parallel", "parallel"), name="deltanet_intra",
    )(qkv, gb)


def _dn_scan_kernel(uf_ref, wf_ref, qgf_ref, kgf_ref, qkf_ref, glf_ref,
                    ub_ref, wb_ref, qgb_ref, kgb_ref, qkb_ref, glb_ref, s0_ref,
                    of_ref, ob_ref, sfin_ref, s_ref, *, chunks):
    j = pl.program_id(1)
    c = DN_CHUNK

    @pl.when(j == 0)
    def _():
        s_ref[...] = s0_ref[0]

    lane = lax.broadcasted_iota(jnp.int32, (c, LANE), 1)
    zeros_b = jnp.zeros((c, LANE), BF16)
    dirs = ((uf_ref, wf_ref, qgf_ref, kgf_ref, qkf_ref, glf_ref, of_ref),
            (ub_ref, wb_ref, qgb_ref, kgb_ref, qkb_ref, glb_ref, ob_ref))
    units = [(d, h) for d in range(2) for h in range(B_HEADS)]
    cols = [slice(h * LANE, (h + 1) * LANE) for _, h in units]
    state = [s_ref[d * B_HEADS + h] for d, h in units]
    for step in range(chunks):
        chunk_of = (step, chunks - 1 - step)
        rows = [slice(chunk_of[d] * c, (chunk_of[d] + 1) * c) for d, _ in units]
        r = [_dot(jnp.concatenate([dirs[d][1][0, 0, rs, hc], dirs[d][2][0, 0, rs, hc]], axis=0), s.astype(BF16))
             for (d, _), hc, rs, s in zip(units, cols, rows, state)]
        vb = [(dirs[d][0][0, 0, rs, hc] - ri[0:c]).astype(BF16) for (d, _), hc, rs, ri in zip(units, cols, rows, r)]
        new_state = []
        for (d, h), hc, rs, s, ri, vi in zip(units, cols, rows, state, r, vb):
            qk2 = dirs[d][4][0, rs, hc]
            if d == 0:
                intra = _dot(jnp.where(lane < c, qk2, jnp.zeros_like(qk2)), jnp.concatenate([vi, zeros_b], axis=0))
            else:
                intra = _dot(jnp.where(lane < c, jnp.zeros_like(qk2), qk2), jnp.concatenate([zeros_b, vi], axis=0))
            dirs[d][6][0, rs, hc] = ri[c:2 * c] + intra
            gl = dirs[d][5][0, chunk_of[d], d * 8 + h:d * 8 + h + 1, :]
            new_state.append(s * gl + _dot_tn(dirs[d][3][0, 0, rs, hc], vi))
        state = new_state
    for (d, h), s in zip(units, state):
        s_ref[d * B_HEADS + h] = s

    @pl.when(j == pl.num_programs(1) - 1)
    def _():
        sfin_ref[0] = s_ref[...]


def deltanet_scan(u, w, qg, kg, qk, gl, s0):
    _, b, l, _ = u.shape
    chunks = math.gcd(DN_SCAN_CHUNKS, l // DN_CHUNK)
    tl = chunks * DN_CHUNK
    nb = l // tl
    fdir = pl.BlockSpec((1, 1, tl, B_WIDTH), lambda i, j: (0, i, j, 0))
    bdir = pl.BlockSpec((1, 1, tl, B_WIDTH), lambda i, j: (1, i, nb - 1 - j, 0))
    fqk = pl.BlockSpec((1, tl, B_WIDTH), lambda i, j: (i, j, 0))
    bqk = pl.BlockSpec((1, tl, B_WIDTH), lambda i, j: (i, nb - 1 - j, 0))
    fgl = pl.BlockSpec((1, chunks, 16, LANE), lambda i, j: (i, j, 0, 0))
    bgl = pl.BlockSpec((1, chunks, 16, LANE), lambda i, j: (i, nb - 1 - j, 0, 0))
    st = pl.BlockSpec((1, 2 * B_HEADS, B_HEAD_DIM, B_HEAD_DIM), lambda i, j: (i, 0, 0, 0))
    return pl.pallas_call(
        functools.partial(_dn_scan_kernel, chunks=chunks), grid=(b, nb),
        in_specs=[fdir, fdir, fdir, fdir, fqk, fgl, bdir, bdir, bdir, bdir, bqk, bgl, st],
        out_specs=[fqk, bqk, st],
        out_shape=[jax.ShapeDtypeStruct((b, l, B_WIDTH), F32),
                   jax.ShapeDtypeStruct((b, l, B_WIDTH), F32),
                   jax.ShapeDtypeStruct((b, 2 * B_HEADS, B_HEAD_DIM, B_HEAD_DIM), F32)],
        scratch_shapes=[pltpu.VMEM((2 * B_HEADS, B_HEAD_DIM, B_HEAD_DIM), F32)],
        compiler_params=_cparams("parallel", "arbitrary"), name="deltanet_scan",
    )(u, w, qg, kg, qk, gl, u, w, qg, kg, qk, gl, s0)


def _dn_out_kernel(of_ref, ob_ref, gate_ref, g_ref, o_ref):
    for h in range(B_HEADS):
        hc = slice(h * LANE, (h + 1) * LANE)
        o = of_ref[0, :, hc] + ob_ref[0, :, hc]
        y = o * lax.rsqrt(jnp.mean(o * o, axis=-1, keepdims=True) + NORM_EPS) * g_ref[...]
        o_ref[0, :, hc] = (y * _silu(gate_ref[0, :, hc].astype(F32))).astype(o_ref.dtype)


def deltanet_out(o_f, o_b, z, norm_g):
    b, l, _ = o_f.shape
    tl = min(512, l)
    blk = ZC_GATE // B_WIDTH
    return pl.pallas_call(
        _dn_out_kernel, grid=(b, l // tl),
        in_specs=[pl.BlockSpec((1, tl, B_WIDTH), lambda i, j: (i, j, 0)),
                  pl.BlockSpec((1, tl, B_WIDTH), lambda i, j: (i, j, 0)),
                  pl.BlockSpec((1, tl, B_WIDTH), lambda i, j: (i, j, blk)),
                  pl.BlockSpec((1, B_HEAD_DIM), lambda i, j: (0, 0))],
        out_specs=pl.BlockSpec((1, tl, B_WIDTH), lambda i, j: (i, j, 0)),
        out_shape=jax.ShapeDtypeStruct((b, l, B_WIDTH), BF16),
        compiler_params=_cparams("parallel", "parallel"), name="deltanet_out",
    )(o_f, o_b, z, norm_g.reshape(1, B_HEAD_DIM))


def _route_kernel(lg_ref, info_ref, cnt_ref, carry_ref):
    i = pl.program_id(0)

    @pl.when(i == 0)
    def _():
        carry_ref[...] = jnp.zeros_like(carry_ref)

    lg = lg_ref[...]
    tl = lg.shape[0]
    lane = lax.broadcasted_iota(jnp.int32, lg.shape, 1)
    valid = lane < N_EXPERTS
    lg = jnp.where(valid, lg, -jnp.inf)
    e = jnp.exp(lg - jnp.max(lg, axis=-1, keepdims=True))
    p = e / jnp.sum(e, axis=-1, keepdims=True)
    p = jnp.where(valid, p, -1.0)
    p1 = jnp.max(p, axis=-1, keepdims=True)
    i1 = jnp.min(jnp.where(p == p1, lane, LANE), axis=-1, keepdims=True)
    pm = jnp.where(lane == i1, -1.0, p)
    p2 = jnp.max(pm, axis=-1, keepdims=True)
    i2 = jnp.min(jnp.where(pm == p2, lane, LANE), axis=-1, keepdims=True)
    tot = p1 + p2
    w1, w2 = p1 / tot, p2 / tot
    hit1, hit2 = lane == i1, lane == i2
    onehot = (hit1 | hit2).astype(F32)
    r = lax.broadcasted_iota(jnp.int32, (tl, tl), 0)
    c = lax.broadcasted_iota(jnp.int32, (tl, tl), 1)
    before = _dot((r > c).astype(BF16), onehot.astype(BF16)) + carry_ref[...]
    r1 = jnp.sum(jnp.where(hit1, before, 0.0), axis=-1, keepdims=True)
    r2 = jnp.sum(jnp.where(hit2, before, 0.0), axis=-1, keepdims=True)
    carry_ref[...] += jnp.sum(onehot, axis=0, keepdims=True)
    cnt_ref[...] = jnp.broadcast_to(carry_ref[...], cnt_ref.shape)
    info = jnp.where(lane == 0, i1.astype(F32), 0.0)
    info = jnp.where(lane == 1, i2.astype(F32), info)
    info = jnp.where(lane == 2, r1, info)
    info = jnp.where(lane == 3, r2, info)
    info = jnp.where(lane == 4, w1, info)
    info = jnp.where(lane == 5, w2, info)
    info_ref[...] = info


def moe_route(logits):
    t = logits.shape[0]
    tl = min(512, t)
    return pl.pallas_call(
        _route_kernel, grid=(t // tl,),
        in_specs=[pl.BlockSpec((tl, LANE), lambda i: (i, 0))],
        out_specs=[pl.BlockSpec((tl, LANE), lambda i: (i, 0)), pl.BlockSpec((8, LANE), lambda i: (0, 0))],
        out_shape=[jax.ShapeDtypeStruct((t, LANE), F32), jax.ShapeDtypeStruct((8, LANE), F32)],
        scratch_shapes=[pltpu.VMEM((1, LANE), F32)],
        compiler_params=_cparams("arbitrary"), name="moe_route",
    )(logits)


def _dispatch_kernel(pos_ref, h_ref, xs_in_ref, xs_ref, sem):
    del xs_in_ref
    tb = pos_ref.shape[2] // TOP_K

    def copy(src_row, dst_row):
        return pltpu.make_async_copy(h_ref.at[pl.ds(pl.multiple_of(src_row * ROW_TILE, ROW_TILE), ROW_TILE)],
                                     xs_ref.at[pl.ds(pl.multiple_of(dst_row * ROW_TILE, ROW_TILE), ROW_TILE)], sem)

    def issue(j, carry):
        for k in range(TOP_K):
            copy(j, pos_ref[0, 0, TOP_K * j + k]).start()
        return carry

    lax.fori_loop(0, tb, issue, 0, unroll=8)
    for _ in range(TOP_K):
        pltpu.make_async_copy(h_ref, xs_ref.at[pl.ds(0, tb * ROW_TILE)], sem).wait()


def moe_dispatch(h_rows, pos, n_rows):
    t = pos.shape[0]
    tb = min(256, t)
    d_rows = h_rows.shape[0] // t
    assert d_rows == ROW_TILE
    zeros = jnp.zeros((n_rows * ROW_TILE, LANE), h_rows.dtype)
    return pl.pallas_call(
        _dispatch_kernel, grid=(t // tb,),
        in_specs=[pl.BlockSpec((1, 1, TOP_K * tb), lambda i: (i, 0, 0), memory_space=pltpu.SMEM),
                  pl.BlockSpec((tb * ROW_TILE, LANE), lambda i: (i, 0)),
                  pl.BlockSpec(memory_space=pl.ANY)],
        out_specs=pl.BlockSpec(memory_space=pl.ANY),
        out_shape=jax.ShapeDtypeStruct(zeros.shape, zeros.dtype),
        scratch_shapes=[pltpu.SemaphoreType.DMA],
        input_output_aliases={2: 0},
        compiler_params=_cparams("arbitrary"), name="moe_dispatch",
    )(pos.reshape(t // tb, 1, TOP_K * tb), h_rows, zeros)


def _moe_up_kernel(te_ref, tv_ref, xs_ref, wg_ref, wu_ref, o_ref, a_ref):
    del te_ref
    i = pl.program_id(0)
    tm = a_ref.shape[0]

    @pl.when(pl.program_id(1) == 0)
    def _():
        for c in range(ROW_TILE):
            lo, hi = _load_token_rows(xs_ref, tm, c)
            a_ref[:, c * LANE:(c + 1) * LANE] = lo.astype(a_ref.dtype)
            a_ref[:, (c + ROW_TILE) * LANE:(c + ROW_TILE + 1) * LANE] = hi.astype(a_ref.dtype)

    @pl.when(tv_ref[i] != 0)
    def _():
        a = a_ref[...]
        g = _dot(a, wg_ref[0].astype(BF16))
        u = _dot(a, wu_ref[0].astype(BF16))
        o_ref[...] = (_silu(g) * u).astype(o_ref.dtype)

    @pl.when(tv_ref[i] == 0)
    def _():
        o_ref[...] = jnp.zeros_like(o_ref)


def moe_up(xs_rows, wg, wu, tile_expert, tile_valid, tm, tn):
    r = xs_rows.shape[0] // ROW_TILE
    d, n = wg.shape[1], wg.shape[2]
    tn = _tile(n, tn)
    nj = n // tn

    def w_index(i, j, te, tv):
        return te[i], 0, jnp.where(tv[i] != 0, j, nj - 1)

    grid_spec = pltpu.PrefetchScalarGridSpec(
        num_scalar_prefetch=2, grid=(r // tm, nj),
        in_specs=[pl.BlockSpec((tm * ROW_TILE, LANE), lambda i, j, te, tv: (i, 0)),
                  pl.BlockSpec((1, d, tn), w_index),
                  pl.BlockSpec((1, d, tn), w_index)],
        out_specs=pl.BlockSpec((tm, tn), lambda i, j, te, tv: (i, j)),
        scratch_shapes=[pltpu.VMEM((tm, d), BF16)])
    return pl.pallas_call(
        _moe_up_kernel, grid_spec=grid_spec,
        out_shape=jax.ShapeDtypeStruct((r, n), BF16),
        compiler_params=_cparams("parallel", "arbitrary"), name="moe_up",
    )(tile_expert, tile_valid, xs_rows, wg, wu)


def _moe_down_kernel(te_ref, tv_ref, a_ref, w_ref, o_ref, acc_ref):
    del te_ref
    i = pl.program_id(0)
    k = pl.program_id(1)

    valid = tv_ref[i] != 0

    @pl.when(jnp.logical_and(valid, k == 0))
    def _():
        acc_ref[...] = _dot(a_ref[...], w_ref[0].astype(BF16))

    @pl.when(jnp.logical_and(valid, k > 0))
    def _():
        acc_ref[...] += _dot(a_ref[...], w_ref[0].astype(BF16))

    @pl.when(jnp.logical_and(jnp.logical_not(valid), k == 0))
    def _():
        acc_ref[...] = jnp.zeros_like(acc_ref)

    @pl.when(k == pl.num_programs(1) - 1)
    def _():
        _store_token_rows(o_ref, acc_ref[...])


def moe_down(hid, wd, tile_expert, tile_valid, tm, tk):
    r, kdim = hid.shape
    n = wd.shape[2]
    assert n == D_TOKEN
    tk = _tile(kdim, tk)
    nk = kdim // tk
    grid_spec = pltpu.PrefetchScalarGridSpec(
        num_scalar_prefetch=2, grid=(r // tm, nk),
        in_specs=[pl.BlockSpec((tm, tk), lambda i, k, te, tv: (i, jnp.where(tv[i] != 0, k, nk - 1))),
                  pl.BlockSpec((1, tk, n), lambda i, k, te, tv: (te[i], jnp.where(tv[i] != 0, k, nk - 1), 0))],
        out_specs=pl.BlockSpec((tm * ROW_TILE, LANE), lambda i, k, te, tv: (i, 0)),
        scratch_shapes=[pltpu.VMEM((tm, n), F32)])
    return pl.pallas_call(
        _moe_down_kernel, grid_spec=grid_spec,
        out_shape=jax.ShapeDtypeStruct((r * ROW_TILE, LANE), U32),
        compiler_params=_cparams("parallel", "arbitrary"), name="moe_down",
    )(tile_expert, tile_valid, hid, wd)


def _combine_kernel(pos_ref, pos_next_ref, ys_ref, x_ref, gate_ref, wt_ref, ng_ref, o_ref, buf_ref, sem, *, final_norm):
    i = pl.program_id(0)
    tb = x_ref.shape[0]
    slot = lax.rem(i, 2)

    def gather(p_ref, s):
        def issue(j, carry):
            for k in range(TOP_K):
                src_row = p_ref[0, 0, TOP_K * j + k]
                pltpu.make_async_copy(
                    ys_ref.at[pl.ds(pl.multiple_of(src_row * ROW_TILE, ROW_TILE), ROW_TILE)],
                    buf_ref.at[s, k, pl.ds(pl.multiple_of(j * ROW_TILE, ROW_TILE), ROW_TILE)], sem.at[s]).start()
            return carry

        lax.fori_loop(0, tb, issue, 0, unroll=8)

    @pl.when(i == 0)
    def _():
        gather(pos_ref, 0)

    @pl.when(i + 1 < pl.num_programs(0))
    def _():
        gather(pos_next_ref, 1 - slot)

    for k in range(TOP_K):
        pltpu.make_async_copy(ys_ref.at[pl.ds(0, tb * ROW_TILE)], buf_ref.at[slot, k], sem.at[slot]).wait()
    w0 = wt_ref[:, 0:1]
    w1 = wt_ref[:, 1:2]
    sq = jnp.zeros((tb, 1), F32)
    for c in range(ROW_TILE):
        first = _load_token_rows(buf_ref.at[slot, 0], tb, c)
        second = _load_token_rows(buf_ref.at[slot, 1], tb, c)
        for half in range(2):
            cols = slice((c + half * ROW_TILE) * LANE, (c + half * ROW_TILE + 1) * LANE)
            f = first[half] * w0 + second[half] * w1
            y = x_ref[:, cols] + gate_ref[0, :, cols] * f
            o_ref[:, cols] = y
            sq = sq + jnp.sum(y * y, axis=-1, keepdims=True)
    if final_norm:
        o_ref[...] = o_ref[...] * lax.rsqrt(sq * (1.0 / D_TOKEN) + NORM_EPS) * ng_ref[...]


def moe_combine(ys_rows, pos, wts, x, gate, rows_per_batch, final_norm_g=None):
    t, d = x.shape
    tb = min(128, t)
    bsz = gate.shape[0]
    final_norm = final_norm_g is not None
    ng = (final_norm_g if final_norm else jnp.ones((d,), F32)).reshape(1, d)
    nb = t // tb
    pos_blocks = pos.reshape(nb, 1, TOP_K * tb)
    return pl.pallas_call(
        functools.partial(_combine_kernel, final_norm=final_norm), grid=(nb,),
        in_specs=[pl.BlockSpec((1, 1, TOP_K * tb), lambda i: (i, 0, 0), memory_space=pltpu.SMEM),
                  pl.BlockSpec((1, 1, TOP_K * tb), lambda i: (jnp.minimum(i + 1, nb - 1), 0, 0),
                               memory_space=pltpu.SMEM),
                  pl.BlockSpec(memory_space=pl.ANY),
                  pl.BlockSpec((tb, d), lambda i: (i, 0)),
                  pl.BlockSpec((1, 1, d), lambda i: ((i * tb) // rows_per_batch, 0, 0)),
                  pl.BlockSpec((tb, TOP_K), lambda i: (i, 0)),
                  pl.BlockSpec((1, d), lambda i: (0, 0))],
        out_specs=pl.BlockSpec((tb, d), lambda i: (i, 0)),
        out_shape=jax.ShapeDtypeStruct((t, d), F32),
        scratch_shapes=[pltpu.VMEM((2, TOP_K, tb * ROW_TILE, LANE), U32), pltpu.SemaphoreType.DMA((2,))],
        compiler_params=_cparams("arbitrary"), name="moe_combine",
    )(pos_blocks, pos_blocks, ys_rows, x, gate.reshape(bsz, 1, d), wts, ng)


MOE_TM = 1024


def moe_ffn(h_rows, logits, wg, wu, wd, x, gate, rows_per_batch, final_norm_g=None):
    t, d = x.shape
    tm = min(MOE_TM, t)
    info, counts = moe_route(logits)
    sizes = counts[0, :N_EXPERTS].astype(jnp.int32)
    padded = ((sizes + tm - 1) // tm) * tm
    ends = jnp.cumsum(padded)
    starts = ends - padded
    n_tiles = (t * TOP_K) // tm + N_EXPERTS
    n_rows = n_tiles * tm
    experts = info[:, 0:TOP_K].astype(jnp.int32)
    pos = starts[experts] + info[:, 2:2 + TOP_K].astype(jnp.int32)
    wts = info[:, 4:4 + TOP_K]
    tile_start = jnp.arange(n_tiles, dtype=jnp.int32) * tm
    tile_expert = jnp.minimum(jnp.sum((tile_start[:, None] >= ends[None, :]).astype(jnp.int32), axis=1), N_EXPERTS - 1)
    tile_valid = (tile_start < ends[-1]).astype(jnp.int32)
    last_expert = jnp.max(jnp.where(sizes > 0, jnp.arange(N_EXPERTS, dtype=jnp.int32), 0))
    tile_expert = jnp.where(tile_valid != 0, tile_expert, last_expert)
    xs_rows = moe_dispatch(h_rows, pos, n_rows)
    hid = moe_up(xs_rows, wg, wu, tile_expert, tile_valid, tm, 512)
    ys_rows = moe_down(hid, wd.astype(BF16), tile_expert, tile_valid, tm, 1792)
    return moe_combine(ys_rows, pos, wts, x, gate, rows_per_batch, final_norm_g)


def _rot_cols(w):
    f = ROPE_AXIS_FREQS
    return jnp.concatenate([-w[:, f:2 * f], w[:, 0:f], -w[:, 3 * f:4 * f], w[:, 2 * f:3 * f]], axis=1)


def _arrange_w_in(w):
    d = w.shape[0]
    a = w[:, 0:2 * A_WIDTH]
    off = 2 * A_WIDTH
    qkv_gate = w[:, off:off + 4 * B_WIDTH]
    logit = w[:, off + 4 * B_WIDTH:off + 4 * B_WIDTH + 4 * B_HEADS]
    off = off + 4 * B_WIDTH + 4 * B_HEADS
    lat = w[:, off:off + Q_LORA + KV_LORA]
    k_pe = w[:, off + Q_LORA + KV_LORA:off + Q_LORA + KV_LORA + C_ROPE]
    z64 = jnp.zeros((d, LANE - C_ROPE), w.dtype)
    main = jnp.concatenate([qkv_gate, a, lat, k_pe, z64, _rot_cols(k_pe), z64], axis=1).astype(BF16)
    logit = jnp.pad(logit, ((0, 0), (0, LANE - 4 * B_HEADS))).astype(BF16)
    return main, logit


def _arrange_w_uq(w):
    k = w.shape[0]
    w = w.reshape(k, C_HEADS, C_NOPE + C_ROPE)
    z64 = jnp.zeros((k, C_HEADS, LANE - C_ROPE), w.dtype)
    pe = w[:, :, C_NOPE:]
    pe_rot = jnp.stack([_rot_cols(pe[:, h]) for h in range(C_HEADS)], axis=1)
    return jnp.concatenate([w[:, :, :C_NOPE], pe, z64, pe_rot, z64], axis=2).reshape(k, C_HEADS * C_HEAD_COLS).astype(BF16)


def _rope_tables(n):
    rows = n // GRID_W
    row = np.repeat(np.arange(rows, dtype=np.float32), GRID_W)
    col = np.tile(np.arange(GRID_W, dtype=np.float32), rows)
    inv = np.power(np.float32(ROPE_BASE), -np.arange(ROPE_AXIS_FREQS, dtype=np.float32) / np.float32(ROPE_AXIS_FREQS))
    ar = row[:, None] * inv.astype(np.float32)
    ac = col[:, None] * inv.astype(np.float32)
    ang = np.concatenate([ar, ar, ac, ac], axis=-1).astype(np.float32)
    pad = ((0, 0), (0, LANE - C_ROPE))
    return (jnp.asarray(np.pad(np.cos(ang).astype(np.float32), pad)),
            jnp.asarray(np.pad(np.sin(ang).astype(np.float32), pad)))


def _mixer_branches(z, zg, p, cos2, sin2, **kv_placement):
    out_a = chunk_sgu(z, p["sgu_norm_g"], p["sgu_w"], p["sgu_b"])
    qkv, gb = deltanet_prep(z, zg, p["dn_conv_w"], p["dn_a_log"], p["dn_dt_bias"])
    intra = deltanet_intra(qkv, gb)
    q, k, v = mla_prep(z, p["mla_q_norm_g"], p["mla_kv_norm_g"], p["wq_arr"], p["wkv_arr"], cos2, sin2, **kv_placement)
    return out_a, intra, (q, k, v)


def kernel(x, c, ctx, c_ctx, ada_w, ada_b, norm1_g, norm2_g, w_in, sgu_norm_g, sgu_w, sgu_b, dn_conv_w, dn_a_log, dn_dt_bias, dn_norm_g, mla_q_norm_g, mla_w_uq, mla_kv_norm_g, mla_w_ukv, w_out, ffn_w_gate, ffn_w_up, ffn_w_down, moe_router, moe_w_gate, moe_w_up, moe_w_down, final_norm_g):
    b, n, d = x.shape
    lc = ctx.shape[1]
    depth = ada_w.shape[0]
    assert d == D_TOKEN
    cos_lat, sin_lat = _rope_tables(n)
    cos_ctx = jnp.pad(jnp.ones((lc, C_ROPE), F32), ((0, 0), (0, LANE - C_ROPE)))
    sin_ctx = jnp.zeros((lc, LANE), F32)

    cc = jnp.zeros((8, d), F32).at[0:b].set(c).at[b].set(c_ctx)
    mod_all = ada_modulation(cc, ada_w, ada_b)

    xc = ctx
    x_rows = None
    for i in range(depth):
        last = i == depth - 1
        mod = mod_all[i, 0:b].reshape(b, 6, d)
        mod_c = jnp.broadcast_to(mod_all[i, b].reshape(1, 6, d), (b, 6, d))
        w_main, w_logit = _arrange_w_in(w_in[i])
        p = dict(sgu_norm_g=sgu_norm_g[i], sgu_w=sgu_w[i], sgu_b=sgu_b[i], dn_conv_w=dn_conv_w[i],
                 dn_a_log=dn_a_log[i], dn_dt_bias=dn_dt_bias[i], mla_q_norm_g=mla_q_norm_g[i],
                 mla_kv_norm_g=mla_kv_norm_g[i], wq_arr=_arrange_w_uq(mla_w_uq[i]),
                 wkv_arr=mla_w_ukv[i].astype(BF16))
        w_out_b = w_out[i].astype(BF16)

        h = norm_mod(x, norm1_g[i], mod[:, 0], mod[:, 1]).reshape(b * n, d)
        hc = norm_mod(xc, norm1_g[i], mod_c[:, 0], mod_c[:, 1]).reshape(b * lc, d)
        z = matmul(h, w_main, BF16, 1024, 1024).reshape(b, n, Z_COLS)
        zg = matmul(h, w_logit, F32, 1024, LANE).reshape(b, n, LANE)
        zc = matmul(hc, w_main, BF16, 1024, 1024).reshape(b, lc, Z_COLS)
        zgc = matmul(hc, w_logit, F32, 1024, LANE).reshape(b, lc, LANE)

        kv_zero = (jnp.zeros((b, C_HEADS, n + lc, 2 * LANE), BF16), jnp.zeros((b, C_HEADS, n + lc, LANE), BF16))
        oa_c, intra_c, (q_c, k_all, v_all) = _mixer_branches(zc, zgc, p, cos_ctx, sin_ctx, kv_rows=n + lc, kv_row0=n,
                                                             kv_into=kv_zero)
        out_a, intra, (q_l, k_all, v_all) = _mixer_branches(z, zg, p, cos_lat, sin_lat, kv_rows=n + lc, kv_row0=0,
                                                            kv_into=(k_all, v_all))
        s_zero = jnp.zeros((b, 2 * B_HEADS, B_HEAD_DIM, B_HEAD_DIM), F32)
        ocf, ocb, s_ctx = deltanet_scan(*intra_c, s_zero)
        o_f, o_b, _ = deltanet_scan(*intra, s_ctx)
        out_b = deltanet_out(o_f, o_b, z, dn_norm_g[i])
        out_c = flash_attention(q_l, k_all, v_all)
        x2 = mix_out_residual(out_a.reshape(b * n, -1), out_b.reshape(b * n, -1), out_c.reshape(b * n, -1),
                              w_out_b, x.reshape(b * n, d), mod[:, 2], n, 1024, 1024)
        if not last:
            ob_c = deltanet_out(ocf, ocb, zc, dn_norm_g[i])
            oc_c = flash_attention(q_c, k_all[:, :, n:], v_all[:, :, n:])
            xc2 = mix_out_residual(oa_c.reshape(b * lc, -1), ob_c.reshape(b * lc, -1), oc_c.reshape(b * lc, -1),
                                   w_out_b, xc.reshape(b * lc, d), mod_c[:, 2], lc, 1024, 1024)

        if i % 2 == 0:
            wg, wu, wd = (ffn_w_gate[i // 2].astype(BF16), ffn_w_up[i // 2].astype(BF16), ffn_w_down[i // 2].astype(BF16))
            h2 = norm_mod(x2.reshape(b, n, d), norm2_g[i], mod[:, 3], mod[:, 4]).reshape(b * n, d)
            hid = swiglu_up(h2, wg, wu, 1024, 512)
            x = down_residual(hid, wd, x2, mod[:, 5], n, 1024, 512, wd.shape[0]).reshape(b, n, d)
            if not last:
                hc2 = norm_mod(xc2.reshape(b, lc, d), norm2_g[i], mod_c[:, 3], mod_c[:, 4]).reshape(b * lc, d)
                hid_c = swiglu_up(hc2, wg, wu, 1024, 512)
                xc = down_residual(hid_c, wd, xc2, mod_c[:, 5], lc, 1024, 512, wd.shape[0]).reshape(b, lc, d)
        else:
            e = i // 2
            router = jnp.pad(moe_router[e], ((0, 0), (0, LANE - N_EXPERTS)))
            if not last:
                raise NotImplementedError("an expert layer followed by another layer is not part of this model")
            h_rows, logits = norm_mod(x2.reshape(b, n, d), norm2_g[i], mod[:, 3], mod[:, 4], router=router)
            return moe_ffn(h_rows, logits.reshape(b * n, LANE), moe_w_gate[e], moe_w_up[e], moe_w_down[e],
                           x2, mod[:, 5], n, final_norm_g).reshape(b, n, d)
    return rmsnorm_rows(x.reshape(b * n, d), final_norm_g).reshape(b, n, d)
```

```python
import functools
import math

import jax
import jax.numpy as jnp
import numpy as np
from jax import lax
from jax.experimental import pallas as pl
from jax.experimental.pallas import tpu as pltpu

F32 = jnp.float32
BF16 = jnp.bfloat16
HIGHEST = lax.Precision.HIGHEST

NORM_EPS = 1e-6
GRID_W = 64
A_GROUPS = 4
A_GROUP_DIM = 128
A_WIDTH = A_GROUPS * A_GROUP_DIM
A_CHUNK = 128
B_HEADS = 6
B_HEAD_DIM = 128
B_WIDTH = B_HEADS * B_HEAD_DIM
DN_CHUNK = 64
DN_CONV = 5
C_HEADS = 6
C_NOPE = 128
C_ROPE = 64
C_V = 128
C_WIDTH = C_HEADS * C_V
Q_LORA = 512
KV_LORA = 256
ROPE_BASE = 10000.0
ROPE_AXIS_FREQS = C_ROPE // 4
N_EXPERTS = 8
TOP_K = 2

LANE = 128
ROW_TILE = 8
D_TOKEN = 2 * ROW_TILE * LANE
U32 = jnp.uint32
VMEM_LIMIT = 56 * 1024 * 1024

ZC_QKV = 0
ZC_GATE = 3 * B_WIDTH
ZC_A = 4 * B_WIDTH
ZC_C = 4 * B_WIDTH + 2 * A_WIDTH
Z_COLS = ZC_C + 1024
C_HEAD_COLS = 3 * LANE


def _cparams(*sem):
    return pltpu.CompilerParams(dimension_semantics=sem, vmem_limit_bytes=VMEM_LIMIT)


def _tile(n, pref):
    if n <= pref:
        return n
    t = (pref // LANE) * LANE
    while n % t:
        t -= LANE
    return t


def _sigmoid(x):
    return 0.5 + 0.5 * jnp.tanh(0.5 * x)


def _silu(x):
    return x * _sigmoid(x)


def _dot(a, b):
    return jnp.dot(a, b, preferred_element_type=F32)


def _dot_nt(a, b):
    return lax.dot_general(a, b, (((1,), (1,)), ((), ())), preferred_element_type=F32)


def _dot_tn(a, b):
    return lax.dot_general(a, b, (((0,), (0,)), ((), ())), preferred_element_type=F32)


def _split_bf16(a):
    hi = a.astype(BF16)
    lo = (a - hi.astype(F32)).astype(BF16)
    return hi, lo


def _dot_x3(a, b):
    ah, al = _split_bf16(a)
    bh, bl = _split_bf16(b)
    return _dot(ah, bh) + (_dot(ah, bl) + _dot(al, bh))


def _ada_kernel(c_ref, w_ref, b_ref, o_ref):
    c = c_ref[...]
    o_ref[0] = jnp.dot(_silu(c), w_ref[0], preferred_element_type=F32, precision=HIGHEST) + b_ref[0]


def ada_modulation(cc, ada_w, ada_b):
    depth, d, n = ada_w.shape
    tn = 1024
    return pl.pallas_call(
        _ada_kernel,
        grid=(depth, n // tn),
        in_specs=[
            pl.BlockSpec((8, d), lambda i, j: (0, 0)),
            pl.BlockSpec((1, d, tn), lambda i, j: (i, 0, j)),
            pl.BlockSpec((1, 1, tn), lambda i, j: (i, 0, j)),
        ],
        out_specs=pl.BlockSpec((1, 8, tn), lambda i, j: (i, 0, j)),
        out_shape=jax.ShapeDtypeStruct((depth, 8, n), F32),
        compiler_params=_cparams("parallel", "parallel"),
        name="ada_modulation",
    )(cc, ada_w, ada_b.reshape(depth, 1, n))


def _norm_mod_kernel(x_ref, g_ref, sh_ref, sc_ref, o_ref):
    x = x_ref[0]
    y = x * lax.rsqrt(jnp.mean(x * x, axis=-1, keepdims=True) + NORM_EPS) * g_ref[...]
    o_ref[0] = (y * (1.0 + sc_ref[0]) + sh_ref[0]).astype(o_ref.dtype)


def _norm_mod_router_kernel(x_ref, g_ref, sh_ref, sc_ref, r_ref, o_ref, lg_ref):
    x = x_ref[0]
    y = x * lax.rsqrt(jnp.mean(x * x, axis=-1, keepdims=True) + NORM_EPS) * g_ref[...]
    h = y * (1.0 + sc_ref[0]) + sh_ref[0]
    _store_token_rows(o_ref, h)
    lg_ref[0] = _dot_x3(h, r_ref[...])


def _store_token_rows(rows_ref, x):
    n = x.shape[0]
    for c in range(ROW_TILE):
        lo = lax.bitcast_convert_type(x[:, c * LANE:(c + 1) * LANE].astype(BF16).astype(F32), U32)
        hi = lax.bitcast_convert_type(x[:, (c + ROW_TILE) * LANE:(c + ROW_TILE + 1) * LANE].astype(BF16).astype(F32), U32)
        rows_ref[pl.ds(c, n, stride=ROW_TILE), :] = lax.shift_right_logical(lo, jnp.uint32(16)) | (hi & jnp.uint32(0xFFFF0000))


def _load_token_rows(rows_ref, n, c):
    w = rows_ref[pl.ds(c, n, stride=ROW_TILE), :]
    lo = lax.bitcast_convert_type(lax.shift_left(w, jnp.uint32(16)), F32)
    hi = lax.bitcast_convert_type(w & jnp.uint32(0xFFFF0000), F32)
    return lo, hi


def norm_mod(x, g, shift, scale, out_dtype=BF16, router=None):
    b, l, d = x.shape
    tl = min(512, l)
    in_specs = [
        pl.BlockSpec((1, tl, d), lambda i, j: (i, j, 0)),
        pl.BlockSpec((1, d), lambda i, j: (0, 0)),
        pl.BlockSpec((1, 1, d), lambda i, j: (i, 0, 0)),
        pl.BlockSpec((1, 1, d), lambda i, j: (i, 0, 0)),
    ]
    args = [x, g.reshape(1, d), shift.reshape(b, 1, d), scale.reshape(b, 1, d)]
    h_spec = pl.BlockSpec((1, tl, d), lambda i, j: (i, j, 0))
    h_shape = jax.ShapeDtypeStruct((b, l, d), out_dtype)
    if router is None:
        return pl.pallas_call(
            _norm_mod_kernel, grid=(b, l // tl), in_specs=in_specs, out_specs=h_spec, out_shape=h_shape,
            compiler_params=_cparams("parallel", "parallel"), name="norm_mod",
        )(*args)
    assert d == D_TOKEN
    in_specs.append(pl.BlockSpec((d, LANE), lambda i, j: (0, 0)))
    nl = l // tl
    return pl.pallas_call(
        _norm_mod_router_kernel, grid=(b, nl), in_specs=in_specs,
        out_specs=[pl.BlockSpec((tl * ROW_TILE, LANE), lambda i, j: (i * nl + j, 0)),
                   pl.BlockSpec((1, tl, LANE), lambda i, j: (i, j, 0))],
        out_shape=[jax.ShapeDtypeStruct((b * l * ROW_TILE, LANE), U32), jax.ShapeDtypeStruct((b, l, LANE), F32)],
        compiler_params=_cparams("parallel", "parallel"), name="norm_mod_router",
    )(*args, router)


def _rmsnorm_kernel(x_ref, g_ref, o_ref):
    x = x_ref[...]
    o_ref[...] = x * lax.rsqrt(jnp.mean(x * x, axis=-1, keepdims=True) + NORM_EPS) * g_ref[...]


def rmsnorm_rows(x, g):
    m, d = x.shape
    tm = min(512, m)
    return pl.pallas_call(
        _rmsnorm_kernel, grid=(m // tm,),
        in_specs=[pl.BlockSpec((tm, d), lambda i: (i, 0)), pl.BlockSpec((1, d), lambda i: (0, 0))],
        out_specs=pl.BlockSpec((tm, d), lambda i: (i, 0)),
        out_shape=jax.ShapeDtypeStruct((m, d), F32),
        compiler_params=_cparams("parallel"), name="final_rmsnorm",
    )(x, g.reshape(1, d))


def _mm_kernel(a_ref, w_ref, o_ref):
    o_ref[...] = _dot(a_ref[...], w_ref[...]).astype(o_ref.dtype)


def matmul(a, w, out_dtype, tm, tn):
    m, k = a.shape
    n = w.shape[1]
    tm, tn = _tile(m, tm), _tile(n, tn)
    return pl.pallas_call(
        _mm_kernel, grid=(m // tm, n // tn),
        in_specs=[pl.BlockSpec((tm, k), lambda i, j: (i, 0)), pl.BlockSpec((k, tn), lambda i, j: (0, j))],
        out_specs=pl.BlockSpec((tm, tn), lambda i, j: (i, j)),
        out_shape=jax.ShapeDtypeStruct((m, n), out_dtype),
        compiler_params=_cparams("parallel", "parallel"), name="matmul",
    )(a, w)


def _swiglu_up_kernel(a_ref, wg_ref, wu_ref, o_ref):
    a = a_ref[...]
    g = _dot(a, wg_ref[...])
    u = _dot(a, wu_ref[...])
    o_ref[...] = (_silu(g) * u).astype(o_ref.dtype)


def swiglu_up(a, wg, wu, tm, tn):
    m, k = a.shape
    n = wg.shape[1]
    tm, tn = _tile(m, tm), _tile(n, tn)
    return pl.pallas_call(
        _swiglu_up_kernel, grid=(m // tm, n // tn),
        in_specs=[pl.BlockSpec((tm, k), lambda i, j: (i, 0)),
                  pl.BlockSpec((k, tn), lambda i, j: (0, j)),
                  pl.BlockSpec((k, tn), lambda i, j: (0, j))],
        out_specs=pl.BlockSpec((tm, tn), lambda i, j: (i, j)),
        out_shape=jax.ShapeDtypeStruct((m, n), BF16),
        compiler_params=_cparams("parallel", "parallel"), name="swiglu_up",
    )(a, wg, wu)


def _down_res_kernel(a_ref, w_ref, x_ref, gate_ref, o_ref, acc_ref):
    k = pl.program_id(2)

    @pl.when(k == 0)
    def _():
        acc_ref[...] = _dot(a_ref[...], w_ref[...])

    @pl.when(k > 0)
    def _():
        acc_ref[...] += _dot(a_ref[...], w_ref[...])

    @pl.when(k == pl.num_programs(2) - 1)
    def _():
        o_ref[...] = x_ref[...] + gate_ref[0] * acc_ref[...]


def down_residual(a, w, x, gate, rows_per_batch, tm, tn, tk):
    m, kdim = a.shape
    n = w.shape[1]
    tm, tn, tk = _tile(rows_per_batch, tm), _tile(n, tn), _tile(kdim, tk)
    bsz = gate.shape[0]
    return pl.pallas_call(
        _down_res_kernel, grid=(m // tm, n // tn, kdim // tk),
        in_specs=[pl.BlockSpec((tm, tk), lambda i, j, k: (i, k)),
                  pl.BlockSpec((tk, tn), lambda i, j, k: (k, j)),
                  pl.BlockSpec((tm, tn), lambda i, j, k: (i, j)),
                  pl.BlockSpec((1, 1, tn), lambda i, j, k: ((i * tm) // rows_per_batch, 0, j))],
        out_specs=pl.BlockSpec((tm, tn), lambda i, j, k: (i, j)),
        out_shape=jax.ShapeDtypeStruct((m, n), F32),
        scratch_shapes=[pltpu.VMEM((tm, tn), F32)],
        compiler_params=_cparams("parallel", "parallel", "arbitrary"), name="down_residual",
    )(a, w, x, gate.reshape(bsz, 1, n))


def _mix_out_kernel(a_ref, b_ref, c_ref, w_ref, x_ref, gate_ref, o_ref):
    ka, kb = a_ref.shape[1], b_ref.shape[1]
    acc = _dot(a_ref[...], w_ref[0:ka, :])
    acc += _dot(b_ref[...], w_ref[ka:ka + kb, :])
    acc += _dot(c_ref[...], w_ref[ka + kb:, :])
    o_ref[...] = x_ref[...] + gate_ref[0] * acc


def mix_out_residual(oa, ob, oc, w, x, gate, rows_per_batch, tm, tn):
    m = oa.shape[0]
    kdim, n = w.shape
    tm, tn = _tile(rows_per_batch, tm), _tile(n, tn)
    bsz = gate.shape[0]
    return pl.pallas_call(
        _mix_out_kernel, grid=(m // tm, n // tn),
        in_specs=[pl.BlockSpec((tm, oa.shape[1]), lambda i, j: (i, 0)),
                  pl.BlockSpec((tm, ob.shape[1]), lambda i, j: (i, 0)),
                  pl.BlockSpec((tm, oc.shape[1]), lambda i, j: (i, 0)),
                  pl.BlockSpec((kdim, tn), lambda i, j: (0, j)),
                  pl.BlockSpec((tm, tn), lambda i, j: (i, j)),
                  pl.BlockSpec((1, 1, tn), lambda i, j: ((i * tm) // rows_per_batch, 0, j))],
        out_specs=pl.BlockSpec((tm, tn), lambda i, j: (i, j)),
        out_shape=jax.ShapeDtypeStruct((m, n), F32),
        compiler_params=_cparams("parallel", "parallel"), name="mix_out_residual",
    )(oa, ob, oc, w, x, gate.reshape(bsz, 1, n))


def _gelu_tanh(x):
    return 0.5 * x * (1.0 + jnp.tanh(math.sqrt(2.0 / math.pi) * (x + 0.044715 * (x * x * x))))


def _sgu_kernel(z_ref, g_ref, w_ref, b_ref, o_ref):
    tl = z_ref.shape[1]
    for c in range(tl // A_CHUNK):
        rows = slice(c * A_CHUNK, (c + 1) * A_CHUNK)
        for g in range(A_GROUPS):
            cols = slice(g * A_GROUP_DIM, (g + 1) * A_GROUP_DIM)
            u = _gelu_tanh(z_ref[0, rows, cols].astype(F32))
            v = _gelu_tanh(z_ref[0, rows, A_WIDTH + g * A_GROUP_DIM:A_WIDTH + (g + 1) * A_GROUP_DIM].astype(F32))
            vc = v - jnp.mean(v, axis=-1, keepdims=True)
            vn = vc * lax.rsqrt(jnp.mean(vc * vc, axis=-1, keepdims=True) + NORM_EPS) * g_ref[:, cols]
            mixed = _dot(w_ref[g], vn.astype(BF16)) + b_ref[g]
            o_ref[0, rows, cols] = (u * mixed).astype(o_ref.dtype)


def chunk_sgu(z, norm_g, w_s, b_s):
    b, l, _ = z.shape
    tl = min(512, l)
    blk = ZC_A // (2 * A_WIDTH)
    return pl.pallas_call(
        _sgu_kernel, grid=(b, l // tl),
        in_specs=[pl.BlockSpec((1, tl, 2 * A_WIDTH), lambda i, j: (i, j, blk)),
                  pl.BlockSpec((1, A_WIDTH), lambda i, j: (0, 0)),
                  pl.BlockSpec((A_GROUPS, A_CHUNK, A_CHUNK), lambda i, j: (0, 0, 0)),
                  pl.BlockSpec((A_GROUPS, A_CHUNK, 1), lambda i, j: (0, 0, 0))],
        out_specs=pl.BlockSpec((1, tl, A_WIDTH), lambda i, j: (i, j, 0)),
        out_shape=jax.ShapeDtypeStruct((b, l, A_WIDTH), BF16),
        compiler_params=_cparams("parallel", "parallel"), name="chunk_sgu",
    )(z, norm_g.reshape(1, A_WIDTH), w_s.astype(BF16), b_s.reshape(A_GROUPS, A_CHUNK, 1))


def _mla_prep_kernel(z_ref, qg_ref, kvg_ref, wq_ref, wkv_ref, cos_ref, sin_ref, *rest, q_scale):
    q_ref, k_ref, v_ref = rest[-3:]
    lat = z_ref[0, :, 0:Q_LORA].astype(F32)
    cq = (lat * lax.rsqrt(jnp.mean(lat * lat, axis=-1, keepdims=True) + NORM_EPS) * qg_ref[...]).astype(BF16)
    kvl = z_ref[0, :, Q_LORA:Q_LORA + KV_LORA].astype(F32)
    ckv = (kvl * lax.rsqrt(jnp.mean(kvl * kvl, axis=-1, keepdims=True) + NORM_EPS) * kvg_ref[...]).astype(BF16)
    cos = cos_ref[...]
    sin = sin_ref[...]
    base = Q_LORA + KV_LORA
    k_pe = z_ref[0, :, base:base + LANE].astype(F32) * cos + z_ref[0, :, base + LANE:base + 2 * LANE].astype(F32) * sin
    k_pe = k_pe.astype(k_ref.dtype)
    for h in range(C_HEADS):
        qh = _dot(cq, wq_ref[:, h * C_HEAD_COLS:(h + 1) * C_HEAD_COLS])
        q_pe = qh[:, LANE:2 * LANE] * cos + qh[:, 2 * LANE:3 * LANE] * sin
        q_ref[0, h, :, 0:LANE] = (qh[:, 0:LANE] * q_scale).astype(q_ref.dtype)
        q_ref[0, h, :, LANE:2 * LANE] = (q_pe * q_scale).astype(q_ref.dtype)
        kv = _dot(ckv, wkv_ref[:, h * 2 * LANE:(h + 1) * 2 * LANE])
        k_ref[0, h, :, 0:LANE] = kv[:, 0:LANE].astype(k_ref.dtype)
        k_ref[0, h, :, LANE:2 * LANE] = k_pe
        v_ref[0, h] = kv[:, LANE:2 * LANE].astype(v_ref.dtype)


def mla_prep(z, q_norm_g, kv_norm_g, wq_arr, wkv_arr, cos2, sin2, kv_rows=None, kv_row0=0, kv_into=None):
    b, l, _ = z.shape
    tl = min(256, l)
    blk = ZC_C // 1024
    kv_rows = l if kv_rows is None else kv_rows
    assert kv_row0 % tl == 0
    blk0 = kv_row0 // tl
    q_scale = (C_NOPE + C_ROPE) ** -0.5 * math.log2(math.e)
    in_specs = [pl.BlockSpec((1, tl, 1024), lambda i, j: (i, j, blk)),
                pl.BlockSpec((1, Q_LORA), lambda i, j: (0, 0)),
                pl.BlockSpec((1, KV_LORA), lambda i, j: (0, 0)),
                pl.BlockSpec((Q_LORA, C_HEADS * C_HEAD_COLS), lambda i, j: (0, 0)),
                pl.BlockSpec((KV_LORA, C_HEADS * 2 * LANE), lambda i, j: (0, 0)),
                pl.BlockSpec((tl, LANE), lambda i, j: (j, 0)),
                pl.BlockSpec((tl, LANE), lambda i, j: (j, 0))]
    args = [z, q_norm_g.reshape(1, Q_LORA), kv_norm_g.reshape(1, KV_LORA), wq_arr, wkv_arr, cos2, sin2]
    aliases = {}
    if kv_into is not None:
        in_specs += [pl.BlockSpec(memory_space=pl.ANY), pl.BlockSpec(memory_space=pl.ANY)]
        aliases = {len(args): 1, len(args) + 1: 2}
        args += list(kv_into)
    return pl.pallas_call(
        functools.partial(_mla_prep_kernel, q_scale=q_scale), grid=(b, l // tl),
        in_specs=in_specs,
        out_specs=[pl.BlockSpec((1, C_HEADS, tl, 2 * LANE), lambda i, j: (i, 0, j, 0)),
                   pl.BlockSpec((1, C_HEADS, tl, 2 * LANE), lambda i, j: (i, 0, blk0 + j, 0)),
                   pl.BlockSpec((1, C_HEADS, tl, LANE), lambda i, j: (i, 0, blk0 + j, 0))],
        out_shape=[jax.ShapeDtypeStruct((b, C_HEADS, l, 2 * LANE), BF16),
                   jax.ShapeDtypeStruct((b, C_HEADS, kv_rows, 2 * LANE), BF16),
                   jax.ShapeDtypeStruct((b, C_HEADS, kv_rows, LANE), BF16)],
        input_output_aliases=aliases,
        compiler_params=_cparams("parallel", "parallel"), name="mla_prep",
    )(*args)


FLASH_ROWS = 32


def _flash_kernel(q_ref, k_ref, v_ref, o_ref, s_ref, p_ref, m_ref, l_ref, acc_ref, *, tk, nk):
    tq = q_ref.shape[2]
    rb = min(FLASH_ROWS, tq)
    m_ref[...] = jnp.full(m_ref.shape, -jnp.inf, F32)
    l_ref[...] = jnp.zeros(l_ref.shape, F32)
    acc_ref[...] = jnp.zeros(acc_ref.shape, F32)

    for j in range(nk):
        s_buf, p_buf = s_ref.at[j % 2], p_ref.at[j % 2]
        keys = slice(j * tk, (j + 1) * tk)
        s_buf[...] = _dot_nt(q_ref[0, 0], k_ref[0, 0, keys, :])
        for r in range(tq // rb):
            rows = slice(r * rb, (r + 1) * rb)
            blocks = [s_buf[rows, t * LANE:(t + 1) * LANE] for t in range(tk // LANE)]
            mx = blocks[0]
            for blk in blocks[1:]:
                mx = jnp.maximum(mx, blk)
            m_old = m_ref[rows, :]
            m_new = jnp.maximum(m_old, jnp.broadcast_to(jnp.max(mx, axis=-1, keepdims=True), (rb, LANE)))
            alpha = jnp.exp2(m_old - m_new)
            lane_sum = None
            for t, blk in enumerate(blocks):
                p = jnp.exp2(blk - m_new)
                lane_sum = p if lane_sum is None else lane_sum + p
                p_buf[rows, t * LANE:(t + 1) * LANE] = p.astype(BF16)
            l_ref[rows, :] = alpha * l_ref[rows, :] + lane_sum
            m_ref[rows, :] = m_new
            acc_ref[rows, :] = alpha * acc_ref[rows, :]
        acc_ref[...] += _dot(p_buf[...], v_ref[0, 0, keys, :])
    o_ref[0] = (acc_ref[...] / jnp.sum(l_ref[...], axis=-1, keepdims=True)).astype(o_ref.dtype)


def flash_attention(q, k, v):
    b, h, lq, dq = q.shape
    lk = k.shape[2]
    tq = min(512, lq)
    tk = _tile(lk, 256)
    return pl.pallas_call(
        functools.partial(_flash_kernel, tk=tk, nk=lk // tk), grid=(b, h, lq // tq),
        in_specs=[pl.BlockSpec((1, 1, tq, dq), lambda i, j, t: (i, j, t, 0)),
                  pl.BlockSpec((1, 1, lk, dq), lambda i, j, t: (i, j, 0, 0)),
                  pl.BlockSpec((1, 1, lk, C_V), lambda i, j, t: (i, j, 0, 0))],
        out_specs=pl.BlockSpec((1, tq, C_V), lambda i, j, t: (i, t, j)),
        out_shape=jax.ShapeDtypeStruct((b, lq, h * C_V), BF16),
        scratch_shapes=[pltpu.VMEM((2, tq, tk), F32), pltpu.VMEM((2, tq, tk), BF16), pltpu.VMEM((tq, LANE), F32),
                        pltpu.VMEM((tq, LANE), F32), pltpu.VMEM((tq, C_V), F32)],
        compiler_params=_cparams("parallel", "parallel", "parallel"), name="flash_attention",
    )(q, k, v)


def _softplus(x):
    return jnp.maximum(x, 0.0) + jnp.log1p(jnp.exp(-jnp.abs(x)))


def _dn_prep_kernel(zm_ref, zp_ref, zn_ref, zg_ref, w_ref, alog_ref, dtb_ref, qkv_ref, gb_ref):
    j = pl.program_id(1)
    tl = zm_ref.shape[1]
    half = DN_CONV // 2
    keep_prev = (j > 0).astype(F32)
    keep_next = (j < pl.num_programs(1) - 1).astype(F32)
    for c in range(3 * B_HEADS):
        cols = slice(c * LANE, (c + 1) * LANE)
        prev = zp_ref[0, :, cols].astype(F32)[8:16] * keep_prev
        nxt = zn_ref[0, :, cols].astype(F32)[0:8] * keep_next
        ext = jnp.concatenate([prev, zm_ref[0, :, cols].astype(F32), nxt], axis=0)
        y = ext[8 - half:8 - half + tl] * w_ref[0:1, cols]
        for i in range(1, DN_CONV):
            y = y + ext[8 - half + i:8 - half + i + tl] * w_ref[i:i + 1, cols]
        y = _silu(y)
        if c < 2 * B_HEADS:
            y = y * lax.rsqrt(jnp.sum(y * y, axis=-1, keepdims=True) + NORM_EPS)
        qkv_ref[0, :, cols] = y.astype(qkv_ref.dtype)
    zg = zg_ref[0]
    lane = lax.broadcasted_iota(jnp.int32, zg.shape, 1)
    g = -jnp.exp(alog_ref[...]) * _softplus(zg + dtb_ref[...])
    gb_ref[0] = jnp.where(lane < 2 * B_HEADS, g, _sigmoid(zg))


def deltanet_prep(z, zg, conv_w, a_log, dt_bias):
    b, l, _ = z.shape
    tl = min(256, l)
    wq = 3 * B_WIDTH
    nb16 = l // 16
    pad = LANE - 2 * B_HEADS
    alog = jnp.pad(a_log.reshape(1, -1), ((0, 0), (0, pad)))
    dtb = jnp.pad(dt_bias.reshape(1, -1), ((0, 0), (0, pad)))
    return pl.pallas_call(
        _dn_prep_kernel, grid=(b, l // tl),
        in_specs=[pl.BlockSpec((1, tl, wq), lambda i, j: (i, j, 0)),
                  pl.BlockSpec((1, 16, wq), lambda i, j: (i, jnp.maximum(j * (tl // 16) - 1, 0), 0)),
                  pl.BlockSpec((1, 16, wq), lambda i, j: (i, jnp.minimum((j + 1) * (tl // 16), nb16 - 1), 0)),
                  pl.BlockSpec((1, tl, LANE), lambda i, j: (i, j, 0)),
                  pl.BlockSpec((DN_CONV, wq), lambda i, j: (0, 0)),
                  pl.BlockSpec((1, LANE), lambda i, j: (0, 0)),
                  pl.BlockSpec((1, LANE), lambda i, j: (0, 0))],
        out_specs=[pl.BlockSpec((1, tl, wq), lambda i, j: (i, j, 0)),
                   pl.BlockSpec((1, tl, LANE), lambda i, j: (i, j, 0))],
        out_shape=[jax.ShapeDtypeStruct((b, l, wq), BF16), jax.ShapeDtypeStruct((b, l, LANE), F32)],
        compiler_params=_cparams("parallel", "parallel"), name="deltanet_prep",
    )(z, z, z, zg, conv_w, alog, dtb)


def _dot_bf16(a, b):
    return _dot(a.astype(BF16), b.astype(BF16))


def _dn_intra_kernel(qkv_ref, gb_ref, u_ref, w_ref, qg_ref, kg_ref, qk_ref, gl_ref, *, chunks, passes):
    c = DN_CHUNK
    row = lax.broadcasted_iota(jnp.int32, (c, LANE), 0)
    lane = lax.broadcasted_iota(jnp.int32, (c, LANE), 1)
    col = jnp.where(lane < c, lane, lane - c)
    fwd = lane < c
    bwd = jnp.logical_not(fwd)
    incl = (fwd & (row >= col)) | (bwd & (row <= col))
    strict = (fwd & (row > col)) | (bwd & (row < col))
    same16 = lax.shift_right_logical(row, 4) == lax.shift_right_logical(col, 4)
    same32 = lax.shift_right_logical(row, 5) == lax.shift_right_logical(col, 5)
    diag = row == col
    eye2 = diag.astype(F32)
    r64 = lax.broadcasted_iota(jnp.int32, (c, c), 0)
    c64 = lax.broadcasted_iota(jnp.int32, (c, c), 1)
    tri_lo = (r64 >= c64).astype(F32)
    tri_up = (r64 <= c64).astype(F32)
    scale = B_HEAD_DIM ** -0.5
    dotp = _dot_x3 if passes == 3 else _dot_bf16

    def pick(x, idx):
        return jnp.sum(jnp.where(lane == idx, x, 0.0), axis=-1, keepdims=True)

    def blockdiag(y2):
        return jnp.concatenate([jnp.where(fwd, y2, 0.0), jnp.where(fwd, 0.0, y2)], axis=0)

    def mm(xs, ys):
        return [dotp(x2, blockdiag(y2)) for x2, y2 in zip(xs, ys)]

    rows = [slice(ci * c, (ci + 1) * c) for ci in range(chunks)]
    gb = [gb_ref[0, rs, :] for rs in rows]
    cum_f = [jnp.dot(tri_lo, g, preferred_element_type=F32, precision=HIGHEST) for g in gb]
    cum_b = [jnp.dot(tri_up, g, preferred_element_type=F32, precision=HIGHEST) for g in gb]
    tot_f = [x[c - 1:c, :] for x in cum_f]
    tot_b = [x[0:1, :] for x in cum_b]

    units = [(ci, h) for ci in range(chunks) for h in range(B_HEADS)]
    q = [qkv_ref[0, rows[ci], h * LANE:(h + 1) * LANE] for ci, h in units]
    k = [qkv_ref[0, rows[ci], B_WIDTH + h * LANE:B_WIDTH + (h + 1) * LANE] for ci, h in units]
    v = [qkv_ref[0, rows[ci], 2 * B_WIDTH + h * LANE:2 * B_WIDTH + (h + 1) * LANE].astype(F32) for ci, h in units]
    k2 = [jnp.concatenate([x, x], axis=0) for x in k]
    kk2 = [_dot_nt(x, y) for x, y in zip(k, k2)]
    qk2 = [_dot_nt(x, y) for x, y in zip(q, k2)]
    cf = [pick(cum_f[ci], h) for ci, h in units]
    cb = [pick(cum_b[ci], B_HEADS + h) for ci, h in units]
    bf = [pick(gb[ci], 2 * B_HEADS + h) for ci, h in units]
    bb = [pick(gb[ci], 3 * B_HEADS + h) for ci, h in units]
    lf = [pick(jnp.broadcast_to(tot_f[ci], (c, LANE)), h) for ci, h in units]
    lb = [pick(jnp.broadcast_to(tot_b[ci], (c, LANE)), B_HEADS + h) for ci, h in units]
    c2 = [jnp.where(fwd, x, y) for x, y in zip(cf, cb)]
    r2 = [jnp.sum(jnp.where(diag, x, 0.0), axis=0, keepdims=True) for x in c2]
    decay2 = [jnp.where(incl, jnp.exp(jnp.where(incl, x - y, 0.0)), 0.0) for x, y in zip(c2, r2)]
    l2 = [jnp.where(strict, jnp.where(fwd, x, y) * kk * dc, 0.0) for x, y, kk, dc in zip(bf, bb, kk2, decay2)]
    mp = [jnp.where(same16, -x, 0.0) for x in l2]
    p = [eye2 + x for x in mp]
    for _ in range(3):
        mp = mm(mp, mp)
        p = [x + y for x, y in zip(p, mm(p, mp))]
    off = [jnp.where(same32 & jnp.logical_not(same16), x, 0.0) for x in l2]
    p = [x - y for x, y in zip(p, mm(mm(p, off), p))]
    off = [jnp.where(same32, 0.0, x) for x in l2]
    p = [x - y for x, y in zip(p, mm(mm(p, off), p))]
    ef = [jnp.exp(x) for x in cf]
    eb = [jnp.exp(x) for x in cb]
    zero = jnp.zeros((c, LANE), F32)
    sol = []
    for i in range(len(units)):
        kf = k[i].astype(F32)
        rhs = jnp.concatenate([
            jnp.concatenate([v[i] * bf[i], kf * (bf[i] * ef[i]), zero, zero], axis=1),
            jnp.concatenate([zero, zero, v[i] * bb[i], kf * (bb[i] * eb[i])], axis=1)], axis=0)
        sol.append(dotp(p[i], rhs))
    for i, (ci, h) in enumerate(units):
        hc = slice(h * LANE, (h + 1) * LANE)
        rs = rows[ci]
        kf = k[i].astype(F32)
        qf = q[i].astype(F32)
        u_ref[0, 0, rs, hc] = sol[i][:, 0:LANE]
        w_ref[0, 0, rs, hc] = sol[i][:, LANE:2 * LANE].astype(w_ref.dtype)
        u_ref[1, 0, rs, hc] = sol[i][:, 2 * LANE:3 * LANE]
        w_ref[1, 0, rs, hc] = sol[i][:, 3 * LANE:4 * LANE].astype(w_ref.dtype)
        qg_ref[0, 0, rs, hc] = (qf * (ef[i] * scale)).astype(qg_ref.dtype)
        qg_ref[1, 0, rs, hc] = (qf * (eb[i] * scale)).astype(qg_ref.dtype)
        kg_ref[0, 0, rs, hc] = (kf * jnp.exp(lf[i] - cf[i])).astype(kg_ref.dtype)
        kg_ref[1, 0, rs, hc] = (kf * jnp.exp(lb[i] - cb[i])).astype(kg_ref.dtype)
        qk_ref[0, rs, hc] = (qk2[i] * decay2[i] * scale).astype(qk_ref.dtype)
    r8 = lax.broadcasted_iota(jnp.int32, (2 * 8, LANE), 0)
    l8 = lax.broadcasted_iota(jnp.int32, (2 * 8, LANE), 1)
    want = jnp.where(r8 < 8, r8, r8 - 8 + B_HEADS)
    for ci in range(chunks):
        src = jnp.concatenate([jnp.broadcast_to(tot_f[ci], (8, LANE)), jnp.broadcast_to(tot_b[ci], (8, LANE))], axis=0)
        tot = jnp.sum(jnp.where(l8 == want, src, 0.0), axis=-1, keepdims=True)
        gl_ref[0, ci] = jnp.broadcast_to(jnp.exp(tot), (2 * 8, LANE))


DN_INTRA_CHUNKS = 8
DN_INTRA_PASSES = 1
DN_SCAN_CHUNKS = 8


def deltanet_intra(qkv, gb):
    b, l, _ = qkv.shape
    nc = l // DN_CHUNK
    chunks = math.gcd(DN_INTRA_CHUNKS, nc)
    tl = chunks * DN_CHUNK
    dir_spec = pl.BlockSpec((2, 1, tl, B_WIDTH), lambda i, j: (0, i, j, 0))
    return pl.pallas_call(
        functools.partial(_dn_intra_kernel, chunks=chunks, passes=DN_INTRA_PASSES), grid=(b, nc // chunks),
        in_specs=[pl.BlockSpec((1, tl, 3 * B_WIDTH), lambda i, j: (i, j, 0)),
                  pl.BlockSpec((1, tl, LANE), lambda i, j: (i, j, 0))],
        out_specs=[dir_spec, dir_spec, dir_spec, dir_spec,
                   pl.BlockSpec((1, tl, B_WIDTH), lambda i, j: (i, j, 0)),
                   pl.BlockSpec((1, chunks, 16, LANE), lambda i, j: (i, j, 0, 0))],
        out_shape=[jax.ShapeDtypeStruct((2, b, l, B_WIDTH), F32),
                   jax.ShapeDtypeStruct((2, b, l, B_WIDTH), BF16),
                   jax.ShapeDtypeStruct((2, b, l, B_WIDTH), BF16),
                   jax.ShapeDtypeStruct((2, b, l, B_WIDTH), BF16),
                   jax.ShapeDtypeStruct((b, l, B_WIDTH), BF16),
                   jax.ShapeDtypeStruct((b, nc, 16, LANE), F32)],
        compiler_params=_cparams("parallel", "parallel"), name="deltanet_intra",
    )(qkv, gb)


def _dn_scan_kernel(uf_ref, wf_ref, qgf_ref, kgf_ref, qkf_ref, glf_ref,
                    ub_ref, wb_ref, qgb_ref, kgb_ref, qkb_ref, glb_ref, s0_ref,
                    of_ref, ob_ref, sfin_ref, s_ref, *, chunks):
    j = pl.program_id(1)
    c = DN_CHUNK

    @pl.when(j == 0)
    def _():
        s_ref[...] = s0_ref[0]

    lane = lax.broadcasted_iota(jnp.int32, (c, LANE), 1)
    zeros_b = jnp.zeros((c, LANE), BF16)
    dirs = ((uf_ref, wf_ref, qgf_ref, kgf_ref, qkf_ref, glf_ref, of_ref),
            (ub_ref, wb_ref, qgb_ref, kgb_ref, qkb_ref, glb_ref, ob_ref))
    units = [(d, h) for d in range(2) for h in range(B_HEADS)]
    cols = [slice(h * LANE, (h + 1) * LANE) for _, h in units]
    state = [s_ref[d * B_HEADS + h] for d, h in units]
    for step in range(chunks):
        chunk_of = (step, chunks - 1 - step)
        rows = [slice(chunk_of[d] * c, (chunk_of[d] + 1) * c) for d, _ in units]
        r = [_dot(jnp.concatenate([dirs[d][1][0, 0, rs, hc], dirs[d][2][0, 0, rs, hc]], axis=0), s.astype(BF16))
             for (d, _), hc, rs, s in zip(units, cols, rows, state)]
        vb = [(dirs[d][0][0, 0, rs, hc] - ri[0:c]).astype(BF16) for (d, _), hc, rs, ri in zip(units, cols, rows, r)]
        new_state = []
        for (d, h), hc, rs, s, ri, vi in zip(units, cols, rows, state, r, vb):
            qk2 = dirs[d][4][0, rs, hc]
            if d == 0:
                intra = _dot(jnp.where(lane < c, qk2, jnp.zeros_like(qk2)), jnp.concatenate([vi, zeros_b], axis=0))
            else:
                intra = _dot(jnp.where(lane < c, jnp.zeros_like(qk2), qk2), jnp.concatenate([zeros_b, vi], axis=0))
            dirs[d][6][0, rs, hc] = ri[c:2 * c] + intra
            gl = dirs[d][5][0, chunk_of[d], d * 8 + h:d * 8 + h + 1, :]
            new_state.append(s * gl + _dot_tn(dirs[d][3][0, 0, rs, hc], vi))
        state = new_state
    for (d, h), s in zip(units, state):
        s_ref[d * B_HEADS + h] = s

    @pl.when(j == pl.num_programs(1) - 1)
    def _():
        sfin_ref[0] = s_ref[...]


def deltanet_scan(u, w, qg, kg, qk, gl, s0):
    _, b, l, _ = u.shape
    chunks = math.gcd(DN_SCAN_CHUNKS, l // DN_CHUNK)
    tl = chunks * DN_CHUNK
    nb = l // tl
    fdir = pl.BlockSpec((1, 1, tl, B_WIDTH), lambda i, j: (0, i, j, 0))
    bdir = pl.BlockSpec((1, 1, tl, B_WIDTH), lambda i, j: (1, i, nb - 1 - j, 0))
    fqk = pl.BlockSpec((1, tl, B_WIDTH), lambda i, j: (i, j, 0))
    bqk = pl.BlockSpec((1, tl, B_WIDTH), lambda i, j: (i, nb - 1 - j, 0))
    fgl = pl.BlockSpec((1, chunks, 16, LANE), lambda i, j: (i, j, 0, 0))
    bgl = pl.BlockSpec((1, chunks, 16, LANE), lambda i, j: (i, nb - 1 - j, 0, 0))
    st = pl.BlockSpec((1, 2 * B_HEADS, B_HEAD_DIM, B_HEAD_DIM), lambda i, j: (i, 0, 0, 0))
    return pl.pallas_call(
        functools.partial(_dn_scan_kernel, chunks=chunks), grid=(b, nb),
        in_specs=[fdir, fdir, fdir, fdir, fqk, fgl, bdir, bdir, bdir, bdir, bqk, bgl, st],
        out_specs=[fqk, bqk, st],
        out_shape=[jax.ShapeDtypeStruct((b, l, B_WIDTH), F32),
                   jax.ShapeDtypeStruct((b, l, B_WIDTH), F32),
                   jax.ShapeDtypeStruct((b, 2 * B_HEADS, B_HEAD_DIM, B_HEAD_DIM), F32)],
        scratch_shapes=[pltpu.VMEM((2 * B_HEADS, B_HEAD_DIM, B_HEAD_DIM), F32)],
        compiler_params=_cparams("parallel", "arbitrary"), name="deltanet_scan",
    )(u, w, qg, kg, qk, gl, u, w, qg, kg, qk, gl, s0)


def _dn_out_kernel(of_ref, ob_ref, gate_ref, g_ref, o_ref):
    for h in range(B_HEADS):
        hc = slice(h * LANE, (h + 1) * LANE)
        o = of_ref[0, :, hc] + ob_ref[0, :, hc]
        y = o * lax.rsqrt(jnp.mean(o * o, axis=-1, keepdims=True) + NORM_EPS) * g_ref[...]
        o_ref[0, :, hc] = (y * _silu(gate_ref[0, :, hc].astype(F32))).astype(o_ref.dtype)


def deltanet_out(o_f, o_b, z, norm_g):
    b, l, _ = o_f.shape
    tl = min(512, l)
    blk = ZC_GATE // B_WIDTH
    return pl.pallas_call(
        _dn_out_kernel, grid=(b, l // tl),
        in_specs=[pl.BlockSpec((1, tl, B_WIDTH), lambda i, j: (i, j, 0)),
                  pl.BlockSpec((1, tl, B_WIDTH), lambda i, j: (i, j, 0)),
                  pl.BlockSpec((1, tl, B_WIDTH), lambda i, j: (i, j, blk)),
                  pl.BlockSpec((1, B_HEAD_DIM), lambda i, j: (0, 0))],
        out_specs=pl.BlockSpec((1, tl, B_WIDTH), lambda i, j: (i, j, 0)),
        out_shape=jax.ShapeDtypeStruct((b, l, B_WIDTH), BF16),
        compiler_params=_cparams("parallel", "parallel"), name="deltanet_out",
    )(o_f, o_b, z, norm_g.reshape(1, B_HEAD_DIM))


def _route_kernel(lg_ref, info_ref, cnt_ref, carry_ref):
    i = pl.program_id(0)

    @pl.when(i == 0)
    def _():
        carry_ref[...] = jnp.zeros_like(carry_ref)

    lg = lg_ref[...]
    tl = lg.shape[0]
    lane = lax.broadcasted_iota(jnp.int32, lg.shape, 1)
    valid = lane < N_EXPERTS
    lg = jnp.where(valid, lg, -jnp.inf)
    e = jnp.exp(lg - jnp.max(lg, axis=-1, keepdims=True))
    p = e / jnp.sum(e, axis=-1, keepdims=True)
    p = jnp.where(valid, p, -1.0)
    p1 = jnp.max(p, axis=-1, keepdims=True)
    i1 = jnp.min(jnp.where(p == p1, lane, LANE), axis=-1, keepdims=True)
    pm = jnp.where(lane == i1, -1.0, p)
    p2 = jnp.max(pm, axis=-1, keepdims=True)
    i2 = jnp.min(jnp.where(pm == p2, lane, LANE), axis=-1, keepdims=True)
    tot = p1 + p2
    w1, w2 = p1 / tot, p2 / tot
    hit1, hit2 = lane == i1, lane == i2
    onehot = (hit1 | hit2).astype(F32)
    r = lax.broadcasted_iota(jnp.int32, (tl, tl), 0)
    c = lax.broadcasted_iota(jnp.int32, (tl, tl), 1)
    before = _dot((r > c).astype(BF16), onehot.astype(BF16)) + carry_ref[...]
    r1 = jnp.sum(jnp.where(hit1, before, 0.0), axis=-1, keepdims=True)
    r2 = jnp.sum(jnp.where(hit2, before, 0.0), axis=-1, keepdims=True)
    carry_ref[...] += jnp.sum(onehot, axis=0, keepdims=True)
    cnt_ref[...] = jnp.broadcast_to(carry_ref[...], cnt_ref.shape)
    info = jnp.where(lane == 0, i1.astype(F32), 0.0)
    info = jnp.where(lane == 1, i2.astype(F32), info)
    info = jnp.where(lane == 2, r1, info)
    info = jnp.where(lane == 3, r2, info)
    info = jnp.where(lane == 4, w1, info)
    info = jnp.where(lane == 5, w2, info)
    info_ref[...] = info


def moe_route(logits):
    t = logits.shape[0]
    tl = min(512, t)
    return pl.pallas_call(
        _route_kernel, grid=(t // tl,),
        in_specs=[pl.BlockSpec((tl, LANE), lambda i: (i, 0))],
        out_specs=[pl.BlockSpec((tl, LANE), lambda i: (i, 0)), pl.BlockSpec((8, LANE), lambda i: (0, 0))],
        out_shape=[jax.ShapeDtypeStruct((t, LANE), F32), jax.ShapeDtypeStruct((8, LANE), F32)],
        scratch_shapes=[pltpu.VMEM((1, LANE), F32)],
        compiler_params=_cparams("arbitrary"), name="moe_route",
    )(logits)


def _dispatch_kernel(pos_ref, h_ref, xs_in_ref, xs_ref, sem):
    del xs_in_ref
    tb = pos_ref.shape[2] // TOP_K

    def copy(src_row, dst_row):
        return pltpu.make_async_copy(h_ref.at[pl.ds(pl.multiple_of(src_row * ROW_TILE, ROW_TILE), ROW_TILE)],
                                     xs_ref.at[pl.ds(pl.multiple_of(dst_row * ROW_TILE, ROW_TILE), ROW_TILE)], sem)

    def issue(j, carry):
        for k in range(TOP_K):
            copy(j, pos_ref[0, 0, TOP_K * j + k]).start()
        return carry

    lax.fori_loop(0, tb, issue, 0, unroll=8)
    for _ in range(TOP_K):
        pltpu.make_async_copy(h_ref, xs_ref.at[pl.ds(0, tb * ROW_TILE)], sem).wait()


def moe_dispatch(h_rows, pos, n_rows):
    t = pos.shape[0]
    tb = min(256, t)
    d_rows = h_rows.shape[0] // t
    assert d_rows == ROW_TILE
    zeros = jnp.zeros((n_rows * ROW_TILE, LANE), h_rows.dtype)
    return pl.pallas_call(
        _dispatch_kernel, grid=(t // tb,),
        in_specs=[pl.BlockSpec((1, 1, TOP_K * tb), lambda i: (i, 0, 0), memory_space=pltpu.SMEM),
                  pl.BlockSpec((tb * ROW_TILE, LANE), lambda i: (i, 0)),
                  pl.BlockSpec(memory_space=pl.ANY)],
        out_specs=pl.BlockSpec(memory_space=pl.ANY),
        out_shape=jax.ShapeDtypeStruct(zeros.shape, zeros.dtype),
        scratch_shapes=[pltpu.SemaphoreType.DMA],
        input_output_aliases={2: 0},
        compiler_params=_cparams("arbitrary"), name="moe_dispatch",
    )(pos.reshape(t // tb, 1, TOP_K * tb), h_rows, zeros)


def _moe_up_kernel(te_ref, tv_ref, xs_ref, wg_ref, wu_ref, o_ref, a_ref):
    del te_ref
    i = pl.program_id(0)
    tm = a_ref.shape[0]

    @pl.when(pl.program_id(1) == 0)
    def _():
        for c in range(ROW_TILE):
            lo, hi = _load_token_rows(xs_ref, tm, c)
            a_ref[:, c * LANE:(c + 1) * LANE] = lo.astype(a_ref.dtype)
            a_ref[:, (c + ROW_TILE) * LANE:(c + ROW_TILE + 1) * LANE] = hi.astype(a_ref.dtype)

    @pl.when(tv_ref[i] != 0)
    def _():
        a = a_ref[...]
        g = _dot(a, wg_ref[0].astype(BF16))
        u = _dot(a, wu_ref[0].astype(BF16))
        o_ref[...] = (_silu(g) * u).astype(o_ref.dtype)

    @pl.when(tv_ref[i] == 0)
    def _():
        o_ref[...] = jnp.zeros_like(o_ref)


def moe_up(xs_rows, wg, wu, tile_expert, tile_valid, tm, tn):
    r = xs_rows.shape[0] // ROW_TILE
    d, n = wg.shape[1], wg.shape[2]
    tn = _tile(n, tn)
    nj = n // tn

    def w_index(i, j, te, tv):
        return te[i], 0, jnp.where(tv[i] != 0, j, nj - 1)

    grid_spec = pltpu.PrefetchScalarGridSpec(
        num_scalar_prefetch=2, grid=(r // tm, nj),
        in_specs=[pl.BlockSpec((tm * ROW_TILE, LANE), lambda i, j, te, tv: (i, 0)),
                  pl.BlockSpec((1, d, tn), w_index),
                  pl.BlockSpec((1, d, tn), w_index)],
        out_specs=pl.BlockSpec((tm, tn), lambda i, j, te, tv: (i, j)),
        scratch_shapes=[pltpu.VMEM((tm, d), BF16)])
    return pl.pallas_call(
        _moe_up_kernel, grid_spec=grid_spec,
        out_shape=jax.ShapeDtypeStruct((r, n), BF16),
        compiler_params=_cparams("parallel", "arbitrary"), name="moe_up",
    )(tile_expert, tile_valid, xs_rows, wg, wu)


def _moe_down_kernel(te_ref, tv_ref, a_ref, w_ref, o_ref, acc_ref):
    del te_ref
    i = pl.program_id(0)
    k = pl.program_id(1)

    valid = tv_ref[i] != 0

    @pl.when(jnp.logical_and(valid, k == 0))
    def _():
        acc_ref[...] = _dot(a_ref[...], w_ref[0].astype(BF16))

    @pl.when(jnp.logical_and(valid, k > 0))
    def _():
        acc_ref[...] += _dot(a_ref[...], w_ref[0].astype(BF16))

    @pl.when(jnp.logical_and(jnp.logical_not(valid), k == 0))
    def _():
        acc_ref[...] = jnp.zeros_like(acc_ref)

    @pl.when(k == pl.num_programs(1) - 1)
    def _():
        _store_token_rows(o_ref, acc_ref[...])


def moe_down(hid, wd, tile_expert, tile_valid, tm, tk):
    r, kdim = hid.shape
    n = wd.shape[2]
    assert n == D_TOKEN
    tk = _tile(kdim, tk)
    nk = kdim // tk
    grid_spec = pltpu.PrefetchScalarGridSpec(
        num_scalar_prefetch=2, grid=(r // tm, nk),
        in_specs=[pl.BlockSpec((tm, tk), lambda i, k, te, tv: (i, jnp.where(tv[i] != 0, k, nk - 1))),
                  pl.BlockSpec((1, tk, n), lambda i, k, te, tv: (te[i], jnp.where(tv[i] != 0, k, nk - 1), 0))],
        out_specs=pl.BlockSpec((tm * ROW_TILE, LANE), lambda i, k, te, tv: (i, 0)),
        scratch_shapes=[pltpu.VMEM((tm, n), F32)])
    return pl.pallas_call(
        _moe_down_kernel, grid_spec=grid_spec,
        out_shape=jax.ShapeDtypeStruct((r * ROW_TILE, LANE), U32),
        compiler_params=_cparams("parallel", "arbitrary"), name="moe_down",
    )(tile_expert, tile_valid, hid, wd)


def _combine_kernel(pos_ref, pos_next_ref, ys_ref, x_ref, gate_ref, wt_ref, ng_ref, o_ref, buf_ref, sem, *, final_norm):
    i = pl.program_id(0)
    tb = x_ref.shape[0]
    slot = lax.rem(i, 2)

    def gather(p_ref, s):
        def issue(j, carry):
            for k in range(TOP_K):
                src_row = p_ref[0, 0, TOP_K * j + k]
                pltpu.make_async_copy(
                    ys_ref.at[pl.ds(pl.multiple_of(src_row * ROW_TILE, ROW_TILE), ROW_TILE)],
                    buf_ref.at[s, k, pl.ds(pl.multiple_of(j * ROW_TILE, ROW_TILE), ROW_TILE)], sem.at[s]).start()
            return carry

        lax.fori_loop(0, tb, issue, 0, unroll=8)

    @pl.when(i == 0)
    def _():
        gather(pos_ref, 0)

    @pl.when(i + 1 < pl.num_programs(0))
    def _():
        gather(pos_next_ref, 1 - slot)

    for k in range(TOP_K):
        pltpu.make_async_copy(ys_ref.at[pl.ds(0, tb * ROW_TILE)], buf_ref.at[slot, k], sem.at[slot]).wait()
    w0 = wt_ref[:, 0:1]
    w1 = wt_ref[:, 1:2]
    sq = jnp.zeros((tb, 1), F32)
    for c in range(ROW_TILE):
        first = _load_token_rows(buf_ref.at[slot, 0], tb, c)
        second = _load_token_rows(buf_ref.at[slot, 1], tb, c)
        for half in range(2):
            cols = slice((c + half * ROW_TILE) * LANE, (c + half * ROW_TILE + 1) * LANE)
            f = first[half] * w0 + second[half] * w1
            y = x_ref[:, cols] + gate_ref[0, :, cols] * f
            o_ref[:, cols] = y
            sq = sq + jnp.sum(y * y, axis=-1, keepdims=True)
    if final_norm:
        o_ref[...] = o_ref[...] * lax.rsqrt(sq * (1.0 / D_TOKEN) + NORM_EPS) * ng_ref[...]


def moe_combine(ys_rows, pos, wts, x, gate, rows_per_batch, final_norm_g=None):
    t, d = x.shape
    tb = min(128, t)
    bsz = gate.shape[0]
    final_norm = final_norm_g is not None
    ng = (final_norm_g if final_norm else jnp.ones((d,), F32)).reshape(1, d)
    nb = t // tb
    pos_blocks = pos.reshape(nb, 1, TOP_K * tb)
    return pl.pallas_call(
        functools.partial(_combine_kernel, final_norm=final_norm), grid=(nb,),
        in_specs=[pl.BlockSpec((1, 1, TOP_K * tb), lambda i: (i, 0, 0), memory_space=pltpu.SMEM),
                  pl.BlockSpec((1, 1, TOP_K * tb), lambda i: (jnp.minimum(i + 1, nb - 1), 0, 0),
                               memory_space=pltpu.SMEM),
                  pl.BlockSpec(memory_space=pl.ANY),
                  pl.BlockSpec((tb, d), lambda i: (i, 0)),
                  pl.BlockSpec((1, 1, d), lambda i: ((i * tb) // rows_per_batch, 0, 0)),
                  pl.BlockSpec((tb, TOP_K), lambda i: (i, 0)),
                  pl.BlockSpec((1, d), lambda i: (0, 0))],
        out_specs=pl.BlockSpec((tb, d), lambda i: (i, 0)),
        out_shape=jax.ShapeDtypeStruct((t, d), F32),
        scratch_shapes=[pltpu.VMEM((2, TOP_K, tb * ROW_TILE, LANE), U32), pltpu.SemaphoreType.DMA((2,))],
        compiler_params=_cparams("arbitrary"), name="moe_combine",
    )(pos_blocks, pos_blocks, ys_rows, x, gate.reshape(bsz, 1, d), wts, ng)


MOE_TM = 1024


def moe_ffn(h_rows, logits, wg, wu, wd, x, gate, rows_per_batch, final_norm_g=None):
    t, d = x.shape
    tm = min(MOE_TM, t)
    info, counts = moe_route(logits)
    sizes = counts[0, :N_EXPERTS].astype(jnp.int32)
    padded = ((sizes + tm - 1) // tm) * tm
    ends = jnp.cumsum(padded)
    starts = ends - padded
    n_tiles = (t * TOP_K) // tm + N_EXPERTS
    n_rows = n_tiles * tm
    experts = info[:, 0:TOP_K].astype(jnp.int32)
    pos = starts[experts] + info[:, 2:2 + TOP_K].astype(jnp.int32)
    wts = info[:, 4:4 + TOP_K]
    tile_start = jnp.arange(n_tiles, dtype=jnp.int32) * tm
    tile_expert = jnp.minimum(jnp.sum((tile_start[:, None] >= ends[None, :]).astype(jnp.int32), axis=1), N_EXPERTS - 1)
    tile_valid = (tile_start < ends[-1]).astype(jnp.int32)
    last_expert = jnp.max(jnp.where(sizes > 0, jnp.arange(N_EXPERTS, dtype=jnp.int32), 0))
    tile_expert = jnp.where(tile_valid != 0, tile_expert, last_expert)
    xs_rows = moe_dispatch(h_rows, pos, n_rows)
    hid = moe_up(xs_rows, wg, wu, tile_expert, tile_valid, tm, 512)
    ys_rows = moe_down(hid, wd.astype(BF16), tile_expert, tile_valid, tm, 1792)
    return moe_combine(ys_rows, pos, wts, x, gate, rows_per_batch, final_norm_g)


def _rot_cols(w):
    f = ROPE_AXIS_FREQS
    return jnp.concatenate([-w[:, f:2 * f], w[:, 0:f], -w[:, 3 * f:4 * f], w[:, 2 * f:3 * f]], axis=1)


def _arrange_w_in(w):
    d = w.shape[0]
    a = w[:, 0:2 * A_WIDTH]
    off = 2 * A_WIDTH
    qkv_gate = w[:, off:off + 4 * B_WIDTH]
    logit = w[:, off + 4 * B_WIDTH:off + 4 * B_WIDTH + 4 * B_HEADS]
    off = off + 4 * B_WIDTH + 4 * B_HEADS
    lat = w[:, off:off + Q_LORA + KV_LORA]
    k_pe = w[:, off + Q_LORA + KV_LORA:off + Q_LORA + KV_LORA + C_ROPE]
    z64 = jnp.zeros((d, LANE - C_ROPE), w.dtype)
    main = jnp.concatenate([qkv_gate, a, lat, k_pe, z64, _rot_cols(k_pe), z64], axis=1).astype(BF16)
    logit = jnp.pad(logit, ((0, 0), (0, LANE - 4 * B_HEADS))).astype(BF16)
    return main, logit


def _arrange_w_uq(w):
    k = w.shape[0]
    w = w.reshape(k, C_HEADS, C_NOPE + C_ROPE)
    z64 = jnp.zeros((k, C_HEADS, LANE - C_ROPE), w.dtype)
    pe = w[:, :, C_NOPE:]
    pe_rot = jnp.stack([_rot_cols(pe[:, h]) for h in range(C_HEADS)], axis=1)
    return jnp.concatenate([w[:, :, :C_NOPE], pe, z64, pe_rot, z64], axis=2).reshape(k, C_HEADS * C_HEAD_COLS).astype(BF16)


def _rope_tables(n):
    rows = n // GRID_W
    row = np.repeat(np.arange(rows, dtype=np.float32), GRID_W)
    col = np.tile(np.arange(GRID_W, dtype=np.float32), rows)
    inv = np.power(np.float32(ROPE_BASE), -np.arange(ROPE_AXIS_FREQS, dtype=np.float32) / np.float32(ROPE_AXIS_FREQS))
    ar = row[:, None] * inv.astype(np.float32)
    ac = col[:, None] * inv.astype(np.float32)
    ang = np.concatenate([ar, ar, ac, ac], axis=-1).astype(np.float32)
    pad = ((0, 0), (0, LANE - C_ROPE))
    return (jnp.asarray(np.pad(np.cos(ang).astype(np.float32), pad)),
            jnp.asarray(np.pad(np.sin(ang).astype(np.float32), pad)))


def _mixer_branches(z, zg, p, cos2, sin2, **kv_placement):
    out_a = chunk_sgu(z, p["sgu_norm_g"], p["sgu_w"], p["sgu_b"])
    qkv, gb = deltanet_prep(z, zg, p["dn_conv_w"], p["dn_a_log"], p["dn_dt_bias"])
    intra = deltanet_intra(qkv, gb)
    q, k, v = mla_prep(z, p["mla_q_norm_g"], p["mla_kv_norm_g"], p["wq_arr"], p["wkv_arr"], cos2, sin2, **kv_placement)
    return out_a, intra, (q, k, v)


def kernel(x, c, ctx, c_ctx, ada_w, ada_b, norm1_g, norm2_g, w_in, sgu_norm_g, sgu_w, sgu_b, dn_conv_w, dn_a_log, dn_dt_bias, dn_norm_g, mla_q_norm_g, mla_w_uq, mla_kv_norm_g, mla_w_ukv, w_out, ffn_w_gate, ffn_w_up, ffn_w_down, moe_router, moe_w_gate, moe_w_up, moe_w_down, final_norm_g):
    b, n, d = x.shape
    lc = ctx.shape[1]
    depth = ada_w.shape[0]
    assert d == D_TOKEN
    cos_lat, sin_lat = _rope_tables(n)
    cos_ctx = jnp.pad(jnp.ones((lc, C_ROPE), F32), ((0, 0), (0, LANE - C_ROPE)))
    sin_ctx = jnp.zeros((lc, LANE), F32)

    cc = jnp.zeros((8, d), F32).at[0:b].set(c).at[b].set(c_ctx)
    mod_all = ada_modulation(cc, ada_w, ada_b)

    xc = ctx
    x_rows = None
    for i in range(depth):
        last = i == depth - 1
        mod = mod_all[i, 0:b].reshape(b, 6, d)
        mod_c = jnp.broadcast_to(mod_all[i, b].reshape(1, 6, d), (b, 6, d))
        w_main, w_logit = _arrange_w_in(w_in[i])
        p = dict(sgu_norm_g=sgu_norm_g[i], sgu_w=sgu_w[i], sgu_b=sgu_b[i], dn_conv_w=dn_conv_w[i],
                 dn_a_log=dn_a_log[i], dn_dt_bias=dn_dt_bias[i], mla_q_norm_g=mla_q_norm_g[i],
                 mla_kv_norm_g=mla_kv_norm_g[i], wq_arr=_arrange_w_uq(mla_w_uq[i]),
                 wkv_arr=mla_w_ukv[i].astype(BF16))
        w_out_b = w_out[i].astype(BF16)

        h = norm_mod(x, norm1_g[i], mod[:, 0], mod[:, 1]).reshape(b * n, d)
        hc = norm_mod(xc, norm1_g[i], mod_c[:, 0], mod_c[:, 1]).reshape(b * lc, d)
        z = matmul(h, w_main, BF16, 1024, 1024).reshape(b, n, Z_COLS)
        zg = matmul(h, w_logit, F32, 1024, LANE).reshape(b, n, LANE)
        zc = matmul(hc, w_main, BF16, 1024, 1024).reshape(b, lc, Z_COLS)
        zgc = matmul(hc, w_logit, F32, 1024, LANE).reshape(b, lc, LANE)

        kv_zero = (jnp.zeros((b, C_HEADS, n + lc, 2 * LANE), BF16), jnp.zeros((b, C_HEADS, n + lc, LANE), BF16))
        oa_c, intra_c, (q_c, k_all, v_all) = _mixer_branches(zc, zgc, p, cos_ctx, sin_ctx, kv_rows=n + lc, kv_row0=n,
                                                             kv_into=kv_zero)
        out_a, intra, (q_l, k_all, v_all) = _mixer_branches(z, zg, p, cos_lat, sin_lat, kv_rows=n + lc, kv_row0=0,
                                                            kv_into=(k_all, v_all))
        s_zero = jnp.zeros((b, 2 * B_HEADS, B_HEAD_DIM, B_HEAD_DIM), F32)
        ocf, ocb, s_ctx = deltanet_scan(*intra_c, s_zero)
        o_f, o_b, _ = deltanet_scan(*intra, s_ctx)
        out_b = deltanet_out(o_f, o_b, z, dn_norm_g[i])
        out_c = flash_attention(q_l, k_all, v_all)
        x2 = mix_out_residual(out_a.reshape(b * n, -1), out_b.reshape(b * n, -1), out_c.reshape(b * n, -1),
                              w_out_b, x.reshape(b * n, d), mod[:, 2], n, 1024, 1024)
        if not last:
            ob_c = deltanet_out(ocf, ocb, zc, dn_norm_g[i])
            oc_c = flash_attention(q_c, k_all[:, :, n:], v_all[:, :, n:])
            xc2 = mix_out_residual(oa_c.reshape(b * lc, -1), ob_c.reshape(b * lc, -1), oc_c.reshape(b * lc, -1),
                                   w_out_b, xc.reshape(b * lc, d), mod_c[:, 2], lc, 1024, 1024)

        if i % 2 == 0:
            wg, wu, wd = (ffn_w_gate[i // 2].astype(BF16), ffn_w_up[i // 2].astype(BF16), ffn_w_down[i // 2].astype(BF16))
            h2 = norm_mod(x2.reshape(b, n, d), norm2_g[i], mod[:, 3], mod[:, 4]).reshape(b * n, d)
            hid = swiglu_up(h2, wg, wu, 1024, 512)
            x = down_residual(hid, wd, x2, mod[:, 5], n, 1024, 512, wd.shape[0]).reshape(b, n, d)
            if not last:
                hc2 = norm_mod(xc2.reshape(b, lc, d), norm2_g[i], mod_c[:, 3], mod_c[:, 4]).reshape(b * lc, d)
                hid_c = swiglu_up(hc2, wg, wu, 1024, 512)
                xc = down_residual(hid_c, wd, xc2, mod_c[:, 5], lc, 1024, 512, wd.shape[0]).reshape(b, lc, d)
        else:
            e = i // 2
            router = jnp.pad(moe_router[e], ((0, 0), (0, LANE - N_EXPERTS)))
            if not last:
                raise NotImplementedError("an expert layer followed by another layer is not part of this model")
            h_rows, logits = norm_mod(x2.reshape(b, n, d), norm2_g[i], mod[:, 3], mod[:, 4], router=router)
            return moe_ffn(h_rows, logits.reshape(b * n, LANE), moe_w_gate[e], moe_w_up[e], moe_w_down[e],
                           x2, mod[:, 5], n, final_norm_g).reshape(b, n, d)
    return rmsnorm_rows(x.reshape(b * n, d), final_norm_g).reshape(b, n, d)
```
